```python
import math
import jax, jax.numpy as jnp
from jax import lax
import numpy as np


D_MODEL = 1024
BATCH = 8
SEQ = 2048
DEPTH = 2
DEC_BATCH = 32
DEC_SEQ = 8
PAST_LEN = 8192
PAGE_SIZE = 128

N_A_LAYERS = DEPTH // 2
N_B_LAYERS = DEPTH - N_A_LAYERS
EPS = 1e-6
FFN_DIM = 2688
GLA_HEADS = 4
GLA_DK = D_MODEL // 2 // GLA_HEADS
GLA_DV = D_MODEL // GLA_HEADS
GLA_RANK = 16
GLA_TAU = 16.0
GLA_CHUNK = 64
GLA_QK = GLA_HEADS * GLA_DK
GLA_V = GLA_HEADS * GLA_DV
GLA_IN = 2 * GLA_QK + GLA_V + GLA_RANK + GLA_V
HEAD_DIM = 64
N_KV_HEADS = 4
HEADS_PER_GROUP = 4
DIL_GROUPS = ((128, 1), (512, 4), (2048, 16))
N_GROUPS = len(DIL_GROUPS)
N_Q_HEADS = N_GROUPS * HEADS_PER_GROUP
MAX_WINDOW = max(w for w, _ in DIL_GROUPS)
Q_BLOCK = 128
ROPE_THETA = 10000.0

kernel_name = 'yoco_gla_dilated_window_decoder_step'


def rmsnorm(x, g):
    xf = x.astype(jnp.float32)
    y = xf * lax.rsqrt(jnp.mean(xf * xf, axis=-1, keepdims=True) + EPS)
    return (y * g.astype(jnp.float32)).astype(x.dtype)


def swiglu_ffn(x, w_in, w_out):
    g, u = jnp.split(x @ w_in, 2, axis=-1)
    return (jax.nn.silu(g) * u) @ w_out


def rope(x, pos):
    half = HEAD_DIM // 2
    inv = ROPE_THETA ** (-jnp.arange(half, dtype=jnp.float32) / half)
    ang = pos.astype(jnp.float32)[:, None] * inv[None, :]
    cos = jnp.cos(ang)[None, :, None, :]
    sin = jnp.sin(ang)[None, :, None, :]
    xf = x.astype(jnp.float32)
    x1, x2 = xf[..., :half], xf[..., half:]
    return jnp.concatenate([x1 * cos - x2 * sin, x1 * sin + x2 * cos], axis=-1).astype(x.dtype)


def gla_recurrence(q, k, v, log_a, s0):
    b, n = q.shape[:2]
    c = math.gcd(n, GLA_CHUNK)
    nc = n // c

    def to_chunks(t):
        return jnp.moveaxis(t.reshape(b, nc, c, *t.shape[2:]), 1, 0)

    causal = jnp.tril(jnp.ones((c, c), dtype=bool))

    def step(state, inp):
        qc, kc, vc, gc = inp
        qf, kf, vf = qc.astype(jnp.float32), kc.astype(jnp.float32), vc.astype(jnp.float32)
        cum = jnp.cumsum(gc.astype(jnp.float32), axis=1)
        diff = cum[:, :, None] - cum[:, None, :]
        decay = jnp.exp(jnp.where(causal[None, :, :, None, None], diff, -jnp.inf))
        attn = jnp.einsum('bihd,bjhd,bijhd->bhij', qf, kf, decay)
        o = (jnp.einsum('bhij,bjhv->bihv', attn, vf)
             + jnp.einsum('bihd,bhdv->bihv', qf * jnp.exp(cum), state))
        last = cum[:, -1]
        k_dec = kf * jnp.exp(last[:, None] - cum)
        state = jnp.exp(last)[..., None] * state + jnp.einsum('bjhd,bjhv->bhdv', k_dec, vf)
        return state, o

    s_fin, o = lax.scan(step, s0.astype(jnp.float32),
                        (to_chunks(q), to_chunks(k), to_chunks(v), to_chunks(log_a)))
    o = jnp.moveaxis(o, 0, 1).reshape(b, n, GLA_HEADS, GLA_DV)
    return o, s_fin


def gla_mixer(h, w_in, w_g2, b_g, out_norm, w_out, s0):
    b, n, _ = h.shape
    proj = h @ w_in
    q, k, v, g_lr, r = jnp.split(proj, [GLA_QK, 2 * GLA_QK, 2 * GLA_QK + GLA_V,
                                        2 * GLA_QK + GLA_V + GLA_RANK], axis=-1)
    q = q.reshape(b, n, GLA_HEADS, GLA_DK) * (GLA_DK ** -0.5)
    k = k.reshape(b, n, GLA_HEADS, GLA_DK)
    v = v.reshape(b, n, GLA_HEADS, GLA_DV)
    log_a = jax.nn.log_sigmoid((g_lr @ w_g2 + b_g).astype(jnp.float32)) / GLA_TAU
    log_a = log_a.reshape(b, n, GLA_HEADS, GLA_DK)
    o, s_fin = gla_recurrence(q, k, v, log_a, s0)
    o = rmsnorm(o, out_norm).reshape(b, n, GLA_V).astype(h.dtype)
    y = (o * jax.nn.silu(r)) @ w_out
    return y, s_fin.astype(h.dtype)


def shared_kv(x, kv_norm, kv_w, k_norm, pos):
    b, n, _ = x.shape
    k, v = jnp.split(rmsnorm(x, kv_norm) @ kv_w, 2, axis=-1)
    k = rope(rmsnorm(k.reshape(b, n, N_KV_HEADS, HEAD_DIM), k_norm), pos)
    v = v.reshape(b, n, N_KV_HEADS, HEAD_DIM)
    return k, v


def dilated_window_attention(q, k_src, v_src, n_pad):
    b, n = q.shape[:2]
    qb = math.gcd(n, Q_BLOCK)
    nb = n // qb
    scale = HEAD_DIM ** -0.5
    q_blocks = jnp.moveaxis(q.reshape(b, nb, qb, N_GROUPS, HEADS_PER_GROUP, HEAD_DIM), 1, 0)

    def block(args):
        qblk, start = args
        rows = MAX_WINDOW + start + jnp.arange(qb)
        outs, lses = [], []
        for g, (win, dil) in enumerate(DIL_GROUPS):
            nk = win // dil + 1
            idx = rows[:, None] - dil * jnp.arange(nk)[None, :]
            valid = idx >= n_pad
            kg = jnp.take(k_src, idx, axis=1).astype(jnp.float32)
            vg = jnp.take(v_src, idx, axis=1).astype(jnp.float32)
            s = jnp.einsum('bqjd,bqkjd->bqjk', qblk[:, :, g].astype(jnp.float32), kg) * scale
            s = jnp.where(valid[None, :, None, :], s, -jnp.inf)
            m = jnp.max(s, axis=-1, keepdims=True)
            p = jnp.exp(s - m)
            den = jnp.sum(p, axis=-1, keepdims=True)
            outs.append(jnp.einsum('bqjk,bqkjd->bqjd', p, vg) / den)
            lses.append((m + jnp.log(den))[..., 0])
        o = jnp.stack(outs, axis=2)
        w = jax.nn.softmax(jnp.stack(lses, axis=2), axis=2)
        return o * w[..., None]

    out = lax.map(block, (q_blocks, jnp.arange(nb, dtype=jnp.int32) * qb))
    return jnp.moveaxis(out, 0, 1).reshape(b, n, N_Q_HEADS * HEAD_DIM)


def setup_inputs(seed: int = 0) -> dict:
    key = jax.random.key(seed)
    ks = jax.random.split(key, 19)

    def nrm(k, shape, scale):
        return jax.random.normal(k, shape, jnp.float32) * scale

    win_len = min(MAX_WINDOW, PAST_LEN)
    return {
        'x_prompt': nrm(ks[0], (BATCH, SEQ, D_MODEL), 1.0),
        'x_sample': nrm(ks[1], (DEC_BATCH, DEC_SEQ, D_MODEL), 1.0),
        'state_gla': nrm(ks[2], (N_A_LAYERS, DEC_BATCH, GLA_HEADS, GLA_DK, GLA_DV), 0.5),
        'cache_k_win': nrm(ks[3], (DEC_BATCH, win_len, N_KV_HEADS, HEAD_DIM), 1.0),
        'cache_v_win': nrm(ks[4], (DEC_BATCH, win_len, N_KV_HEADS, HEAD_DIM), 1.0),
        'norm_gains': 1.0 + nrm(ks[5], (DEPTH, 3, D_MODEL), 0.05),
        'ffn_w_in': nrm(ks[6], (DEPTH, 2, D_MODEL, 2 * FFN_DIM), D_MODEL ** -0.5),
        'ffn_w_out': nrm(ks[7], (DEPTH, 2, FFN_DIM, D_MODEL), FFN_DIM ** -0.5),
        'gla_w_in': nrm(ks[8], (N_A_LAYERS, D_MODEL, GLA_IN), D_MODEL ** -0.5),
        'gla_w_gate2': nrm(ks[9], (N_A_LAYERS, GLA_RANK, GLA_QK), GLA_RANK ** -0.5),
        'gla_b_gate': 2.0 + nrm(ks[10], (N_A_LAYERS, GLA_QK), 0.5),
        'gla_out_norm': 1.0 + nrm(ks[11], (N_A_LAYERS, GLA_DV), 0.05),
        'gla_w_out': nrm(ks[12], (N_A_LAYERS, GLA_V, D_MODEL), GLA_V ** -0.5),
        'kv_norm': 1.0 + nrm(ks[13], (D_MODEL,), 0.05),
        'kv_w': nrm(ks[14], (D_MODEL, 2 * N_KV_HEADS * HEAD_DIM), D_MODEL ** -0.5),
        'k_norm': 1.0 + nrm(ks[15], (HEAD_DIM,), 0.05),
        'attn_w_q': nrm(ks[16], (N_B_LAYERS, D_MODEL, N_Q_HEADS * HEAD_DIM), D_MODEL ** -0.5),
        'q_norm': 1.0 + nrm(ks[17], (N_B_LAYERS, HEAD_DIM), 0.05),
        'attn_w_out': nrm(ks[18], (N_B_LAYERS, N_Q_HEADS * HEAD_DIM, D_MODEL), (N_Q_HEADS * HEAD_DIM) ** -0.5),
    }


def reference(x_prompt, x_sample, state_gla, cache_k_win, cache_v_win, norm_gains, ffn_w_in, ffn_w_out,
              gla_w_in, gla_w_gate2, gla_b_gate, gla_out_norm, gla_w_out, kv_norm, kv_w, k_norm,
              attn_w_q, q_norm, attn_w_out):

    def run(x, pos0, s0_all, k_buf, v_buf):
        b, n, _ = x.shape
        pos = pos0 + jnp.arange(n, dtype=jnp.int32)
        new_states = []
        k_new = v_new = k_src = v_src = None
        n_pad = 0
        for layer in range(DEPTH):
            x = x + 0.5 * swiglu_ffn(rmsnorm(x, norm_gains[layer, 0]), ffn_w_in[layer, 0], ffn_w_out[layer, 0])
            h = rmsnorm(x, norm_gains[layer, 1])
            if layer < N_A_LAYERS:
                y, s_fin = gla_mixer(h, gla_w_in[layer], gla_w_gate2[layer], gla_b_gate[layer],
                                     gla_out_norm[layer], gla_w_out[layer], s0_all[layer])
                new_states.append(s_fin)
            else:
                bl = layer - N_A_LAYERS
                q = (h @ attn_w_q[bl]).reshape(b, n, N_Q_HEADS, HEAD_DIM)
                q = rope(rmsnorm(q, q_norm[bl]), pos).reshape(b, n, N_GROUPS, HEADS_PER_GROUP, HEAD_DIM)
                y = dilated_window_attention(q, k_src, v_src, n_pad).astype(x.dtype) @ attn_w_out[bl]
            x = x + y
            x = x + 0.5 * swiglu_ffn(rmsnorm(x, norm_gains[layer, 2]), ffn_w_in[layer, 1], ffn_w_out[layer, 1])
            if layer == N_A_LAYERS - 1:
                k_new, v_new = shared_kv(x, kv_norm, kv_w, k_norm, pos)
                n_pad = MAX_WINDOW - k_buf.shape[1]
                pad = jnp.zeros((b, n_pad, N_KV_HEADS, HEAD_DIM), x.dtype)
                k_src = jnp.concatenate([pad, k_buf.astype(x.dtype), k_new], axis=1)
                v_src = jnp.concatenate([pad, v_buf.astype(x.dtype), v_new], axis=1)
        return x, jnp.stack(new_states, axis=0), k_new, v_new

    s0_prompt = jnp.zeros((N_A_LAYERS, BATCH, GLA_HEADS, GLA_DK, GLA_DV), x_prompt.dtype)
    empty = jnp.zeros((BATCH, 0, N_KV_HEADS, HEAD_DIM), x_prompt.dtype)
    y_prompt, state_gla_prompt, k_p, v_p = run(x_prompt, 0, s0_prompt, empty, empty)
    y_sample, state_gla_sample, k_s, v_s = run(x_sample, PAST_LEN, state_gla, cache_k_win, cache_v_win)
    keep = min(MAX_WINDOW, SEQ)
    k_win_prompt = k_p[:, SEQ - keep:]
    v_win_prompt = v_p[:, SEQ - keep:]
    return (y_prompt, y_sample, state_gla_prompt, state_gla_sample, k_win_prompt, v_win_prompt, k_s, v_s)
```

```python
import functools

import numpy as np
import jax
import jax.numpy as jnp
from jax import lax
from jax.experimental import pallas as pl
from jax.experimental.pallas import tpu as pltpu

F32 = jnp.float32
BF16 = jnp.bfloat16

D_MODEL = 1024
FFN_DIM = 2688
EPS = 1e-6
PAST_LEN = 8192
GLA_HEADS = 4
GLA_DK = 128
GLA_DV = 256
GLA_RANK = 16
GLA_TAU = 16.0
GLA_QK = GLA_HEADS * GLA_DK
GLA_V = GLA_HEADS * GLA_DV
HEAD_DIM = 64
N_KV_HEADS = 4
KV_DIM = N_KV_HEADS * HEAD_DIM
DIL_GROUPS = ((128, 1), (512, 4), (2048, 16))
N_GROUPS = len(DIL_GROUPS)
MAX_WINDOW = max(w for w, _ in DIL_GROUPS)
ROPE_THETA = 10000.0

LANES = 128
GLA_CHUNK = 128
ATTN_QBLOCK = 128
ROW_TILE = 512
FFN_CHUNKS = ((0, 896), (896, 1792), (1792, 2688))
VMEM_LIMIT = 52 * 1024 * 1024


def _dot(a, b):
    return jnp.dot(a, b, preferred_element_type=F32)


def _dot_nt(a, b):
    return lax.dot_general(a, b, (((1,), (1,)), ((), ())), preferred_element_type=F32)


def _dot_tn(a, b):
    return lax.dot_general(a, b, (((0,), (0,)), ((), ())), preferred_element_type=F32)


def _rms(x, g):
    ms = jnp.mean(x * x, axis=-1, keepdims=True)
    return x * lax.rsqrt(ms + EPS) * g


def _silu(x):
    return x * jax.nn.sigmoid(x)


def _group_mean_sq(x, avg):
    sq = x * x
    hi = sq.astype(BF16)
    lo = (sq - hi.astype(F32)).astype(BF16)
    return _dot(hi, avg) + _dot(lo, avg)


def _rope(x, cos, sin_signed):
    w = x.shape[-1]
    lane = lax.broadcasted_iota(jnp.int32, (1, w), 1)
    first_half = (lane % HEAD_DIM) < (HEAD_DIM // 2)
    rot = jnp.where(first_half, pltpu.roll(x, w - HEAD_DIM // 2, 1), pltpu.roll(x, HEAD_DIM // 2, 1))
    return x * cos + rot * sin_signed


def _ffn_kernel(x_ref, g_ref, win_ref, wout_ref, o_ref):
    x = x_ref[...]
    h = _rms(x, g_ref[...]).astype(BF16)
    acc = None
    for c0, c1 in FFN_CHUNKS:
        gate = _dot(h, win_ref[:, c0:c1])
        up = _dot(h, win_ref[:, FFN_DIM + c0:FFN_DIM + c1])
        act = (_silu(gate) * up).astype(BF16)
        part = _dot(act, wout_ref[c0:c1, :])
        acc = part if acc is None else acc + part
    o_ref[...] = x + 0.5 * acc


def _gla_proj_kernel(x_ref, g_ref, wq_ref, wk_ref, wv_ref, wr_ref, wg_ref, wg2_ref, bg_ref,
                     q_ref, k_ref, v_ref, r_ref, la_ref):
    h = _rms(x_ref[...], g_ref[...]).astype(BF16)
    q_ref[...] = _dot(h, wq_ref[...]) * (GLA_DK ** -0.5)
    k_ref[...] = _dot(h, wk_ref[...])
    v_ref[...] = _dot(h, wv_ref[...]).astype(BF16)
    r_ref[...] = _dot(h, wr_ref[...])
    g_lr = _dot(h, wg_ref[...])
    z = _dot(g_lr.astype(BF16), wg2_ref[...]) + bg_ref[...]
    log_sig = jnp.minimum(z, 0.0) - jnp.log1p(jnp.exp(-jnp.abs(z)))
    la_ref[...] = log_sig * (1.0 / GLA_TAU)


def _gla_out_kernel(o_ref, r_ref, x_ref, gn_ref, wout_ref, y_ref):
    o = o_ref[...]
    parts = []
    for h in range(GLA_HEADS):
        parts.append(_rms(o[:, h * GLA_DV:(h + 1) * GLA_DV], gn_ref[...]))
    on = jnp.concatenate(parts, axis=1)
    act = (on * _silu(r_ref[...])).astype(BF16)
    y_ref[...] = x_ref[...] + _dot(act, wout_ref[...])


def _kv_kernel(x_ref, g_ref, w_ref, kn_ref, cos_ref, sin_ref, avg_ref, k_ref, v_ref):
    h = _rms(x_ref[...], g_ref[...]).astype(BF16)
    kv = _dot(h, w_ref[...])
    k = kv[:, :KV_DIM]
    kn = k * lax.rsqrt(_group_mean_sq(k, avg_ref[...]) + EPS) * kn_ref[...]
    k_ref[...] = _rope(kn, cos_ref[...], sin_ref[...])
    v_ref[...] = kv[:, KV_DIM:]


def _q_kernel(x_ref, g_ref, w_ref, qn_ref, cos_ref, sin_ref, avg_ref, q0_ref, q1_ref, q2_ref):
    h = _rms(x_ref[...], g_ref[...]).astype(BF16)
    for g, q_ref in enumerate((q0_ref, q1_ref, q2_ref)):
        q = _dot(h, w_ref[:, g * KV_DIM:(g + 1) * KV_DIM])
        qn = q * lax.rsqrt(_group_mean_sq(q, avg_ref[...]) + EPS) * qn_ref[...]
        q_ref[...] = _rope(qn, cos_ref[...], sin_ref[...]) * (HEAD_DIM ** -0.5)


def _attn_out_kernel(o0_ref, o1_ref, o2_ref, x_ref, w_ref, y_ref):
    acc = x_ref[...]
    for g, o_ref in enumerate((o0_ref, o1_ref, o2_ref)):
        acc = acc + _dot(o_ref[...].astype(BF16), w_ref[g * KV_DIM:(g + 1) * KV_DIM, :])
    y_ref[...] = acc


def _gla_rec_kernel(*refs, chunk, has_state):
    if has_state:
        q_ref, k_ref, v_ref, la_ref, s0_ref, o_ref, st_ref = refs
    else:
        q_ref, k_ref, v_ref, la_ref, o_ref, st_ref = refs
    c = pl.program_id(1)

    @pl.when(c == 0)
    def _():
        if has_state:
            st_ref[...] = s0_ref[...]
        else:
            st_ref[...] = jnp.zeros_like(st_ref)

    q = q_ref[0]
    k = k_ref[0]
    la = la_ref[0]
    heads = [slice(h * GLA_DK, (h + 1) * GLA_DK) for h in range(GLA_HEADS)]
    row = lax.broadcasted_iota(jnp.int32, (chunk, 1), 0)
    ri = lax.broadcasted_iota(jnp.int32, (chunk, chunk), 0)
    ci = lax.broadcasted_iota(jnp.int32, (chunk, chunk), 1)

    qb = q.astype(BF16)
    kb = k.astype(BF16)
    attn = [jnp.where(ri == ci, _dot_nt(qb[:, hs], kb[:, hs]), 0.0) for hs in heads]

    lsum = la
    rsum = jnp.zeros_like(la)
    s = 1
    while s < chunk:
        qt = (q * jnp.exp(lsum)).astype(BF16)
        kt = (k * jnp.exp(rsum)).astype(BF16)
        mask = ((ri ^ ci) < 2 * s) & ((ri & s) != 0) & ((ci & s) == 0)
        for h, hs in enumerate(heads):
            attn[h] = attn[h] + jnp.where(mask, _dot_nt(qt[:, hs], kt[:, hs]), 0.0)
        btot = lsum + rsum
        odd = (row & s) != 0
        lsum = lsum + jnp.where(odd, pltpu.roll(btot, s, 0), 0.0)
        rsum = rsum + jnp.where(odd, 0.0, pltpu.roll(btot, chunk - s, 0))
        s *= 2

    qt = (q * jnp.exp(lsum)).astype(BF16)
    kd = (k * jnp.exp(rsum)).astype(BF16)
    total = lsum[0:1, :] + rsum[0:1, :]
    for h, hs in enumerate(heads):
        vs = slice(h * GLA_DV, (h + 1) * GLA_DV)
        v = v_ref[0, :, vs]
        st = st_ref[0, h]
        o_ref[0, :, vs] = _dot(attn[h].astype(BF16), v) + _dot(qt[:, hs], st.astype(BF16))
        tot_col = jnp.broadcast_to(total[:, hs], (GLA_DK, GLA_DK)).T[:, 0:1]
        st_ref[0, h] = jnp.exp(tot_col) * st + _dot_tn(kd[:, hs], v)


def _pair_masks():
    lane = lax.broadcasted_iota(jnp.int32, (1, LANES), 1)
    first = lane < HEAD_DIM
    return first, jnp.logical_not(first)


def _attn_prompt_kernel(q0_ref, q1_ref, q2_ref, k_ref, v_ref, o0_ref, o1_ref, o2_ref, lse_ref, *, n):
    q_refs = (q0_ref, q1_ref, q2_ref)
    o_refs = (o0_ref, o1_ref, o2_ref)
    head_masks = _pair_masks()

    for g, (win, dil) in enumerate(DIL_GROUPS):
        length = n // dil
        qblk = min(ATTN_QBLOCK, length)
        nblk = length // qblk
        nkeys = min(2 * qblk, length)
        reach = win // dil

        def tile(idx, carry, g=g, dil=dil, qblk=qblk, nkeys=nkeys, reach=reach):
            rho = idx % dil
            blk = idx // dil
            u0 = jnp.maximum(blk - 1, 0) * qblk
            q_rows = pl.ds(blk * qblk * dil + rho, qblk, stride=dil)
            k_rows = pl.ds(u0 * dil + rho, nkeys, stride=dil)
            qt = q_refs[g][0, q_rows, :]
            kt = k_ref[0, k_rows, :].astype(BF16)
            vt = v_ref[0, k_rows, :].astype(BF16)
            uq = blk * qblk + lax.broadcasted_iota(jnp.int32, (qblk, 1), 0)
            uk = u0 + lax.broadcasted_iota(jnp.int32, (1, nkeys), 1)
            delta = uq - uk
            valid = (delta >= 0) & (delta <= reach)
            outs, lses = [], []
            for hm in head_masks:
                s = _dot_nt(jnp.where(hm, qt, 0.0).astype(BF16), kt)
                s = jnp.where(valid, s, -jnp.inf)
                m = jnp.max(s, axis=-1, keepdims=True)
                p = jnp.exp(s - m)
                den = jnp.sum(p, axis=-1, keepdims=True)
                outs.append(_dot(p.astype(BF16), vt) / den)
                lses.append(m + jnp.log(den))
            o_refs[g][0, q_rows, :] = jnp.where(head_masks[0], outs[0], outs[1])
            lse_ref[g, q_rows, :] = jnp.where(head_masks[0], lses[0], lses[1])
            return carry

        lax.fori_loop(0, dil * nblk, tile, 0)

    rows_per_step = 256

    def reweight(t, carry):
        rows = pl.ds(pl.multiple_of(t * rows_per_step, rows_per_step), rows_per_step)
        lse = [lse_ref[g, rows, :] for g in range(N_GROUPS)]
        top = jnp.maximum(jnp.maximum(lse[0], lse[1]), lse[2])
        e = [jnp.exp(l - top) for l in lse]
        tot = e[0] + e[1] + e[2]
        for g in range(N_GROUPS):
            o_refs[g][0, rows, :] = o_refs[g][0, rows, :] * (e[g] / tot)
        return carry

    lax.fori_loop(0, n // rows_per_step, reweight, 0)


def _attn_sample_kernel(q0_ref, q1_ref, q2_ref, kn_ref, vn_ref, kc_ref, vc_ref, o0_ref, o1_ref, o2_ref,
                        *, n, cache_len):
    q_refs = (q0_ref, q1_ref, q2_ref)
    o_refs = (o0_ref, o1_ref, o2_ref)
    head_masks = _pair_masks()
    tq = lax.broadcasted_iota(jnp.int32, (n, 1), 0)
    d_cache = (cache_len + tq) - lax.broadcasted_iota(jnp.int32, (1, cache_len), 1)
    d_new = tq - lax.broadcasted_iota(jnp.int32, (1, n), 1)

    for pair in range(KV_DIM // LANES):
        ls = slice(pair * LANES, (pair + 1) * LANES)
        kc = kc_ref[0, :, ls].astype(BF16)
        vc = vc_ref[0, :, ls].astype(BF16)
        kn = kn_ref[0, :, ls].astype(BF16)
        vn = vn_ref[0, :, ls].astype(BF16)
        weighted = []
        for hm in head_masks:
            outs, lses = [], []
            for g, (win, dil) in enumerate(DIL_GROUPS):
                qm = jnp.where(hm, q_refs[g][0, :, ls], 0.0).astype(BF16)
                valid_c = ((d_cache & (dil - 1)) == 0) & (d_cache <= win)
                valid_n = (d_new >= 0) & ((d_new & (dil - 1)) == 0) & (d_new <= win)
                sc = jnp.where(valid_c, _dot_nt(qm, kc), -jnp.inf)
                sn = jnp.where(valid_n, _dot_nt(qm, kn), -jnp.inf)
                m = jnp.maximum(jnp.max(sc, axis=-1, keepdims=True), jnp.max(sn, axis=-1, keepdims=True))
                pc = jnp.exp(sc - m)
                pn = jnp.exp(sn - m)
                den = jnp.sum(pc, axis=-1, keepdims=True) + jnp.sum(pn, axis=-1, keepdims=True)
                outs.append((_dot(pc.astype(BF16), vc) + _dot(pn.astype(BF16), vn)) / den)
                lses.append(m + jnp.log(den))
            top = jnp.maximum(jnp.maximum(lses[0], lses[1]), lses[2])
            e = [jnp.exp(l - top) for l in lses]
            tot = e[0] + e[1] + e[2]
            weighted.append([outs[g] * (e[g] / tot) for g in range(N_GROUPS)])
        for g in range(N_GROUPS):
            o_refs[g][0, :, ls] = jnp.where(head_masks[0], weighted[0][g], weighted[1][g])


def _params(semantics):
    return pltpu.CompilerParams(dimension_semantics=semantics, vmem_limit_bytes=VMEM_LIMIT)


def _row_spec(tm, width):
    return pl.BlockSpec((tm, width), lambda i: (i, 0))


def _const_spec(shape):
    return pl.BlockSpec(shape, lambda i: (0,) * len(shape), pipeline_mode=pl.Buffered(1))


def _row_tile(rows):
    return ROW_TILE if rows % ROW_TILE == 0 else rows


def _ffn_call(x, gain, w_in, w_out):
    rows = x.shape[0]
    tm = _row_tile(rows)
    return pl.pallas_call(
        _ffn_kernel,
        grid=(rows // tm,),
        in_specs=[_row_spec(tm, D_MODEL), _const_spec((1, D_MODEL)),
                  _const_spec(w_in.shape), _const_spec(w_out.shape)],
        out_specs=_row_spec(tm, D_MODEL),
        out_shape=jax.ShapeDtypeStruct((rows, D_MODEL), F32),
        compiler_params=_params(("parallel",)),
        name="ffn",
    )(x, gain, w_in, w_out)


def _gla_proj_call(x, gain, wq, wk, wv, wr, wg, wg2, bg):
    rows = x.shape[0]
    tm = _row_tile(rows)
    consts = (gain, wq, wk, wv, wr, wg, wg2, bg)
    return pl.pallas_call(
        _gla_proj_kernel,
        grid=(rows // tm,),
        in_specs=[_row_spec(tm, D_MODEL)] + [_const_spec(a.shape) for a in consts],
        out_specs=[_row_spec(tm, GLA_QK), _row_spec(tm, GLA_QK), _row_spec(tm, GLA_V),
                   _row_spec(tm, GLA_V), _row_spec(tm, GLA_QK)],
        out_shape=[jax.ShapeDtypeStruct((rows, GLA_QK), F32), jax.ShapeDtypeStruct((rows, GLA_QK), F32),
                   jax.ShapeDtypeStruct((rows, GLA_V), BF16), jax.ShapeDtypeStruct((rows, GLA_V), F32),
                   jax.ShapeDtypeStruct((rows, GLA_QK), F32)],
        compiler_params=_params(("parallel",)),
        name="gla_proj",
    )(x, *consts)


def _gla_rec_call(q, k, v, la, s0):
    batch, n, _ = q.shape
    chunk = min(GLA_CHUNK, n)
    assert n % chunk == 0 and chunk & (chunk - 1) == 0
    has_state = s0 is not None
    seq_spec = lambda width: pl.BlockSpec((1, chunk, width), lambda b, c: (b, c, 0))
    state_spec = pl.BlockSpec((1, GLA_HEADS, GLA_DK, GLA_DV), lambda b, c: (b, 0, 0, 0))
    in_specs = [seq_spec(GLA_QK), seq_spec(GLA_QK), seq_spec(GLA_V), seq_spec(GLA_QK)]
    args = [q, k, v, la]
    if has_state:
        in_specs.append(state_spec)
        args.append(s0)
    return pl.pallas_call(
        functools.partial(_gla_rec_kernel, chunk=chunk, has_state=has_state),
        grid=(batch, n // chunk),
        in_specs=in_specs,
        out_specs=[seq_spec(GLA_V), state_spec],
        out_shape=[jax.ShapeDtypeStruct((batch, n, GLA_V), F32),
                   jax.ShapeDtypeStruct((batch, GLA_HEADS, GLA_DK, GLA_DV), F32)],
        compiler_params=_params(("parallel", "arbitrary")),
        name="gla_rec",
    )(*args)


def _gla_out_call(o, r, x, gn, w_out):
    rows = x.shape[0]
    tm = _row_tile(rows)
    return pl.pallas_call(
        _gla_out_kernel,
        grid=(rows // tm,),
        in_specs=[_row_spec(tm, GLA_V), _row_spec(tm, GLA_V), _row_spec(tm, D_MODEL),
                  _const_spec(gn.shape), _const_spec(w_out.shape)],
        out_specs=_row_spec(tm, D_MODEL),
        out_shape=jax.ShapeDtypeStruct((rows, D_MODEL), F32),
        compiler_params=_params(("parallel",)),
        name="gla_out",
    )(o, r, x, gn, w_out)


def _table_spec(tm, n_rows_table):
    period = n_rows_table // tm
    return pl.BlockSpec((tm, KV_DIM), lambda i: (i % period, 0))


def _kv_call(x, gain, w, kn, cos, sin, avg):
    rows = x.shape[0]
    tm = min(_row_tile(rows), cos.shape[0])
    return pl.pallas_call(
        _kv_kernel,
        grid=(rows // tm,),
        in_specs=[_row_spec(tm, D_MODEL), _const_spec(gain.shape), _const_spec(w.shape), _const_spec(kn.shape),
                  _table_spec(tm, cos.shape[0]), _table_spec(tm, cos.shape[0]), _const_spec(avg.shape)],
        out_specs=[_row_spec(tm, KV_DIM), _row_spec(tm, KV_DIM)],
        out_shape=[jax.ShapeDtypeStruct((rows, KV_DIM), F32)] * 2,
        compiler_params=_params(("parallel",)),
        name="shared_kv",
    )(x, gain, w, kn, cos, sin, avg)


def _q_call(x, gain, w, qn, cos, sin, avg):
    rows = x.shape[0]
    tm = min(_row_tile(rows), cos.shape[0])
    return pl.pallas_call(
        _q_kernel,
        grid=(rows // tm,),
        in_specs=[_row_spec(tm, D_MODEL), _const_spec(gain.shape), _const_spec(w.shape), _const_spec(qn.shape),
                  _table_spec(tm, cos.shape[0]), _table_spec(tm, cos.shape[0]), _const_spec(avg.shape)],
        out_specs=[_row_spec(tm, KV_DIM)] * N_GROUPS,
        out_shape=[jax.ShapeDtypeStruct((rows, KV_DIM), F32)] * N_GROUPS,
        compiler_params=_params(("parallel",)),
        name="attn_q",
    )(x, gain, w, qn, cos, sin, avg)


def _attn_prompt_call(qs, k, v):
    batch, n, _ = k.shape
    assert n % (ATTN_QBLOCK * max(d for _, d in DIL_GROUPS)) == 0
    spec = pl.BlockSpec((1, n, LANES), lambda b, p: (b, 0, p))
    return pl.pallas_call(
        functools.partial(_attn_prompt_kernel, n=n),
        grid=(batch, KV_DIM // LANES),
        in_specs=[spec] * (N_GROUPS + 2),
        out_specs=[spec] * N_GROUPS,
        out_shape=[jax.ShapeDtypeStruct((batch, n, KV_DIM), F32)] * N_GROUPS,
        scratch_shapes=[pltpu.VMEM((N_GROUPS, n, LANES), F32)],
        compiler_params=_params(("parallel", "parallel")),
        name="attn_prompt",
    )(*qs, k, v)


def _attn_sample_call(qs, k_new, v_new, k_cache, v_cache):
    batch, n, _ = k_new.shape
    cache_len = k_cache.shape[1]
    new_spec = pl.BlockSpec((1, n, KV_DIM), lambda b: (b, 0, 0))
    cache_spec = pl.BlockSpec((1, cache_len, KV_DIM), lambda b: (b, 0, 0))
    return pl.pallas_call(
        functools.partial(_attn_sample_kernel, n=n, cache_len=cache_len),
        grid=(batch,),
        in_specs=[new_spec] * (N_GROUPS + 2) + [cache_spec] * 2,
        out_specs=[new_spec] * N_GROUPS,
        out_shape=[jax.ShapeDtypeStruct((batch, n, KV_DIM), F32)] * N_GROUPS,
        compiler_params=_params(("parallel",)),
        name="attn_sample",
    )(*qs, k_new, v_new, k_cache, v_cache)


def _attn_out_call(os_, x, w):
    rows = x.shape[0]
    tm = _row_tile(rows)
    return pl.pallas_call(
        _attn_out_kernel,
        grid=(rows // tm,),
        in_specs=[_row_spec(tm, KV_DIM)] * N_GROUPS + [_row_spec(tm, D_MODEL), _const_spec(w.shape)],
        out_specs=_row_spec(tm, D_MODEL),
        out_shape=jax.ShapeDtypeStruct((rows, D_MODEL), F32),
        compiler_params=_params(("parallel",)),
        name="attn_out",
    )(*os_, x, w)


def _rope_tables(pos0, n, reps):
    half = HEAD_DIM // 2
    inv = ROPE_THETA ** (-np.arange(half, dtype=np.float64) / half)
    ang = (pos0 + np.arange(n, dtype=np.float64))[:, None] * inv[None, :]
    cos = np.concatenate([np.cos(ang), np.cos(ang)], axis=-1)
    sin = np.concatenate([-np.sin(ang), np.sin(ang)], axis=-1)
    cos = np.tile(cos, (reps, N_KV_HEADS)).astype(np.float32)
    sin = np.tile(sin, (reps, N_KV_HEADS)).astype(np.float32)
    return jnp.asarray(cos), jnp.asarray(sin)


def _head_mean_matrix():
    idx = np.arange(KV_DIM) // HEAD_DIM
    return jnp.asarray((idx[:, None] == idx[None, :]).astype(np.float32) / HEAD_DIM, dtype=BF16)


def _run_group(x, pos0, s0, caches, w):
    batch, n, _ = x.shape
    rows = batch * n
    x = x.reshape(rows, D_MODEL)
    reps = 1 if n % ROW_TILE == 0 else batch
    cos, sin = _rope_tables(pos0, n, reps)
    avg = _head_mean_matrix()

    x = _ffn_call(x, w["gain"][0][0], w["ffn_in"][0][0], w["ffn_out"][0][0])
    q, k, v, r, la = _gla_proj_call(x, w["gain"][0][1], *w["gla_in"], w["gla_g2"], w["gla_bg"])
    seq = lambda a: a.reshape(batch, n, a.shape[-1])
    o, s_fin = _gla_rec_call(seq(q), seq(k), seq(v), seq(la), s0)
    x = _gla_out_call(o.reshape(rows, GLA_V), r, x, w["gla_norm"], w["gla_out"])
    x = _ffn_call(x, w["gain"][0][2], w["ffn_in"][0][1], w["ffn_out"][0][1])

    k_new, v_new = _kv_call(x, w["kv_gain"], w["kv_w"], w["k_norm"], cos, sin, avg)

    x = _ffn_call(x, w["gain"][1][0], w["ffn_in"][1][0], w["ffn_out"][1][0])
    qs = _q_call(x, w["gain"][1][1], w["attn_q"], w["q_norm"], cos, sin, avg)
    qs = [seq(a) for a in qs]
    if caches is None:
        os_ = _attn_prompt_call(qs, seq(k_new), seq(v_new))
    else:
        os_ = _attn_sample_call(qs, seq(k_new), seq(v_new), *caches)
    x = _attn_out_call([a.reshape(rows, KV_DIM) for a in os_], x, w["attn_out"])
    x = _ffn_call(x, w["gain"][1][2], w["ffn_in"][1][1], w["ffn_out"][1][1])

    kv_shape = (batch, n, N_KV_HEADS, HEAD_DIM)
    return x.reshape(batch, n, D_MODEL), s_fin[None], k_new.reshape(kv_shape), v_new.reshape(kv_shape)


def kernel(x_prompt, x_sample, state_gla, cache_k_win, cache_v_win, norm_gains, ffn_w_in, ffn_w_out,
           gla_w_in, gla_w_gate2, gla_b_gate, gla_out_norm, gla_w_out, kv_norm, kv_w, k_norm,
           attn_w_q, q_norm, attn_w_out):
    assert norm_gains.shape[0] == 2 and gla_w_in.shape[0] == 1 and attn_w_q.shape[0] == 1
    row = lambda a: a.reshape(1, -1)
    gw = gla_w_in[0]
    cuts = (0, GLA_QK, 2 * GLA_QK, 2 * GLA_QK + GLA_V, 2 * GLA_QK + GLA_V + GLA_RANK, gw.shape[1])
    wq, wk, wv, wg, wr = (gw[:, a:b].astype(BF16) for a, b in zip(cuts[:-1], cuts[1:]))
    pad = LANES - GLA_RANK
    w = {
        "gain": [[row(norm_gains[l, i]) for i in range(3)] for l in range(2)],
        "ffn_in": [[ffn_w_in[l, i].astype(BF16) for i in range(2)] for l in range(2)],
        "ffn_out": [[ffn_w_out[l, i].astype(BF16) for i in range(2)] for l in range(2)],
        "gla_in": (wq, wk, wv, wr, jnp.pad(wg, ((0, 0), (0, pad)))),
        "gla_g2": jnp.pad(gla_w_gate2[0].astype(BF16), ((0, pad), (0, 0))),
        "gla_bg": row(gla_b_gate[0]),
        "gla_norm": row(gla_out_norm[0]),
        "gla_out": gla_w_out[0].astype(BF16),
        "kv_gain": row(kv_norm),
        "kv_w": kv_w.astype(BF16),
        "k_norm": row(jnp.tile(k_norm, N_KV_HEADS)),
        "attn_q": attn_w_q[0].astype(BF16),
        "q_norm": row(jnp.tile(q_norm[0], N_KV_HEADS)),
        "attn_out": attn_w_out[0].astype(BF16),
    }
    cache_len = cache_k_win.shape[1]
    caches = (cache_k_win.reshape(-1, cache_len, KV_DIM), cache_v_win.reshape(-1, cache_len, KV_DIM))

    y_p, s_p, k_p, v_p = _run_group(x_prompt, 0, None, None, w)
    y_s, s_s, k_s, v_s = _run_group(x_sample, PAST_LEN, state_gla[0], caches, w)
    keep = min(MAX_WINDOW, x_prompt.shape[1])
    return (y_p, y_s, s_p, s_s, k_p[:, -keep:], v_p[:, -keep:], k_s, v_s)
```

```python
import functools

import numpy as np
import jax
import jax.numpy as jnp
from jax import lax
from jax.experimental import pallas as pl
from jax.experimental.pallas import tpu as pltpu

F32 = jnp.float32
BF16 = jnp.bfloat16

D_MODEL = 1024
FFN_DIM = 2688
EPS = 1e-6
PAST_LEN = 8192
GLA_HEADS = 4
GLA_DK = 128
GLA_DV = 256
GLA_RANK = 16
GLA_TAU = 16.0
GLA_QK = GLA_HEADS * GLA_DK
GLA_V = GLA_HEADS * GLA_DV
HEAD_DIM = 64
N_KV_HEADS = 4
KV_DIM = N_KV_HEADS * HEAD_DIM
DIL_GROUPS = ((128, 1), (512, 4), (2048, 16))
N_GROUPS = len(DIL_GROUPS)
MAX_WINDOW = max(w for w, _ in DIL_GROUPS)
ROPE_THETA = 10000.0

LANES = 128
GLA_CHUNK = 128
ATTN_QBLOCK = 128
ROW_TILE = 512
MXU_TILE = 256
FFN_MAIN = (FFN_DIM // MXU_TILE) * MXU_TILE
FFN_REM = FFN_DIM - FFN_MAIN
FFN_CHUNK = 512
ATTN_SPLIT = 4
ATTN_UNROLL = 16
VMEM_LIMIT = 52 * 1024 * 1024


def _dot(a, b):
    return jnp.dot(a, b, preferred_element_type=F32)


def _dot_nt(a, b):
    return lax.dot_general(a, b, (((1,), (1,)), ((), ())), preferred_element_type=F32)


def _dot_tn(a, b):
    return lax.dot_general(a, b, (((0,), (0,)), ((), ())), preferred_element_type=F32)


def _rms(x, g):
    ms = jnp.mean(x * x, axis=-1, keepdims=True)
    return x * lax.rsqrt(ms + EPS) * g


def _silu(x):
    return x * jax.nn.sigmoid(x)


def _group_mean_sq(x, avg):
    sq = x * x
    hi = sq.astype(BF16)
    lo = (sq - hi.astype(F32)).astype(BF16)
    return _dot(hi, avg) + _dot(lo, avg)


def _rope(x, cos, sin_signed):
    w = x.shape[-1]
    lane = lax.broadcasted_iota(jnp.int32, (1, w), 1)
    first_half = (lane % HEAD_DIM) < (HEAD_DIM // 2)
    rot = jnp.where(first_half, pltpu.roll(x, w - HEAD_DIM // 2, 1), pltpu.roll(x, HEAD_DIM // 2, 1))
    return x * cos + rot * sin_signed


def _ffn_kernel(x_ref, g_ref, win_ref, wout_ref, o_ref):
    x = x_ref[...]
    h = _rms(x, g_ref[...]).astype(BF16)
    acc = None
    for c0 in range(0, FFN_MAIN, FFN_CHUNK):
        c1 = c0 + FFN_CHUNK
        gate = _dot(h, win_ref[:, c0:c1])
        up = _dot(h, win_ref[:, FFN_MAIN + c0:FFN_MAIN + c1])
        act = (_silu(gate) * up).astype(BF16)
        part = _dot(act, wout_ref[c0:c1, :])
        acc = part if acc is None else acc + part
    if FFN_REM:
        gate_up = _dot(h, win_ref[:, 2 * FFN_MAIN:])
        act = (_silu(gate_up[:, :FFN_REM]) * gate_up[:, FFN_REM:]).astype(BF16)
        acc = acc + _dot(act, wout_ref[FFN_MAIN:, :])
    o_ref[...] = x + 0.5 * acc


def _gla_proj_kernel(x_ref, g_ref, wq_ref, wk_ref, wv_ref, wr_ref, wg_ref, wg2_ref, bg_ref,
                     q_ref, k_ref, v_ref, r_ref, la_ref):
    h = _rms(x_ref[...], g_ref[...]).astype(BF16)
    q_ref[...] = _dot(h, wq_ref[...]) * (GLA_DK ** -0.5)
    k_ref[...] = _dot(h, wk_ref[...])
    v_ref[...] = _dot(h, wv_ref[...]).astype(BF16)
    r_ref[...] = _dot(h, wr_ref[...])
    g_lr = _dot(h, wg_ref[...])
    z = _dot(g_lr.astype(BF16), wg2_ref[...]) + bg_ref[...]
    log_sig = jnp.minimum(z, 0.0) - jnp.log1p(jnp.exp(-jnp.abs(z)))
    la_ref[...] = log_sig * (1.0 / GLA_TAU)


def _gla_out_kernel(o_ref, r_ref, x_ref, gn_ref, wout_ref, y_ref):
    o = o_ref[...]
    parts = []
    for h in range(GLA_HEADS):
        parts.append(_rms(o[:, h * GLA_DV:(h + 1) * GLA_DV], gn_ref[...]))
    on = jnp.concatenate(parts, axis=1)
    act = (on * _silu(r_ref[...])).astype(BF16)
    y_ref[...] = x_ref[...] + _dot(act, wout_ref[...])


def _kv_kernel(x_ref, g_ref, w_ref, kn_ref, cos_ref, sin_ref, avg_ref, k_ref, v_ref):
    h = _rms(x_ref[...], g_ref[...]).astype(BF16)
    kv = _dot(h, w_ref[...])
    k = kv[:, :KV_DIM]
    kn = k * lax.rsqrt(_group_mean_sq(k, avg_ref[...]) + EPS) * kn_ref[...]
    k_ref[...] = _rope(kn, cos_ref[...], sin_ref[...])
    v_ref[...] = kv[:, KV_DIM:]


def _q_kernel(x_ref, g_ref, w_ref, qn_ref, cos_ref, sin_ref, avg_ref, q0_ref, q1_ref, q2_ref):
    h = _rms(x_ref[...], g_ref[...]).astype(BF16)
    for g, q_ref in enumerate((q0_ref, q1_ref, q2_ref)):
        q = _dot(h, w_ref[:, g * KV_DIM:(g + 1) * KV_DIM])
        qn = q * lax.rsqrt(_group_mean_sq(q, avg_ref[...]) + EPS) * qn_ref[...]
        q_ref[...] = _rope(qn, cos_ref[...], sin_ref[...]) * (HEAD_DIM ** -0.5)


def _attn_out_kernel(o0_ref, o1_ref, o2_ref, x_ref, w_ref, y_ref):
    acc = x_ref[...]
    for g, o_ref in enumerate((o0_ref, o1_ref, o2_ref)):
        acc = acc + _dot(o_ref[...].astype(BF16), w_ref[g * KV_DIM:(g + 1) * KV_DIM, :])
    y_ref[...] = acc


def _gla_rec_kernel(*refs, chunk, has_state):
    if has_state:
        q_ref, k_ref, v_ref, la_ref, s0_ref, o_ref, st_ref = refs
    else:
        q_ref, k_ref, v_ref, la_ref, o_ref, st_ref = refs
    c = pl.program_id(1)

    @pl.when(c == 0)
    def _():
        if has_state:
            st_ref[...] = s0_ref[...]
        else:
            st_ref[...] = jnp.zeros_like(st_ref)

    q = q_ref[0]
    k = k_ref[0]
    la = la_ref[0]
    heads = [slice(h * GLA_DK, (h + 1) * GLA_DK) for h in range(GLA_HEADS)]
    row = lax.broadcasted_iota(jnp.int32, (chunk, 1), 0)
    ri = lax.broadcasted_iota(jnp.int32, (chunk, chunk), 0)
    ci = lax.broadcasted_iota(jnp.int32, (chunk, chunk), 1)

    qb = q.astype(BF16)
    kb = k.astype(BF16)
    attn = [jnp.where(ri == ci, _dot_nt(qb[:, hs], kb[:, hs]), 0.0) for hs in heads]

    lsum = la
    rsum = jnp.zeros_like(la)
    s = 1
    while s < chunk:
        qt = (q * jnp.exp(lsum)).astype(BF16)
        kt = (k * jnp.exp(rsum)).astype(BF16)
        mask = ((ri ^ ci) < 2 * s) & ((ri & s) != 0) & ((ci & s) == 0)
        for h, hs in enumerate(heads):
            attn[h] = attn[h] + jnp.where(mask, _dot_nt(qt[:, hs], kt[:, hs]), 0.0)
        btot = lsum + rsum
        odd = (row & s) != 0
        lsum = lsum + jnp.where(odd, pltpu.roll(btot, s, 0), 0.0)
        rsum = rsum + jnp.where(odd, 0.0, pltpu.roll(btot, chunk - s, 0))
        s *= 2

    qt = (q * jnp.exp(lsum)).astype(BF16)
    kd = (k * jnp.exp(rsum)).astype(BF16)
    total = lsum[0:1, :] + rsum[0:1, :]
    for h, hs in enumerate(heads):
        vs = slice(h * GLA_DV, (h + 1) * GLA_DV)
        v = v_ref[0, :, vs]
        st = st_ref[0, h]
        o_ref[0, :, vs] = _dot(attn[h].astype(BF16), v) + _dot(qt[:, hs], st.astype(BF16))
        tot_col = jnp.broadcast_to(total[:, hs], (GLA_DK, GLA_DK)).T[:, 0:1]
        st_ref[0, h] = jnp.exp(tot_col) * st + _dot_tn(kd[:, hs], v)


def _pair_masks():
    lane = lax.broadcasted_iota(jnp.int32, (1, LANES), 1)
    first = lane < HEAD_DIM
    return first, jnp.logical_not(first)


def _attn_prompt_kernel(q0_ref, q1_ref, q2_ref, k_ref, v_ref, o0_ref, o1_ref, o2_ref,
                        qd_ref, kd_ref, vd_ref, od_ref, lse_ref, *, n):
    q_refs = (q0_ref, q1_ref, q2_ref)
    o_refs = (o0_ref, o1_ref, o2_ref)
    head_masks = _pair_masks()
    sub = n // ATTN_SPLIT
    split_groups = [g for g, (_, dil) in enumerate(DIL_GROUPS) if dil % ATTN_SPLIT == 0]

    for cls in range(ATTN_SPLIT):
        src = pl.ds(cls, sub, stride=ATTN_SPLIT)
        dst = pl.ds(cls * sub, sub)
        kd_ref[dst, :] = k_ref[0, src, :]
        vd_ref[dst, :] = v_ref[0, src, :]
        for slot, g in enumerate(split_groups):
            qd_ref[slot, dst, :] = q_refs[g][0, src, :]

    def attend(qt, kt, vt, blk, u0, qblk, nkeys, reach):
        row = lax.broadcasted_iota(jnp.int32, (2 * qblk, 1), 0)
        uq = blk * qblk + (row & (qblk - 1))
        uk = u0 + lax.broadcasted_iota(jnp.int32, (1, nkeys), 1)
        delta = uq - uk
        valid = (delta >= 0) & (delta <= reach)
        qm = jnp.concatenate([jnp.where(hm, qt, 0.0) for hm in head_masks], axis=0).astype(BF16)
        s = jnp.where(valid, _dot_nt(qm, kt), -jnp.inf)
        m = jnp.max(s, axis=-1, keepdims=True)
        p = jnp.exp(s - m)
        den = jnp.sum(p, axis=-1, keepdims=True)
        out = _dot(p.astype(BF16), vt) / den
        lse = m + jnp.log(den)
        return (jnp.where(head_masks[0], out[:qblk], out[qblk:]),
                jnp.where(head_masks[0], lse[:qblk], lse[qblk:]))

    for g, (win, dil) in enumerate(DIL_GROUPS):
        length = n // dil
        qblk = min(ATTN_QBLOCK, length)
        nblk = length // qblk
        nkeys = min(2 * qblk, length)
        reach = win // dil

        if g in split_groups:
            slot = split_groups.index(g)
            step = dil // ATTN_SPLIT

            def tile(idx, carry, g=g, slot=slot, step=step, dil=dil, qblk=qblk, nkeys=nkeys, reach=reach):
                cls = idx % ATTN_SPLIT
                off = (idx // ATTN_SPLIT) % step
                blk = idx // dil
                u0 = jnp.maximum(blk - 1, 0) * qblk
                base = cls * sub + off
                q_rows = pl.ds(base + step * blk * qblk, qblk, stride=step)
                k_rows = pl.ds(base + step * u0, nkeys, stride=step)
                o, lse = attend(qd_ref[slot, q_rows, :], kd_ref[k_rows, :].astype(BF16),
                                vd_ref[k_rows, :].astype(BF16), blk, u0, qblk, nkeys, reach)
                od_ref[slot, q_rows, :] = o
                lse_ref[g, q_rows, :] = lse
                return carry
        else:
            assert dil == 1

            def tile(idx, carry, g=g, qblk=qblk, nkeys=nkeys, reach=reach):
                u0 = jnp.maximum(idx - 1, 0) * qblk
                q_rows = pl.ds(pl.multiple_of(idx * qblk, qblk), qblk)
                k_rows = pl.ds(pl.multiple_of(u0, qblk), nkeys)
                o, lse = attend(q_refs[g][0, q_rows, :], k_ref[0, k_rows, :].astype(BF16),
                                v_ref[0, k_rows, :].astype(BF16), idx, u0, qblk, nkeys, reach)
                o_refs[g][0, q_rows, :] = o
                lse_ref[g, q_rows, :] = lse
                return carry

        lax.fori_loop(0, dil * nblk, tile, 0, unroll=ATTN_UNROLL)

    rows_per_step = 256
    steps_per_class = sub // rows_per_step

    def reweight(t, carry):
        cls = t // steps_per_class
        start = (t % steps_per_class) * rows_per_step
        nat = pl.ds(cls + ATTN_SPLIT * start, rows_per_step, stride=ATTN_SPLIT)
        grp = pl.ds(pl.multiple_of(cls * sub + start, rows_per_step), rows_per_step)
        lse = [lse_ref[g, grp if g in split_groups else nat, :] for g in range(N_GROUPS)]
        top = jnp.maximum(jnp.maximum(lse[0], lse[1]), lse[2])
        e = [jnp.exp(l - top) for l in lse]
        tot = e[0] + e[1] + e[2]
        for g in range(N_GROUPS):
            if g in split_groups:
                o = od_ref[split_groups.index(g), grp, :]
            else:
                o = o_refs[g][0, nat, :]
            o_refs[g][0, nat, :] = o * (e[g] / tot)
        return carry

    lax.fori_loop(0, ATTN_SPLIT * steps_per_class, reweight, 0)


def _attn_sample_kernel(q0_ref, q1_ref, q2_ref, kn_ref, vn_ref, kc_ref, vc_ref, o0_ref, o1_ref, o2_ref,
                        *, n, cache_len):
    q_refs = (q0_ref, q1_ref, q2_ref)
    o_refs = (o0_ref, o1_ref, o2_ref)
    head_masks = _pair_masks()
    tq = lax.broadcasted_iota(jnp.int32, (n, 1), 0)
    d_cache = (cache_len + tq) - lax.broadcasted_iota(jnp.int32, (1, cache_len), 1)
    d_new = tq - lax.broadcasted_iota(jnp.int32, (1, n), 1)

    for pair in range(KV_DIM // LANES):
        ls = slice(pair * LANES, (pair + 1) * LANES)
        kc = kc_ref[0, :, ls].astype(BF16)
        vc = vc_ref[0, :, ls].astype(BF16)
        kn = kn_ref[0, :, ls].astype(BF16)
        vn = vn_ref[0, :, ls].astype(BF16)
        weighted = []
        for hm in head_masks:
            outs, lses = [], []
            for g, (win, dil) in enumerate(DIL_GROUPS):
                qm = jnp.where(hm, q_refs[g][0, :, ls], 0.0).astype(BF16)
                valid_c = ((d_cache & (dil - 1)) == 0) & (d_cache <= win)
                valid_n = (d_new >= 0) & ((d_new & (dil - 1)) == 0) & (d_new <= win)
                sc = jnp.where(valid_c, _dot_nt(qm, kc), -jnp.inf)
                sn = jnp.where(valid_n, _dot_nt(qm, kn), -jnp.inf)
                m = jnp.maximum(jnp.max(sc, axis=-1, keepdims=True), jnp.max(sn, axis=-1, keepdims=True))
                pc = jnp.exp(sc - m)
                pn = jnp.exp(sn - m)
                den = jnp.sum(pc, axis=-1, keepdims=True) + jnp.sum(pn, axis=-1, keepdims=True)
                outs.append((_dot(pc.astype(BF16), vc) + _dot(pn.astype(BF16), vn)) / den)
                lses.append(m + jnp.log(den))
            top = jnp.maximum(jnp.maximum(lses[0], lses[1]), lses[2])
            e = [jnp.exp(l - top) for l in lses]
            tot = e[0] + e[1] + e[2]
            weighted.append([outs[g] * (e[g] / tot) for g in range(N_GROUPS)])
        for g in range(N_GROUPS):
            o_refs[g][0, :, ls] = jnp.where(head_masks[0], weighted[0][g], weighted[1][g])


def _params(semantics):
    return pltpu.CompilerParams(dimension_semantics=semantics, vmem_limit_bytes=VMEM_LIMIT)


def _row_spec(tm, width):
    return pl.BlockSpec((tm, width), lambda i: (i, 0))


def _const_spec(shape):
    return pl.BlockSpec(shape, lambda i: (0,) * len(shape), pipeline_mode=pl.Buffered(1))


def _row_tile(rows):
    return ROW_TILE if rows % ROW_TILE == 0 else rows


def _ffn_call(x, gain, w_in, w_out):
    rows = x.shape[0]
    tm = _row_tile(rows)
    return pl.pallas_call(
        _ffn_kernel,
        grid=(rows // tm,),
        in_specs=[_row_spec(tm, D_MODEL), _const_spec((1, D_MODEL)),
                  _const_spec(w_in.shape), _const_spec(w_out.shape)],
        out_specs=_row_spec(tm, D_MODEL),
        out_shape=jax.ShapeDtypeStruct((rows, D_MODEL), F32),
        compiler_params=_params(("parallel",)),
        name="ffn",
    )(x, gain, w_in, w_out)


def _gla_proj_call(x, gain, wq, wk, wv, wr, wg, wg2, bg):
    rows = x.shape[0]
    tm = _row_tile(rows)
    consts = (gain, wq, wk, wv, wr, wg, wg2, bg)
    return pl.pallas_call(
        _gla_proj_kernel,
        grid=(rows // tm,),
        in_specs=[_row_spec(tm, D_MODEL)] + [_const_spec(a.shape) for a in consts],
        out_specs=[_row_spec(tm, GLA_QK), _row_spec(tm, GLA_QK), _row_spec(tm, GLA_V),
                   _row_spec(tm, GLA_V), _row_spec(tm, GLA_QK)],
        out_shape=[jax.ShapeDtypeStruct((rows, GLA_QK), F32), jax.ShapeDtypeStruct((rows, GLA_QK), F32),
                   jax.ShapeDtypeStruct((rows, GLA_V), BF16), jax.ShapeDtypeStruct((rows, GLA_V), F32),
                   jax.ShapeDtypeStruct((rows, GLA_QK), F32)],
        compiler_params=_params(("parallel",)),
        name="gla_proj",
    )(x, *consts)


def _gla_rec_call(q, k, v, la, s0):
    batch, n, _ = q.shape
    chunk = min(GLA_CHUNK, n)
    assert n % chunk == 0 and chunk & (chunk - 1) == 0
    has_state = s0 is not None
    seq_spec = lambda width: pl.BlockSpec((1, chunk, width), lambda b, c: (b, c, 0))
    state_spec = pl.BlockSpec((1, GLA_HEADS, GLA_DK, GLA_DV), lambda b, c: (b, 0, 0, 0))
    in_specs = [seq_spec(GLA_QK), seq_spec(GLA_QK), seq_spec(GLA_V), seq_spec(GLA_QK)]
    args = [q, k, v, la]
    if has_state:
        in_specs.append(state_spec)
        args.append(s0)
    return pl.pallas_call(
        functools.partial(_gla_rec_kernel, chunk=chunk, has_state=has_state),
        grid=(batch, n // chunk),
        in_specs=in_specs,
        out_specs=[seq_spec(GLA_V), state_spec],
        out_shape=[jax.ShapeDtypeStruct((batch, n, GLA_V), F32),
                   jax.ShapeDtypeStruct((batch, GLA_HEADS, GLA_DK, GLA_DV), F32)],
        compiler_params=_params(("parallel", "arbitrary")),
        name="gla_rec",
    )(*args)


def _gla_out_call(o, r, x, gn, w_out):
    rows = x.shape[0]
    tm = _row_tile(rows)
    return pl.pallas_call(
        _gla_out_kernel,
        grid=(rows // tm,),
        in_specs=[_row_spec(tm, GLA_V), _row_spec(tm, GLA_V), _row_spec(tm, D_MODEL),
                  _const_spec(gn.shape), _const_spec(w_out.shape)],
        out_specs=_row_spec(tm, D_MODEL),
        out_shape=jax.ShapeDtypeStruct((rows, D_MODEL), F32),
        compiler_params=_params(("parallel",)),
        name="gla_out",
    )(o, r, x, gn, w_out)


def _table_spec(tm, n_rows_table):
    period = n_rows_table // tm
    return pl.BlockSpec((tm, KV_DIM), lambda i: (i % period, 0))


def _kv_call(x, gain, w, kn, cos, sin, avg):
    rows = x.shape[0]
    tm = min(_row_tile(rows), cos.shape[0])
    return pl.pallas_call(
        _kv_kernel,
        grid=(rows // tm,),
        in_specs=[_row_spec(tm, D_MODEL), _const_spec(gain.shape), _const_spec(w.shape), _const_spec(kn.shape),
                  _table_spec(tm, cos.shape[0]), _table_spec(tm, cos.shape[0]), _const_spec(avg.shape)],
        out_specs=[_row_spec(tm, KV_DIM), _row_spec(tm, KV_DIM)],
        out_shape=[jax.ShapeDtypeStruct((rows, KV_DIM), F32)] * 2,
        compiler_params=_params(("parallel",)),
        name="shared_kv",
    )(x, gain, w, kn, cos, sin, avg)


def _q_call(x, gain, w, qn, cos, sin, avg):
    rows = x.shape[0]
    tm = min(_row_tile(rows), cos.shape[0])
    return pl.pallas_call(
        _q_kernel,
        grid=(rows // tm,),
        in_specs=[_row_spec(tm, D_MODEL), _const_spec(gain.shape), _const_spec(w.shape), _const_spec(qn.shape),
                  _table_spec(tm, cos.shape[0]), _table_spec(tm, cos.shape[0]), _const_spec(avg.shape)],
        out_specs=[_row_spec(tm, KV_DIM)] * N_GROUPS,
        out_shape=[jax.ShapeDtypeStruct((rows, KV_DIM), F32)] * N_GROUPS,
        compiler_params=_params(("parallel",)),
        name="attn_q",
    )(x, gain, w, qn, cos, sin, avg)


def _attn_prompt_call(qs, k, v):
    batch, n, _ = k.shape
    assert n % (ATTN_QBLOCK * max(d for _, d in DIL_GROUPS)) == 0
    n_split = sum(1 for _, d in DIL_GROUPS if d % ATTN_SPLIT == 0)
    spec = pl.BlockSpec((1, n, LANES), lambda b, p: (b, 0, p))
    return pl.pallas_call(
        functools.partial(_attn_prompt_kernel, n=n),
        grid=(batch, KV_DIM // LANES),
        in_specs=[spec] * (N_GROUPS + 2),
        out_specs=[spec] * N_GROUPS,
        out_shape=[jax.ShapeDtypeStruct((batch, n, KV_DIM), F32)] * N_GROUPS,
        scratch_shapes=[pltpu.VMEM((n_split, n, LANES), F32), pltpu.VMEM((n, LANES), F32),
                        pltpu.VMEM((n, LANES), F32), pltpu.VMEM((n_split, n, LANES), F32),
                        pltpu.VMEM((N_GROUPS, n, LANES), F32)],
        compiler_params=_params(("parallel", "parallel")),
        name="attn_prompt",
    )(*qs, k, v)


def _attn_sample_call(qs, k_new, v_new, k_cache, v_cache):
    batch, n, _ = k_new.shape
    cache_len = k_cache.shape[1]
    new_spec = pl.BlockSpec((1, n, KV_DIM), lambda b: (b, 0, 0))
    cache_spec = pl.BlockSpec((1, cache_len, KV_DIM), lambda b: (b, 0, 0))
    return pl.pallas_call(
        functools.partial(_attn_sample_kernel, n=n, cache_len=cache_len),
        grid=(batch,),
        in_specs=[new_spec] * (N_GROUPS + 2) + [cache_spec] * 2,
        out_specs=[new_spec] * N_GROUPS,
        out_shape=[jax.ShapeDtypeStruct((batch, n, KV_DIM), F32)] * N_GROUPS,
        compiler_params=_params(("parallel",)),
        name="attn_sample",
    )(*qs, k_new, v_new, k_cache, v_cache)


def _attn_out_call(os_, x, w):
    rows = x.shape[0]
    tm = _row_tile(rows)
    return pl.pallas_call(
        _attn_out_kernel,
        grid=(rows // tm,),
        in_specs=[_row_spec(tm, KV_DIM)] * N_GROUPS + [_row_spec(tm, D_MODEL), _const_spec(w.shape)],
        out_specs=_row_spec(tm, D_MODEL),
        out_shape=jax.ShapeDtypeStruct((rows, D_MODEL), F32),
        compiler_params=_params(("parallel",)),
        name="attn_out",
    )(*os_, x, w)


def _rope_tables(pos0, n, reps):
    half = HEAD_DIM // 2
    inv = ROPE_THETA ** (-np.arange(half, dtype=np.float64) / half)
    ang = (pos0 + np.arange(n, dtype=np.float64))[:, None] * inv[None, :]
    cos = np.concatenate([np.cos(ang), np.cos(ang)], axis=-1)
    sin = np.concatenate([-np.sin(ang), np.sin(ang)], axis=-1)
    cos = np.tile(cos, (reps, N_KV_HEADS)).astype(np.float32)
    sin = np.tile(sin, (reps, N_KV_HEADS)).astype(np.float32)
    return jnp.asarray(cos), jnp.asarray(sin)


def _ffn_in_layout(w_in):
    gate, up = w_in[:, :FFN_DIM], w_in[:, FFN_DIM:]
    return jnp.concatenate([gate[:, :FFN_MAIN], up[:, :FFN_MAIN], gate[:, FFN_MAIN:], up[:, FFN_MAIN:]],
                           axis=1).astype(BF16)


def _head_mean_matrix():
    idx = np.arange(KV_DIM) // HEAD_DIM
    return jnp.asarray((idx[:, None] == idx[None, :]).astype(np.float32) / HEAD_DIM, dtype=BF16)


def _run_group(x, pos0, s0, caches, w):
    batch, n, _ = x.shape
    rows = batch * n
    x = x.reshape(rows, D_MODEL)
    reps = 1 if n % ROW_TILE == 0 else batch
    cos, sin = _rope_tables(pos0, n, reps)
    avg = _head_mean_matrix()

    x = _ffn_call(x, w["gain"][0][0], w["ffn_in"][0][0], w["ffn_out"][0][0])
    q, k, v, r, la = _gla_proj_call(x, w["gain"][0][1], *w["gla_in"], w["gla_g2"], w["gla_bg"])
    seq = lambda a: a.reshape(batch, n, a.shape[-1])
    o, s_fin = _gla_rec_call(seq(q), seq(k), seq(v), seq(la), s0)
    x = _gla_out_call(o.reshape(rows, GLA_V), r, x, w["gla_norm"], w["gla_out"])
    x = _ffn_call(x, w["gain"][0][2], w["ffn_in"][0][1], w["ffn_out"][0][1])

    k_new, v_new = _kv_call(x, w["kv_gain"], w["kv_w"], w["k_norm"], cos, sin, avg)

    x = _ffn_call(x, w["gain"][1][0], w["ffn_in"][1][0], w["ffn_out"][1][0])
    qs = _q_call(x, w["gain"][1][1], w["attn_q"], w["q_norm"], cos, sin, avg)
    qs = [seq(a) for a in qs]
    if caches is None:
        os_ = _attn_prompt_call(qs, seq(k_new), seq(v_new))
    else:
        os_ = _attn_sample_call(qs, seq(k_new), seq(v_new), *caches)
    x = _attn_out_call([a.reshape(rows, KV_DIM) for a in os_], x, w["attn_out"])
    x = _ffn_call(x, w["gain"][1][2], w["ffn_in"][1][1], w["ffn_out"][1][1])

    kv_shape = (batch, n, N_KV_HEADS, HEAD_DIM)
    return x.reshape(batch, n, D_MODEL), s_fin[None], k_new.reshape(kv_shape), v_new.reshape(kv_shape)


def kernel(x_prompt, x_sample, state_gla, cache_k_win, cache_v_win, norm_gains, ffn_w_in, ffn_w_out,
           gla_w_in, gla_w_gate2, gla_b_gate, gla_out_norm, gla_w_out, kv_norm, kv_w, k_norm,
           attn_w_q, q_norm, attn_w_out):
    assert norm_gains.shape[0] == 2 and gla_w_in.shape[0] == 1 and attn_w_q.shape[0] == 1
    row = lambda a: a.reshape(1, -1)
    gw = gla_w_in[0]
    cuts = (0, GLA_QK, 2 * GLA_QK, 2 * GLA_QK + GLA_V, 2 * GLA_QK + GLA_V + GLA_RANK, gw.shape[1])
    wq, wk, wv, wg, wr = (gw[:, a:b].astype(BF16) for a, b in zip(cuts[:-1], cuts[1:]))
    pad = LANES - GLA_RANK
    w = {
        "gain": [[row(norm_gains[l, i]) for i in range(3)] for l in range(2)],
        "ffn_in": [[_ffn_in_layout(ffn_w_in[l, i]) for i in range(2)] for l in range(2)],
        "ffn_out": [[ffn_w_out[l, i].astype(BF16) for i in range(2)] for l in range(2)],
        "gla_in": (wq, wk, wv, wr, jnp.pad(wg, ((0, 0), (0, pad)))),
        "gla_g2": jnp.pad(gla_w_gate2[0].astype(BF16), ((0, pad), (0, 0))),
        "gla_bg": row(gla_b_gate[0]),
        "gla_norm": row(gla_out_norm[0]),
        "gla_out": gla_w_out[0].astype(BF16),
        "kv_gain": row(kv_norm),
        "kv_w": kv_w.astype(BF16),
        "k_norm": row(jnp.tile(k_norm, N_KV_HEADS)),
        "attn_q": attn_w_q[0].astype(BF16),
        "q_norm": row(jnp.tile(q_norm[0], N_KV_HEADS)),
        "attn_out": attn_w_out[0].astype(BF16),
    }
    cache_len = cache_k_win.shape[1]
    caches = (cache_k_win.reshape(-1, cache_len, KV_DIM), cache_v_win.reshape(-1, cache_len, KV_DIM))

    y_p, s_p, k_p, v_p = _run_group(x_prompt, 0, None, None, w)
    y_s, s_s, k_s, v_s = _run_group(x_sample, PAST_LEN, state_gla[0], caches, w)
    keep = min(MAX_WINDOW, x_prompt.shape[1])
    return (y_p, y_s, s_p, s_s, k_p[:, -keep:], v_p[:, -keep:], k_s, v_s)
```

```python
import functools

import numpy as np
import jax
import jax.numpy as jnp
from jax import lax
from jax.experimental import pallas as pl
from jax.experimental.pallas import tpu as pltpu

F32 = jnp.float32
BF16 = jnp.bfloat16

D_MODEL = 1024
FFN_DIM = 2688
EPS = 1e-6
PAST_LEN = 8192
GLA_HEADS = 4
GLA_DK = 128
GLA_DV = 256
GLA_RANK = 16
GLA_TAU = 16.0
GLA_QK = GLA_HEADS * GLA_DK
GLA_V = GLA_HEADS * GLA_DV
HEAD_DIM = 64
N_KV_HEADS = 4
KV_DIM = N_KV_HEADS * HEAD_DIM
DIL_GROUPS = ((128, 1), (512, 4), (2048, 16))
N_GROUPS = len(DIL_GROUPS)
MAX_WINDOW = max(w for w, _ in DIL_GROUPS)
ROPE_THETA = 10000.0

LANES = 128
GLA_CHUNK = 128
ATTN_QBLOCK = 128
ROW_TILE = 512
MXU_TILE = 256
FFN_MAIN = (FFN_DIM // MXU_TILE) * MXU_TILE
FFN_REM = FFN_DIM - FFN_MAIN
FFN_CHUNK = 1280
ATTN_SPLIT = 4
ATTN_UNROLL = 16
VMEM_LIMIT = 52 * 1024 * 1024


def _dot(a, b):
    return jnp.dot(a, b, preferred_element_type=F32)


def _dot_nt(a, b):
    return lax.dot_general(a, b, (((1,), (1,)), ((), ())), preferred_element_type=F32)


def _dot_tn(a, b):
    return lax.dot_general(a, b, (((0,), (0,)), ((), ())), preferred_element_type=F32)


def _rms(x, g):
    ms = jnp.mean(x * x, axis=-1, keepdims=True)
    return x * lax.rsqrt(ms + EPS) * g


def _silu(x):
    return x * jax.nn.sigmoid(x)


def _group_mean_sq(x, avg):
    sq = x * x
    hi = sq.astype(BF16)
    lo = (sq - hi.astype(F32)).astype(BF16)
    return _dot(hi, avg) + _dot(lo, avg)


def _rope(x, cos, sin_signed):
    w = x.shape[-1]
    lane = lax.broadcasted_iota(jnp.int32, (1, w), 1)
    first_half = (lane % HEAD_DIM) < (HEAD_DIM // 2)
    rot = jnp.where(first_half, pltpu.roll(x, w - HEAD_DIM // 2, 1), pltpu.roll(x, HEAD_DIM // 2, 1))
    return x * cos + rot * sin_signed


def _ffn_half(x, g_ref, win_ref, wout_ref):
    h = _rms(x, g_ref[...]).astype(BF16)
    acc = None
    for c0 in range(0, FFN_MAIN, FFN_CHUNK):
        c1 = c0 + FFN_CHUNK
        gate = _dot(h, win_ref[:, c0:c1])
        up = _dot(h, win_ref[:, FFN_MAIN + c0:FFN_MAIN + c1])
        act = (_silu(gate) * up).astype(BF16)
        part = _dot(act, wout_ref[c0:c1, :])
        acc = part if acc is None else acc + part
    if FFN_REM:
        gate_up = _dot(h, win_ref[:, 2 * FFN_MAIN:])
        act = (_silu(gate_up[:, :FFN_REM]) * gate_up[:, FFN_REM:]).astype(BF16)
        acc = acc + _dot(act, wout_ref[FFN_MAIN:, :])
    return x + 0.5 * acc


def _ffn_gla_in_kernel(x_ref, g_ffn_ref, win_ref, wout_ref, g_mix_ref, wq_ref, wk_ref, wv_ref, wg_ref,
                       wg2_ref, bg_ref, x1_ref, q_ref, k_ref, v_ref, la_ref):
    x1 = _ffn_half(x_ref[...], g_ffn_ref, win_ref, wout_ref)
    x1_ref[...] = x1
    h = _rms(x1, g_mix_ref[...]).astype(BF16)
    q_ref[...] = _dot(h, wq_ref[...]) * (GLA_DK ** -0.5)
    k_ref[...] = _dot(h, wk_ref[...])
    v_ref[...] = _dot(h, wv_ref[...]).astype(BF16)
    g_lr = _dot(h, wg_ref[...])
    z = _dot(g_lr.astype(BF16), wg2_ref[...]) + bg_ref[...]
    log_sig = jnp.minimum(z, 0.0) - jnp.log1p(jnp.exp(-jnp.abs(z)))
    la_ref[...] = log_sig * (1.0 / GLA_TAU)


def _gla_out_ffn_kernel(o_ref, x1_ref, g_mix_ref, wr_ref, gn_ref, wo_ref, g_ffn_ref, win_ref, wout_ref, x3_ref):
    x1 = x1_ref[...]
    r = _dot(_rms(x1, g_mix_ref[...]).astype(BF16), wr_ref[...])
    o = o_ref[...]
    on = jnp.concatenate([_rms(o[:, h * GLA_DV:(h + 1) * GLA_DV], gn_ref[...]) for h in range(GLA_HEADS)],
                         axis=1)
    x2 = x1 + _dot((on * _silu(r)).astype(BF16), wo_ref[...])
    x3_ref[...] = _ffn_half(x2, g_ffn_ref, win_ref, wout_ref)


def _kv_ffn_q_kernel(x3_ref, g_kv_ref, wkv_ref, kn_ref, avg_ref, g_ffn_ref, win_ref, wout_ref, g_mix_ref,
                     wq_ref, qn_ref, cos_ref, sin_ref, x4_ref, k_ref, v_ref, q0_ref, q1_ref, q2_ref):
    x3 = x3_ref[...]
    cos, sin, avg = cos_ref[...], sin_ref[...], avg_ref[...]
    kv = _dot(_rms(x3, g_kv_ref[...]).astype(BF16), wkv_ref[...])
    k = kv[:, :KV_DIM]
    kn = k * lax.rsqrt(_group_mean_sq(k, avg) + EPS) * kn_ref[...]
    k_ref[...] = _rope(kn, cos, sin)
    v_ref[...] = kv[:, KV_DIM:]
    x4 = _ffn_half(x3, g_ffn_ref, win_ref, wout_ref)
    x4_ref[...] = x4
    h = _rms(x4, g_mix_ref[...]).astype(BF16)
    for g, q_ref in enumerate((q0_ref, q1_ref, q2_ref)):
        q = _dot(h, wq_ref[:, g * KV_DIM:(g + 1) * KV_DIM])
        qn = q * lax.rsqrt(_group_mean_sq(q, avg) + EPS) * qn_ref[...]
        q_ref[...] = _rope(qn, cos, sin) * (HEAD_DIM ** -0.5)


def _attn_out_ffn_kernel(o0_ref, o1_ref, o2_ref, x4_ref, wo_ref, g_ffn_ref, win_ref, wout_ref, y_ref):
    x5 = x4_ref[...]
    for g, o_ref in enumerate((o0_ref, o1_ref, o2_ref)):
        x5 = x5 + _dot(o_ref[...].astype(BF16), wo_ref[g * KV_DIM:(g + 1) * KV_DIM, :])
    y_ref[...] = _ffn_half(x5, g_ffn_ref, win_ref, wout_ref)


def _gla_rec_kernel(*refs, chunk, has_state):
    if has_state:
        q_ref, k_ref, v_ref, la_ref, s0_ref, o_ref, st_ref = refs
    else:
        q_ref, k_ref, v_ref, la_ref, o_ref, st_ref = refs
    c = pl.program_id(1)

    @pl.when(c == 0)
    def _():
        if has_state:
            st_ref[...] = s0_ref[...]
        else:
            st_ref[...] = jnp.zeros_like(st_ref)

    q = q_ref[0]
    k = k_ref[0]
    la = la_ref[0]
    heads = [slice(h * GLA_DK, (h + 1) * GLA_DK) for h in range(GLA_HEADS)]
    row = lax.broadcasted_iota(jnp.int32, (chunk, 1), 0)
    ri = lax.broadcasted_iota(jnp.int32, (chunk, chunk), 0)
    ci = lax.broadcasted_iota(jnp.int32, (chunk, chunk), 1)

    qb = q.astype(BF16)
    kb = k.astype(BF16)
    attn = [jnp.where(ri == ci, _dot_nt(qb[:, hs], kb[:, hs]), 0.0) for hs in heads]

    lsum = la
    rsum = jnp.zeros_like(la)
    s = 1
    while s < chunk:
        qt = (q * jnp.exp(lsum)).astype(BF16)
        kt = (k * jnp.exp(rsum)).astype(BF16)
        mask = ((ri ^ ci) < 2 * s) & ((ri & s) != 0) & ((ci & s) == 0)
        for h, hs in enumerate(heads):
            attn[h] = attn[h] + jnp.where(mask, _dot_nt(qt[:, hs], kt[:, hs]), 0.0)
        btot = lsum + rsum
        odd = (row & s) != 0
        lsum = lsum + jnp.where(odd, pltpu.roll(btot, s, 0), 0.0)
        rsum = rsum + jnp.where(odd, 0.0, pltpu.roll(btot, chunk - s, 0))
        s *= 2

    qt = (q * jnp.exp(lsum)).astype(BF16)
    kd = (k * jnp.exp(rsum)).astype(BF16)
    total = lsum[0:1, :] + rsum[0:1, :]
    for h, hs in enumerate(heads):
        vs = slice(h * GLA_DV, (h + 1) * GLA_DV)
        v = v_ref[0, :, vs]
        st = st_ref[0, h]
        o_ref[0, :, vs] = _dot(attn[h].astype(BF16), v) + _dot(qt[:, hs], st.astype(BF16))
        tot_col = jnp.broadcast_to(total[:, hs], (GLA_DK, GLA_DK)).T[:, 0:1]
        st_ref[0, h] = jnp.exp(tot_col) * st + _dot_tn(kd[:, hs], v)


def _pair_masks():
    lane = lax.broadcasted_iota(jnp.int32, (1, LANES), 1)
    first = lane < HEAD_DIM
    return first, jnp.logical_not(first)


def _attn_prompt_kernel(q0_ref, q1_ref, q2_ref, k_ref, v_ref, o0_ref, o1_ref, o2_ref,
                        qd_ref, kd_ref, vd_ref, od_ref, lse_ref, *, n):
    q_refs = (q0_ref, q1_ref, q2_ref)
    o_refs = (o0_ref, o1_ref, o2_ref)
    head_masks = _pair_masks()
    sub = n // ATTN_SPLIT
    split_groups = [g for g, (_, dil) in enumerate(DIL_GROUPS) if dil % ATTN_SPLIT == 0]

    for cls in range(ATTN_SPLIT):
        src = pl.ds(cls, sub, stride=ATTN_SPLIT)
        dst = pl.ds(cls * sub, sub)
        kd_ref[dst, :] = k_ref[0, src, :]
        vd_ref[dst, :] = v_ref[0, src, :]
        for slot, g in enumerate(split_groups):
            qd_ref[slot, dst, :] = q_refs[g][0, src, :]

    def attend(qt, kt, vt, blk, u0, qblk, nkeys, reach):
        row = lax.broadcasted_iota(jnp.int32, (2 * qblk, 1), 0)
        uq = blk * qblk + (row & (qblk - 1))
        uk = u0 + lax.broadcasted_iota(jnp.int32, (1, nkeys), 1)
        delta = uq - uk
        valid = (delta >= 0) & (delta <= reach)
        qm = jnp.concatenate([jnp.where(hm, qt, 0.0) for hm in head_masks], axis=0).astype(BF16)
        s = jnp.where(valid, _dot_nt(qm, kt), -jnp.inf)
        m = jnp.max(s, axis=-1, keepdims=True)
        p = jnp.exp(s - m)
        den = jnp.sum(p, axis=-1, keepdims=True)
        out = _dot(p.astype(BF16), vt) / den
        lse = m + jnp.log(den)
        return (jnp.where(head_masks[0], out[:qblk], out[qblk:]),
                jnp.where(head_masks[0], lse[:qblk], lse[qblk:]))

    for g, (win, dil) in enumerate(DIL_GROUPS):
        length = n // dil
        qblk = min(ATTN_QBLOCK, length)
        nblk = length // qblk
        nkeys = min(2 * qblk, length)
        reach = win // dil

        if g in split_groups:
            slot = split_groups.index(g)
            step = dil // ATTN_SPLIT

            def tile(idx, carry, g=g, slot=slot, step=step, dil=dil, qblk=qblk, nkeys=nkeys, reach=reach):
                cls = idx % ATTN_SPLIT
                off = (idx // ATTN_SPLIT) % step
                blk = idx // dil
                u0 = jnp.maximum(blk - 1, 0) * qblk
                base = cls * sub + off
                q_rows = pl.ds(base + step * blk * qblk, qblk, stride=step)
                k_rows = pl.ds(base + step * u0, nkeys, stride=step)
                o, lse = attend(qd_ref[slot, q_rows, :], kd_ref[k_rows, :].astype(BF16),
                                vd_ref[k_rows, :].astype(BF16), blk, u0, qblk, nkeys, reach)
                od_ref[slot, q_rows, :] = o
                lse_ref[g, q_rows, :] = lse
                return carry
        else:
            assert dil == 1

            def tile(idx, carry, g=g, qblk=qblk, nkeys=nkeys, reach=reach):
                u0 = jnp.maximum(idx - 1, 0) * qblk
                q_rows = pl.ds(pl.multiple_of(idx * qblk, qblk), qblk)
                k_rows = pl.ds(pl.multiple_of(u0, qblk), nkeys)
                o, lse = attend(q_refs[g][0, q_rows, :], k_ref[0, k_rows, :].astype(BF16),
                                v_ref[0, k_rows, :].astype(BF16), idx, u0, qblk, nkeys, reach)
                o_refs[g][0, q_rows, :] = o
                lse_ref[g, q_rows, :] = lse
                return carry

        lax.fori_loop(0, dil * nblk, tile, 0, unroll=ATTN_UNROLL)

    rows_per_step = 256
    steps_per_class = sub // rows_per_step

    def reweight(t, carry):
        cls = t // steps_per_class
        start = (t % steps_per_class) * rows_per_step
        nat = pl.ds(cls + ATTN_SPLIT * start, rows_per_step, stride=ATTN_SPLIT)
        grp = pl.ds(pl.multiple_of(cls * sub + start, rows_per_step), rows_per_step)
        lse = [lse_ref[g, grp if g in split_groups else nat, :] for g in range(N_GROUPS)]
        top = jnp.maximum(jnp.maximum(lse[0], lse[1]), lse[2])
        e = [jnp.exp(l - top) for l in lse]
        tot = e[0] + e[1] + e[2]
        for g in range(N_GROUPS):
            if g in split_groups:
                o = od_ref[split_groups.index(g), grp, :]
            else:
                o = o_refs[g][0, nat, :]
            o_refs[g][0, nat, :] = o * (e[g] / tot)
        return carry

    lax.fori_loop(0, ATTN_SPLIT * steps_per_class, reweight, 0)


def _attn_sample_kernel(q0_ref, q1_ref, q2_ref, kn_ref, vn_ref, kc_ref, vc_ref, o0_ref, o1_ref, o2_ref,
                        *, n, cache_len):
    q_refs = (q0_ref, q1_ref, q2_ref)
    o_refs = (o0_ref, o1_ref, o2_ref)
    head_masks = _pair_masks()
    tq = lax.broadcasted_iota(jnp.int32, (n, 1), 0)
    d_cache = (cache_len + tq) - lax.broadcasted_iota(jnp.int32, (1, cache_len), 1)
    d_new = tq - lax.broadcasted_iota(jnp.int32, (1, n), 1)

    for pair in range(KV_DIM // LANES):
        ls = slice(pair * LANES, (pair + 1) * LANES)
        kc = kc_ref[0, :, ls].astype(BF16)
        vc = vc_ref[0, :, ls].astype(BF16)
        kn = kn_ref[0, :, ls].astype(BF16)
        vn = vn_ref[0, :, ls].astype(BF16)
        weighted = []
        for hm in head_masks:
            outs, lses = [], []
            for g, (win, dil) in enumerate(DIL_GROUPS):
                qm = jnp.where(hm, q_refs[g][0, :, ls], 0.0).astype(BF16)
                valid_c = ((d_cache & (dil - 1)) == 0) & (d_cache <= win)
                valid_n = (d_new >= 0) & ((d_new & (dil - 1)) == 0) & (d_new <= win)
                sc = jnp.where(valid_c, _dot_nt(qm, kc), -jnp.inf)
                sn = jnp.where(valid_n, _dot_nt(qm, kn), -jnp.inf)
                m = jnp.maximum(jnp.max(sc, axis=-1, keepdims=True), jnp.max(sn, axis=-1, keepdims=True))
                pc = jnp.exp(sc - m)
                pn = jnp.exp(sn - m)
                den = jnp.sum(pc, axis=-1, keepdims=True) + jnp.sum(pn, axis=-1, keepdims=True)
                outs.append((_dot(pc.astype(BF16), vc) + _dot(pn.astype(BF16), vn)) / den)
                lses.append(m + jnp.log(den))
            top = jnp.maximum(jnp.maximum(lses[0], lses[1]), lses[2])
            e = [jnp.exp(l - top) for l in lses]
            tot = e[0] + e[1] + e[2]
            weighted.append([outs[g] * (e[g] / tot) for g in range(N_GROUPS)])
        for g in range(N_GROUPS):
            o_refs[g][0, :, ls] = jnp.where(head_masks[0], weighted[0][g], weighted[1][g])


def _params(semantics):
    return pltpu.CompilerParams(dimension_semantics=semantics, vmem_limit_bytes=VMEM_LIMIT)


def _row_spec(tm, width):
    return pl.BlockSpec((tm, width), lambda i: (i, 0))


def _const_spec(shape):
    return pl.BlockSpec(shape, lambda i: (0,) * len(shape), pipeline_mode=pl.Buffered(1))


def _rowwise_call(kernel_fn, name, rows, row_inputs, consts, tables, outs):
    tm = ROW_TILE if rows % ROW_TILE == 0 else rows
    if tables:
        tm = min(tm, tables[0].shape[0])
    assert rows % tm == 0 and all(t.shape[0] % tm == 0 for t in tables)
    in_specs = [_row_spec(tm, a.shape[1]) for a in row_inputs]
    in_specs += [_const_spec(a.shape) for a in consts]
    for t in tables:
        period = t.shape[0] // tm
        in_specs.append(pl.BlockSpec((tm, t.shape[1]), lambda i, period=period: (i % period, 0)))
    return pl.pallas_call(
        kernel_fn,
        grid=(rows // tm,),
        in_specs=in_specs,
        out_specs=[_row_spec(tm, width) for width, _ in outs],
        out_shape=[jax.ShapeDtypeStruct((rows, width), dtype) for width, dtype in outs],
        compiler_params=_params(("parallel",)),
        name=name,
    )(*row_inputs, *consts, *tables)


def _gla_rec_call(q, k, v, la, s0):
    batch, n, _ = q.shape
    chunk = min(GLA_CHUNK, n)
    assert n % chunk == 0 and chunk & (chunk - 1) == 0
    has_state = s0 is not None
    seq_spec = lambda width: pl.BlockSpec((1, chunk, width), lambda b, c: (b, c, 0))
    state_spec = pl.BlockSpec((1, GLA_HEADS, GLA_DK, GLA_DV), lambda b, c: (b, 0, 0, 0))
    in_specs = [seq_spec(GLA_QK), seq_spec(GLA_QK), seq_spec(GLA_V), seq_spec(GLA_QK)]
    args = [q, k, v, la]
    if has_state:
        in_specs.append(state_spec)
        args.append(s0)
    return pl.pallas_call(
        functools.partial(_gla_rec_kernel, chunk=chunk, has_state=has_state),
        grid=(batch, n // chunk),
        in_specs=in_specs,
        out_specs=[seq_spec(GLA_V), state_spec],
        out_shape=[jax.ShapeDtypeStruct((batch, n, GLA_V), F32),
                   jax.ShapeDtypeStruct((batch, GLA_HEADS, GLA_DK, GLA_DV), F32)],
        compiler_params=_params(("parallel", "arbitrary")),
        name="gla_rec",
    )(*args)


def _attn_prompt_call(qs, k, v):
    batch, n, _ = k.shape
    assert n % (ATTN_QBLOCK * max(d for _, d in DIL_GROUPS)) == 0
    n_split = sum(1 for _, d in DIL_GROUPS if d % ATTN_SPLIT == 0)
    spec = pl.BlockSpec((1, n, LANES), lambda b, p: (b, 0, p))
    return pl.pallas_call(
        functools.partial(_attn_prompt_kernel, n=n),
        grid=(batch, KV_DIM // LANES),
        in_specs=[spec] * (N_GROUPS + 2),
        out_specs=[spec] * N_GROUPS,
        out_shape=[jax.ShapeDtypeStruct((batch, n, KV_DIM), F32)] * N_GROUPS,
        scratch_shapes=[pltpu.VMEM((n_split, n, LANES), F32), pltpu.VMEM((n, LANES), F32),
                        pltpu.VMEM((n, LANES), F32), pltpu.VMEM((n_split, n, LANES), F32),
                        pltpu.VMEM((N_GROUPS, n, LANES), F32)],
        compiler_params=_params(("parallel", "parallel")),
        name="attn_prompt",
    )(*qs, k, v)


def _attn_sample_call(qs, k_new, v_new, k_cache, v_cache):
    batch, n, _ = k_new.shape
    cache_len = k_cache.shape[1]
    new_spec = pl.BlockSpec((1, n, KV_DIM), lambda b: (b, 0, 0))
    cache_spec = pl.BlockSpec((1, cache_len, KV_DIM), lambda b: (b, 0, 0))
    return pl.pallas_call(
        functools.partial(_attn_sample_kernel, n=n, cache_len=cache_len),
        grid=(batch,),
        in_specs=[new_spec] * (N_GROUPS + 2) + [cache_spec] * 2,
        out_specs=[new_spec] * N_GROUPS,
        out_shape=[jax.ShapeDtypeStruct((batch, n, KV_DIM), F32)] * N_GROUPS,
        compiler_params=_params(("parallel",)),
        name="attn_sample",
    )(*qs, k_new, v_new, k_cache, v_cache)


def _rope_tables(pos0, n, reps):
    half = HEAD_DIM // 2
    inv = ROPE_THETA ** (-np.arange(half, dtype=np.float64) / half)
    ang = (pos0 + np.arange(n, dtype=np.float64))[:, None] * inv[None, :]
    cos = np.concatenate([np.cos(ang), np.cos(ang)], axis=-1)
    sin = np.concatenate([-np.sin(ang), np.sin(ang)], axis=-1)
    cos = np.tile(cos, (reps, N_KV_HEADS)).astype(np.float32)
    sin = np.tile(sin, (reps, N_KV_HEADS)).astype(np.float32)
    return jnp.asarray(cos), jnp.asarray(sin)


def _ffn_in_layout(w_in):
    gate, up = w_in[:, :FFN_DIM], w_in[:, FFN_DIM:]
    return jnp.concatenate([gate[:, :FFN_MAIN], up[:, :FFN_MAIN], gate[:, FFN_MAIN:], up[:, FFN_MAIN:]],
                           axis=1).astype(BF16)


def _head_mean_matrix():
    idx = np.arange(KV_DIM) // HEAD_DIM
    return jnp.asarray((idx[:, None] == idx[None, :]).astype(np.float32) / HEAD_DIM, dtype=BF16)


def _run_group(x, pos0, s0, caches, w):
    batch, n, _ = x.shape
    rows = batch * n
    x = x.reshape(rows, D_MODEL)
    reps = 1 if n % ROW_TILE == 0 else batch
    cos, sin = _rope_tables(pos0, n, reps)
    avg = _head_mean_matrix()

    gain, ffn_in, ffn_out = w["gain"], w["ffn_in"], w["ffn_out"]
    seq = lambda a: a.reshape(batch, n, a.shape[-1])
    wide, qk, kv = (D_MODEL, F32), (GLA_QK, F32), (KV_DIM, F32)

    x1, q, k, v, la = _rowwise_call(
        _ffn_gla_in_kernel, "ffn_gla_in", rows, [x],
        [gain[0][0], ffn_in[0][0], ffn_out[0][0], gain[0][1], *w["gla_qkvg"], w["gla_g2"], w["gla_bg"]], [],
        [wide, qk, qk, (GLA_V, BF16), qk])
    o, s_fin = _gla_rec_call(seq(q), seq(k), seq(v), seq(la), s0)
    (x3,) = _rowwise_call(
        _gla_out_ffn_kernel, "gla_out_ffn", rows, [o.reshape(rows, GLA_V), x1],
        [gain[0][1], w["gla_r"], w["gla_norm"], w["gla_out"], gain[0][2], ffn_in[0][1], ffn_out[0][1]], [],
        [wide])

    x4, k_new, v_new, *qs = _rowwise_call(
        _kv_ffn_q_kernel, "kv_ffn_q", rows, [x3],
        [w["kv_gain"], w["kv_w"], w["k_norm"], avg, gain[1][0], ffn_in[1][0], ffn_out[1][0], gain[1][1],
         w["attn_q"], w["q_norm"]], [cos, sin],
        [wide, kv, kv] + [kv] * N_GROUPS)

    qs = [seq(a) for a in qs]
    if caches is None:
        os_ = _attn_prompt_call(qs, seq(k_new), seq(v_new))
    else:
        os_ = _attn_sample_call(qs, seq(k_new), seq(v_new), *caches)
    (x,) = _rowwise_call(
        _attn_out_ffn_kernel, "attn_out_ffn", rows, [a.reshape(rows, KV_DIM) for a in os_] + [x4],
        [w["attn_out"], gain[1][2], ffn_in[1][1], ffn_out[1][1]], [],
        [wide])

    kv_shape = (batch, n, N_KV_HEADS, HEAD_DIM)
    return x.reshape(batch, n, D_MODEL), s_fin[None], k_new.reshape(kv_shape), v_new.reshape(kv_shape)


def kernel(x_prompt, x_sample, state_gla, cache_k_win, cache_v_win, norm_gains, ffn_w_in, ffn_w_out,
           gla_w_in, gla_w_gate2, gla_b_gate, gla_out_norm, gla_w_out, kv_norm, kv_w, k_norm,
           attn_w_q, q_norm, attn_w_out):
    assert norm_gains.shape[0] == 2 and gla_w_in.shape[0] == 1 and attn_w_q.shape[0] == 1
    row = lambda a: a.reshape(1, -1)
    gw = gla_w_in[0]
    cuts = (0, GLA_QK, 2 * GLA_QK, 2 * GLA_QK + GLA_V, 2 * GLA_QK + GLA_V + GLA_RANK, gw.shape[1])
    wq, wk, wv, wg, wr = (gw[:, a:b].astype(BF16) for a, b in zip(cuts[:-1], cuts[1:]))
    pad = LANES - GLA_RANK
    w = {
        "gain": [[row(norm_gains[l, i]) for i in range(3)] for l in range(2)],
        "ffn_in": [[_ffn_in_layout(ffn_w_in[l, i]) for i in range(2)] for l in range(2)],
        "ffn_out": [[ffn_w_out[l, i].astype(BF16) for i in range(2)] for l in range(2)],
        "gla_qkvg": (wq, wk, wv, jnp.pad(wg, ((0, 0), (0, pad)))),
        "gla_r": wr,
        "gla_g2": jnp.pad(gla_w_gate2[0].astype(BF16), ((0, pad), (0, 0))),
        "gla_bg": row(gla_b_gate[0]),
        "gla_norm": row(gla_out_norm[0]),
        "gla_out": gla_w_out[0].astype(BF16),
        "kv_gain": row(kv_norm),
        "kv_w": kv_w.astype(BF16),
        "k_norm": row(jnp.tile(k_norm, N_KV_HEADS)),
        "attn_q": attn_w_q[0].astype(BF16),
        "q_norm": row(jnp.tile(q_norm[0], N_KV_HEADS)),
        "attn_out": attn_w_out[0].astype(BF16),
    }
    cache_len = cache_k_win.shape[1]
    caches = (cache_k_win.reshape(-1, cache_len, KV_DIM), cache_v_win.reshape(-1, cache_len, KV_DIM))

    y_p, s_p, k_p, v_p = _run_group(x_prompt, 0, None, None, w)
    y_s, s_s, k_s, v_s = _run_group(x_sample, PAST_LEN, state_gla[0], caches, w)
    keep = min(MAX_WINDOW, x_prompt.shape[1])
    return (y_p, y_s, s_p, s_s, k_p[:, -keep:], v_p[:, -keep:], k_s, v_s)
```

```python
import functools

import numpy as np
import jax
import jax.numpy as jnp
from jax import lax
from jax.experimental import pallas as pl
from jax.experimental.pallas import tpu as pltpu

F32 = jnp.float32
BF16 = jnp.bfloat16

D_MODEL = 1024
FFN_DIM = 2688
EPS = 1e-6
PAST_LEN = 8192
GLA_HEADS = 4
GLA_DK = 128
GLA_DV = 256
GLA_RANK = 16
GLA_TAU = 16.0
GLA_QK = GLA_HEADS * GLA_DK
GLA_V = GLA_HEADS * GLA_DV
HEAD_DIM = 64
N_KV_HEADS = 4
KV_DIM = N_KV_HEADS * HEAD_DIM
DIL_GROUPS = ((128, 1), (512, 4), (2048, 16))
N_GROUPS = len(DIL_GROUPS)
MAX_WINDOW = max(w for w, _ in DIL_GROUPS)
ROPE_THETA = 10000.0

LANES = 128
GLA_CHUNK = 128
ATTN_QBLOCK = 128
ROW_TILE = 512
MXU_TILE = 256
FFN_MAIN = (FFN_DIM // MXU_TILE) * MXU_TILE
FFN_REM = FFN_DIM - FFN_MAIN
FFN_CHUNK = 1280
ATTN_SPLIT = 4
ATTN_UNROLL = 16
VMEM_LIMIT = 52 * 1024 * 1024


def _dot(a, b):
    return jnp.dot(a, b, preferred_element_type=F32)


def _dot_nt(a, b):
    return lax.dot_general(a, b, (((1,), (1,)), ((), ())), preferred_element_type=F32)


def _dot_tn(a, b):
    return lax.dot_general(a, b, (((0,), (0,)), ((), ())), preferred_element_type=F32)


def _rms(x, g):
    ms = jnp.mean(x * x, axis=-1, keepdims=True)
    return x * lax.rsqrt(ms + EPS) * g


def _silu(x):
    return x * jax.nn.sigmoid(x)


def _group_mean_sq(x, avg):
    sq = x * x
    hi = sq.astype(BF16)
    lo = (sq - hi.astype(F32)).astype(BF16)
    return _dot(hi, avg) + _dot(lo, avg)


def _rope(x, cos, sin_signed):
    w = x.shape[-1]
    lane = lax.broadcasted_iota(jnp.int32, (1, w), 1)
    first_half = (lane % HEAD_DIM) < (HEAD_DIM // 2)
    rot = jnp.where(first_half, pltpu.roll(x, w - HEAD_DIM // 2, 1), pltpu.roll(x, HEAD_DIM // 2, 1))
    return x * cos + rot * sin_signed


def _ffn_half(x, g_ref, win_ref, wout_ref):
    h = _rms(x, g_ref[...]).astype(BF16)
    acc = None
    for c0 in range(0, FFN_MAIN, FFN_CHUNK):
        c1 = c0 + FFN_CHUNK
        gate = _dot(h, win_ref[:, c0:c1])
        up = _dot(h, win_ref[:, FFN_MAIN + c0:FFN_MAIN + c1])
        act = (_silu(gate) * up).astype(BF16)
        part = _dot(act, wout_ref[c0:c1, :])
        acc = part if acc is None else acc + part
    if FFN_REM:
        gate_up = _dot(h, win_ref[:, 2 * FFN_MAIN:])
        act = (_silu(gate_up[:, :FFN_REM]) * gate_up[:, FFN_REM:]).astype(BF16)
        acc = acc + _dot(act, wout_ref[FFN_MAIN:, :])
    return x + 0.5 * acc


def _ffn_gla_in_kernel(x_ref, g_ffn_ref, win_ref, wout_ref, g_mix_ref, wq_ref, wk_ref, wv_ref, wg_ref,
                       wg2_ref, bg_ref, x1_ref, q_ref, k_ref, v_ref, la_ref):
    x1 = _ffn_half(x_ref[...], g_ffn_ref, win_ref, wout_ref)
    x1_ref[...] = x1
    h = _rms(x1, g_mix_ref[...]).astype(BF16)
    q_ref[...] = _dot(h, wq_ref[...]) * (GLA_DK ** -0.5)
    k_ref[...] = _dot(h, wk_ref[...])
    v_ref[...] = _dot(h, wv_ref[...]).astype(BF16)
    g_lr = _dot(h, wg_ref[...])
    z = _dot(g_lr.astype(BF16), wg2_ref[...]) + bg_ref[...]
    log_sig = jnp.minimum(z, 0.0) - jnp.log1p(jnp.exp(-jnp.abs(z)))
    la_ref[...] = log_sig * (1.0 / GLA_TAU)


def _gla_out_ffn_kernel(o_ref, x1_ref, g_mix_ref, wr_ref, gn_ref, wo_ref, g_ffn_ref, win_ref, wout_ref, x3_ref):
    x1 = x1_ref[...]
    r = _dot(_rms(x1, g_mix_ref[...]).astype(BF16), wr_ref[...])
    o = o_ref[...]
    on = jnp.concatenate([_rms(o[:, h * GLA_DV:(h + 1) * GLA_DV], gn_ref[...]) for h in range(GLA_HEADS)],
                         axis=1)
    x2 = x1 + _dot((on * _silu(r)).astype(BF16), wo_ref[...])
    x3_ref[...] = _ffn_half(x2, g_ffn_ref, win_ref, wout_ref)


def _kv_ffn_q_kernel(x3_ref, g_kv_ref, wkv_ref, kn_ref, avg_ref, g_ffn_ref, win_ref, wout_ref, g_mix_ref,
                     wq_ref, qn_ref, cos_ref, sin_ref, x4_ref, k_ref, v_ref, q0_ref, q1_ref, q2_ref):
    x3 = x3_ref[...]
    cos, sin, avg = cos_ref[...], sin_ref[...], avg_ref[...]
    kv = _dot(_rms(x3, g_kv_ref[...]).astype(BF16), wkv_ref[...])
    k = kv[:, :KV_DIM]
    kn = k * lax.rsqrt(_group_mean_sq(k, avg) + EPS) * kn_ref[...]
    k_ref[...] = _rope(kn, cos, sin)
    v_ref[...] = kv[:, KV_DIM:]
    x4 = _ffn_half(x3, g_ffn_ref, win_ref, wout_ref)
    x4_ref[...] = x4
    h = _rms(x4, g_mix_ref[...]).astype(BF16)
    for g, q_ref in enumerate((q0_ref, q1_ref, q2_ref)):
        q = _dot(h, wq_ref[:, g * KV_DIM:(g + 1) * KV_DIM])
        qn = q * lax.rsqrt(_group_mean_sq(q, avg) + EPS) * qn_ref[...]
        q_ref[...] = _rope(qn, cos, sin) * (HEAD_DIM ** -0.5)


def _attn_out_ffn_kernel(o0_ref, o1_ref, o2_ref, x4_ref, wo_ref, g_ffn_ref, win_ref, wout_ref, y_ref):
    x5 = x4_ref[...]
    for g, o_ref in enumerate((o0_ref, o1_ref, o2_ref)):
        x5 = x5 + _dot(o_ref[...].astype(BF16), wo_ref[g * KV_DIM:(g + 1) * KV_DIM, :])
    y_ref[...] = _ffn_half(x5, g_ffn_ref, win_ref, wout_ref)


def _gla_rec_kernel(*refs, chunk, has_state):
    if has_state:
        q_ref, k_ref, v_ref, la_ref, s0_ref, o_ref, st_ref = refs
    else:
        q_ref, k_ref, v_ref, la_ref, o_ref, st_ref = refs
    c = pl.program_id(1)

    @pl.when(c == 0)
    def _():
        if has_state:
            st_ref[...] = s0_ref[...]
        else:
            st_ref[...] = jnp.zeros_like(st_ref)

    q = q_ref[0]
    k = k_ref[0]
    la = la_ref[0]
    heads = [slice(h * GLA_DK, (h + 1) * GLA_DK) for h in range(GLA_HEADS)]
    row = lax.broadcasted_iota(jnp.int32, (chunk, 1), 0)
    ri = lax.broadcasted_iota(jnp.int32, (chunk, chunk), 0)
    ci = lax.broadcasted_iota(jnp.int32, (chunk, chunk), 1)

    qb = q.astype(BF16)
    kb = k.astype(BF16)
    attn = [jnp.where(ri == ci, _dot_nt(qb[:, hs], kb[:, hs]), 0.0) for hs in heads]

    lsum = la
    rsum = jnp.zeros_like(la)
    s = 1
    while s < chunk:
        qt = (q * jnp.exp(lsum)).astype(BF16)
        kt = (k * jnp.exp(rsum)).astype(BF16)
        mask = ((ri ^ ci) < 2 * s) & ((ri & s) != 0) & ((ci & s) == 0)
        for h, hs in enumerate(heads):
            attn[h] = attn[h] + jnp.where(mask, _dot_nt(qt[:, hs], kt[:, hs]), 0.0)
        btot = lsum + rsum
        odd = (row & s) != 0
        lsum = lsum + jnp.where(odd, pltpu.roll(btot, s, 0), 0.0)
        rsum = rsum + jnp.where(odd, 0.0, pltpu.roll(btot, chunk - s, 0))
        s *= 2

    qt = (q * jnp.exp(lsum)).astype(BF16)
    kd = (k * jnp.exp(rsum)).astype(BF16)
    total = lsum[0:1, :] + rsum[0:1, :]
    for h, hs in enumerate(heads):
        vs = slice(h * GLA_DV, (h + 1) * GLA_DV)
        v = v_ref[0, :, vs]
        st = st_ref[0, h]
        o_ref[0, :, vs] = _dot(attn[h].astype(BF16), v) + _dot(qt[:, hs], st.astype(BF16))
        tot_col = jnp.broadcast_to(total[:, hs], (GLA_DK, GLA_DK)).T[:, 0:1]
        st_ref[0, h] = jnp.exp(tot_col) * st + _dot_tn(kd[:, hs], v)


def _pair_masks():
    lane = lax.broadcasted_iota(jnp.int32, (1, LANES), 1)
    first = lane < HEAD_DIM
    return first, jnp.logical_not(first)


def _attn_prompt_kernel(q0_ref, q1_ref, q2_ref, k_ref, v_ref, o0_ref, o1_ref, o2_ref,
                        qd_ref, kd_ref, vd_ref, od_ref, lse_ref, *, n):
    q_refs = (q0_ref, q1_ref, q2_ref)
    o_refs = (o0_ref, o1_ref, o2_ref)
    head_masks = _pair_masks()
    sub = n // ATTN_SPLIT
    split_groups = [g for g, (_, dil) in enumerate(DIL_GROUPS) if dil % ATTN_SPLIT == 0]

    for cls in range(ATTN_SPLIT):
        src = pl.ds(cls, sub, stride=ATTN_SPLIT)
        dst = pl.ds(cls * sub, sub)
        kd_ref[dst, :] = k_ref[0, src, :]
        vd_ref[dst, :] = v_ref[0, src, :]
        for slot, g in enumerate(split_groups):
            qd_ref[slot, dst, :] = q_refs[g][0, src, :]

    def attend(qt, kt, vt, blk, u0, qblk, nkeys, reach):
        row = lax.broadcasted_iota(jnp.int32, (2 * qblk, 1), 0)
        uq = blk * qblk + (row & (qblk - 1))
        uk = u0 + lax.broadcasted_iota(jnp.int32, (1, nkeys), 1)
        delta = uq - uk
        valid = (delta >= 0) & (delta <= reach)
        qm = jnp.concatenate([jnp.where(hm, qt, 0.0) for hm in head_masks], axis=0).astype(BF16)
        s = jnp.where(valid, _dot_nt(qm, kt), -jnp.inf)
        m = jnp.max(s, axis=-1, keepdims=True)
        p = jnp.exp(s - m)
        den = jnp.sum(p, axis=-1, keepdims=True)
        out = _dot(p.astype(BF16), vt) / den
        lse = m + jnp.log(den)
        return (jnp.where(head_masks[0], out[:qblk], out[qblk:]),
                jnp.where(head_masks[0], lse[:qblk], lse[qblk:]))

    for g, (win, dil) in enumerate(DIL_GROUPS):
        length = n // dil
        qblk = min(ATTN_QBLOCK, length)
        nblk = length // qblk
        nkeys = min(2 * qblk, length)
        reach = win // dil

        if g in split_groups:
            slot = split_groups.index(g)
            step = dil // ATTN_SPLIT

            def tile(idx, carry, g=g, slot=slot, step=step, dil=dil, qblk=qblk, nkeys=nkeys, reach=reach):
                cls = idx % ATTN_SPLIT
                off = (idx // ATTN_SPLIT) % step
                blk = idx // dil
                u0 = jnp.maximum(blk - 1, 0) * qblk
                base = cls * sub + off
                q_rows = pl.ds(base + step * blk * qblk, qblk, stride=step)
                k_rows = pl.ds(base + step * u0, nkeys, stride=step)
                o, lse = attend(qd_ref[slot, q_rows, :], kd_ref[k_rows, :].astype(BF16),
                                vd_ref[k_rows, :].astype(BF16), blk, u0, qblk, nkeys, reach)
                od_ref[slot, q_rows, :] = o
                lse_ref[g, q_rows, :] = lse
                return carry
        else:
            assert dil == 1

            def tile(idx, carry, g=g, qblk=qblk, nkeys=nkeys, reach=reach):
                u0 = jnp.maximum(idx - 1, 0) * qblk
                q_rows = pl.ds(pl.multiple_of(idx * qblk, qblk), qblk)
                k_rows = pl.ds(pl.multiple_of(u0, qblk), nkeys)
                o, lse = attend(q_refs[g][0, q_rows, :], k_ref[0, k_rows, :].astype(BF16),
                                v_ref[0, k_rows, :].astype(BF16), idx, u0, qblk, nkeys, reach)
                o_refs[g][0, q_rows, :] = o
                lse_ref[g, q_rows, :] = lse
                return carry

        lax.fori_loop(0, dil * nblk, tile, 0, unroll=ATTN_UNROLL)

    rows_per_step = 256
    steps_per_class = sub // rows_per_step

    def reweight(t, carry):
        cls = t // steps_per_class
        start = (t % steps_per_class) * rows_per_step
        nat = pl.ds(cls + ATTN_SPLIT * start, rows_per_step, stride=ATTN_SPLIT)
        grp = pl.ds(pl.multiple_of(cls * sub + start, rows_per_step), rows_per_step)
        lse = [lse_ref[g, grp if g in split_groups else nat, :] for g in range(N_GROUPS)]
        top = jnp.maximum(jnp.maximum(lse[0], lse[1]), lse[2])
        e = [jnp.exp(l - top) for l in lse]
        tot = e[0] + e[1] + e[2]
        for g in range(N_GROUPS):
            if g in split_groups:
                o = od_ref[split_groups.index(g), grp, :]
            else:
                o = o_refs[g][0, nat, :]
            o_refs[g][0, nat, :] = o * (e[g] / tot)
        return carry

    lax.fori_loop(0, ATTN_SPLIT * steps_per_class, reweight, 0)


def _attn_sample_kernel(q0_ref, q1_ref, q2_ref, kn_ref, vn_ref, kc_ref, vc_ref, o0_ref, o1_ref, o2_ref,
                        *, n, cache_len):
    q_refs = (q0_ref, q1_ref, q2_ref)
    o_refs = (o0_ref, o1_ref, o2_ref)
    per_head = N_GROUPS * n
    rows = N_KV_HEADS * per_head
    r = lax.broadcasted_iota(jnp.int32, (rows, 1), 0)
    head = r // per_head
    grp = (r % per_head) // n
    tq = r % n
    dil = jnp.zeros_like(r)
    win = jnp.zeros_like(r)
    for g, (w_g, d_g) in enumerate(DIL_GROUPS):
        dil = jnp.where(grp == g, d_g, dil)
        win = jnp.where(grp == g, w_g, win)
    col = lax.broadcasted_iota(jnp.int32, (1, cache_len * N_KV_HEADS), 1)
    d_cache = (cache_len + tq) - col // N_KV_HEADS
    valid_c = (col % N_KV_HEADS == head) & ((d_cache & (dil - 1)) == 0) & (d_cache <= win)
    d_new = tq - lax.broadcasted_iota(jnp.int32, (1, n), 1)
    valid_n = (d_new >= 0) & ((d_new & (dil - 1)) == 0) & (d_new <= win)
    head_lanes = lax.broadcasted_iota(jnp.int32, (1, KV_DIM), 1) // HEAD_DIM == head

    q_wide = jnp.concatenate([q_ref[0] for q_ref in q_refs] * N_KV_HEADS, axis=0)
    q_wide = jnp.where(head_lanes, q_wide, 0.0).astype(BF16)
    q_head = jnp.concatenate(
        [q_ref[0, :, h * HEAD_DIM:(h + 1) * HEAD_DIM] for h in range(N_KV_HEADS) for q_ref in q_refs],
        axis=0).astype(BF16)

    sc = jnp.where(valid_c, _dot_nt(q_head, kc_ref[0].astype(BF16)), -jnp.inf)
    sn = jnp.where(valid_n, _dot_nt(q_wide, kn_ref[0].astype(BF16)), -jnp.inf)
    m = jnp.maximum(jnp.max(sc, axis=-1, keepdims=True), jnp.max(sn, axis=-1, keepdims=True))
    pc = jnp.exp(sc - m)
    pn = jnp.exp(sn - m)
    den = jnp.sum(pc, axis=-1, keepdims=True) + jnp.sum(pn, axis=-1, keepdims=True)
    out_c = _dot(pc.astype(BF16), vc_ref[0].astype(BF16))
    out_n = _dot(pn.astype(BF16), vn_ref[0].astype(BF16))
    lse = m + jnp.log(den)

    outs = [[] for _ in range(N_GROUPS)]
    for h in range(N_KV_HEADS):
        blocks = [slice(h * per_head + g * n, h * per_head + (g + 1) * n) for g in range(N_GROUPS)]
        lses = [lse[b] for b in blocks]
        top = jnp.maximum(jnp.maximum(lses[0], lses[1]), lses[2])
        e = [jnp.exp(l - top) for l in lses]
        tot = e[0] + e[1] + e[2]
        for g, b in enumerate(blocks):
            o = (out_c[b] + out_n[b, h * HEAD_DIM:(h + 1) * HEAD_DIM]) / den[b]
            outs[g].append(o * (e[g] / tot))
    for g in range(N_GROUPS):
        o_refs[g][0] = jnp.concatenate(outs[g], axis=1)


def _params(semantics):
    return pltpu.CompilerParams(dimension_semantics=semantics, vmem_limit_bytes=VMEM_LIMIT)


def _row_spec(tm, width):
    return pl.BlockSpec((tm, width), lambda i: (i, 0))


def _const_spec(shape):
    return pl.BlockSpec(shape, lambda i: (0,) * len(shape), pipeline_mode=pl.Buffered(1))


def _rowwise_call(kernel_fn, name, rows, row_inputs, consts, tables, outs):
    tm = ROW_TILE if rows % ROW_TILE == 0 else rows
    if tables:
        tm = min(tm, tables[0].shape[0])
    assert rows % tm == 0 and all(t.shape[0] % tm == 0 for t in tables)
    in_specs = [_row_spec(tm, a.shape[1]) for a in row_inputs]
    in_specs += [_const_spec(a.shape) for a in consts]
    for t in tables:
        period = t.shape[0] // tm
        in_specs.append(pl.BlockSpec((tm, t.shape[1]), lambda i, period=period: (i % period, 0)))
    return pl.pallas_call(
        kernel_fn,
        grid=(rows // tm,),
        in_specs=in_specs,
        out_specs=[_row_spec(tm, width) for width, _ in outs],
        out_shape=[jax.ShapeDtypeStruct((rows, width), dtype) for width, dtype in outs],
        compiler_params=_params(("parallel",)),
        name=name,
    )(*row_inputs, *consts, *tables)


def _gla_rec_call(q, k, v, la, s0):
    batch, n, _ = q.shape
    chunk = min(GLA_CHUNK, n)
    assert n % chunk == 0 and chunk & (chunk - 1) == 0
    has_state = s0 is not None
    seq_spec = lambda width: pl.BlockSpec((1, chunk, width), lambda b, c: (b, c, 0))
    state_spec = pl.BlockSpec((1, GLA_HEADS, GLA_DK, GLA_DV), lambda b, c: (b, 0, 0, 0))
    in_specs = [seq_spec(GLA_QK), seq_spec(GLA_QK), seq_spec(GLA_V), seq_spec(GLA_QK)]
    args = [q, k, v, la]
    if has_state:
        in_specs.append(state_spec)
        args.append(s0)
    return pl.pallas_call(
        functools.partial(_gla_rec_kernel, chunk=chunk, has_state=has_state),
        grid=(batch, n // chunk),
        in_specs=in_specs,
        out_specs=[seq_spec(GLA_V), state_spec],
        out_shape=[jax.ShapeDtypeStruct((batch, n, GLA_V), F32),
                   jax.ShapeDtypeStruct((batch, GLA_HEADS, GLA_DK, GLA_DV), F32)],
        compiler_params=_params(("parallel", "arbitrary")),
        name="gla_rec",
    )(*args)


def _attn_prompt_call(qs, k, v):
    batch, n, _ = k.shape
    assert n % (ATTN_QBLOCK * max(d for _, d in DIL_GROUPS)) == 0
    n_split = sum(1 for _, d in DIL_GROUPS if d % ATTN_SPLIT == 0)
    spec = pl.BlockSpec((1, n, LANES), lambda b, p: (b, 0, p))
    return pl.pallas_call(
        functools.partial(_attn_prompt_kernel, n=n),
        grid=(batch, KV_DIM // LANES),
        in_specs=[spec] * (N_GROUPS + 2),
        out_specs=[spec] * N_GROUPS,
        out_shape=[jax.ShapeDtypeStruct((batch, n, KV_DIM), F32)] * N_GROUPS,
        scratch_shapes=[pltpu.VMEM((n_split, n, LANES), F32), pltpu.VMEM((n, LANES), F32),
                        pltpu.VMEM((n, LANES), F32), pltpu.VMEM((n_split, n, LANES), F32),
                        pltpu.VMEM((N_GROUPS, n, LANES), F32)],
        compiler_params=_params(("parallel", "parallel")),
        name="attn_prompt",
    )(*qs, k, v)


def _attn_sample_call(qs, k_new, v_new, k_cache, v_cache):
    batch, n, _ = k_new.shape
    cache_len = k_cache.shape[1]
    k_cache = k_cache.reshape(batch, cache_len * N_KV_HEADS, HEAD_DIM)
    v_cache = v_cache.reshape(batch, cache_len * N_KV_HEADS, HEAD_DIM)
    new_spec = pl.BlockSpec((1, n, KV_DIM), lambda b: (b, 0, 0))
    cache_spec = pl.BlockSpec((1, cache_len * N_KV_HEADS, HEAD_DIM), lambda b: (b, 0, 0))
    return pl.pallas_call(
        functools.partial(_attn_sample_kernel, n=n, cache_len=cache_len),
        grid=(batch,),
        in_specs=[new_spec] * (N_GROUPS + 2) + [cache_spec] * 2,
        out_specs=[new_spec] * N_GROUPS,
        out_shape=[jax.ShapeDtypeStruct((batch, n, KV_DIM), F32)] * N_GROUPS,
        compiler_params=_params(("parallel",)),
        name="attn_sample",
    )(*qs, k_new, v_new, k_cache, v_cache)


def _rope_tables(pos0, n, reps):
    half = HEAD_DIM // 2
    inv = ROPE_THETA ** (-np.arange(half, dtype=np.float64) / half)
    ang = (pos0 + np.arange(n, dtype=np.float64))[:, None] * inv[None, :]
    cos = np.concatenate([np.cos(ang), np.cos(ang)], axis=-1)
    sin = np.concatenate([-np.sin(ang), np.sin(ang)], axis=-1)
    cos = np.tile(cos, (reps, N_KV_HEADS)).astype(np.float32)
    sin = np.tile(sin, (reps, N_KV_HEADS)).astype(np.float32)
    return jnp.asarray(cos), jnp.asarray(sin)


def _ffn_in_layout(w_in):
    gate, up = w_in[:, :FFN_DIM], w_in[:, FFN_DIM:]
    return jnp.concatenate([gate[:, :FFN_MAIN], up[:, :FFN_MAIN], gate[:, FFN_MAIN:], up[:, FFN_MAIN:]],
                           axis=1).astype(BF16)


def _head_mean_matrix():
    idx = np.arange(KV_DIM) // HEAD_DIM
    return jnp.asarray((idx[:, None] == idx[None, :]).astype(np.float32) / HEAD_DIM, dtype=BF16)


def _run_group(x, pos0, s0, caches, w):
    batch, n, _ = x.shape
    rows = batch * n
    x = x.reshape(rows, D_MODEL)
    reps = 1 if n % ROW_TILE == 0 else batch
    cos, sin = _rope_tables(pos0, n, reps)
    avg = _head_mean_matrix()

    gain, ffn_in, ffn_out = w["gain"], w["ffn_in"], w["ffn_out"]
    seq = lambda a: a.reshape(batch, n, a.shape[-1])
    wide, qk, kv = (D_MODEL, F32), (GLA_QK, F32), (KV_DIM, F32)

    x1, q, k, v, la = _rowwise_call(
        _ffn_gla_in_kernel, "ffn_gla_in", rows, [x],
        [gain[0][0], ffn_in[0][0], ffn_out[0][0], gain[0][1], *w["gla_qkvg"], w["gla_g2"], w["gla_bg"]], [],
        [wide, qk, qk, (GLA_V, BF16), qk])
    o, s_fin = _gla_rec_call(seq(q), seq(k), seq(v), seq(la), s0)
    (x3,) = _rowwise_call(
        _gla_out_ffn_kernel, "gla_out_ffn", rows, [o.reshape(rows, GLA_V), x1],
        [gain[0][1], w["gla_r"], w["gla_norm"], w["gla_out"], gain[0][2], ffn_in[0][1], ffn_out[0][1]], [],
        [wide])

    x4, k_new, v_new, *qs = _rowwise_call(
        _kv_ffn_q_kernel, "kv_ffn_q", rows, [x3],
        [w["kv_gain"], w["kv_w"], w["k_norm"], avg, gain[1][0], ffn_in[1][0], ffn_out[1][0], gain[1][1],
         w["attn_q"], w["q_norm"]], [cos, sin],
        [wide, kv, kv] + [kv] * N_GROUPS)

    qs = [seq(a) for a in qs]
    if caches is None:
        os_ = _attn_prompt_call(qs, seq(k_new), seq(v_new))
    else:
        os_ = _attn_sample_call(qs, seq(k_new), seq(v_new), *caches)
    (x,) = _rowwise_call(
        _attn_out_ffn_kernel, "attn_out_ffn", rows, [a.reshape(rows, KV_DIM) for a in os_] + [x4],
        [w["attn_out"], gain[1][2], ffn_in[1][1], ffn_out[1][1]], [],
        [wide])

    kv_shape = (batch, n, N_KV_HEADS, HEAD_DIM)
    return x.reshape(batch, n, D_MODEL), s_fin[None], k_new.reshape(kv_shape), v_new.reshape(kv_shape)


def kernel(x_prompt, x_sample, state_gla, cache_k_win, cache_v_win, norm_gains, ffn_w_in, ffn_w_out,
           gla_w_in, gla_w_gate2, gla_b_gate, gla_out_norm, gla_w_out, kv_norm, kv_w, k_norm,
           attn_w_q, q_norm, attn_w_out):
    assert norm_gains.shape[0] == 2 and gla_w_in.shape[0] == 1 and attn_w_q.shape[0] == 1
    row = lambda a: a.reshape(1, -1)
    gw = gla_w_in[0]
    cuts = (0, GLA_QK, 2 * GLA_QK, 2 * GLA_QK + GLA_V, 2 * GLA_QK + GLA_V + GLA_RANK, gw.shape[1])
    wq, wk, wv, wg, wr = (gw[:, a:b].astype(BF16) for a, b in zip(cuts[:-1], cuts[1:]))
    pad = LANES - GLA_RANK
    w = {
        "gain": [[row(norm_gains[l, i]) for i in range(3)] for l in range(2)],
        "ffn_in": [[_ffn_in_layout(ffn_w_in[l, i]) for i in range(2)] for l in range(2)],
        "ffn_out": [[ffn_w_out[l, i].astype(BF16) for i in range(2)] for l in range(2)],
        "gla_qkvg": (wq, wk, wv, jnp.pad(wg, ((0, 0), (0, pad)))),
        "gla_r": wr,
        "gla_g2": jnp.pad(gla_w_gate2[0].astype(BF16), ((0, pad), (0, 0))),
        "gla_bg": row(gla_b_gate[0]),
        "gla_norm": row(gla_out_norm[0]),
        "gla_out": gla_w_out[0].astype(BF16),
        "kv_gain": row(kv_norm),
        "kv_w": kv_w.astype(BF16),
        "k_norm": row(jnp.tile(k_norm, N_KV_HEADS)),
        "attn_q": attn_w_q[0].astype(BF16),
        "q_norm": row(jnp.tile(q_norm[0], N_KV_HEADS)),
        "attn_out": attn_w_out[0].astype(BF16),
    }
    caches = (cache_k_win, cache_v_win)

    y_p, s_p, k_p, v_p = _run_group(x_prompt, 0, None, None, w)
    y_s, s_s, k_s, v_s = _run_group(x_sample, PAST_LEN, state_gla[0], caches, w)
    keep = min(MAX_WINDOW, x_prompt.shape[1])
    return (y_p, y_s, s_p, s_s, k_p[:, -keep:], v_p[:, -keep:], k_s, v_s)
```

```python
import functools

import numpy as np
import jax
import jax.numpy as jnp
from jax import lax
from jax.experimental import pallas as pl
from jax.experimental.pallas import tpu as pltpu

F32 = jnp.float32
BF16 = jnp.bfloat16

D_MODEL = 1024
FFN_DIM = 2688
EPS = 1e-6
PAST_LEN = 8192
GLA_HEADS = 4
GLA_DK = 128
GLA_DV = 256
GLA_RANK = 16
GLA_TAU = 16.0
GLA_QK = GLA_HEADS * GLA_DK
GLA_V = GLA_HEADS * GLA_DV
HEAD_DIM = 64
N_KV_HEADS = 4
KV_DIM = N_KV_HEADS * HEAD_DIM
DIL_GROUPS = ((128, 1), (512, 4), (2048, 16))
N_GROUPS = len(DIL_GROUPS)
MAX_WINDOW = max(w for w, _ in DIL_GROUPS)
ROPE_THETA = 10000.0

LANES = 128
GLA_CHUNK = 128
ATTN_QBLOCK = 128
ROW_TILE = 512
MXU_TILE = 256
FFN_MAIN = (FFN_DIM // MXU_TILE) * MXU_TILE
FFN_REM = FFN_DIM - FFN_MAIN
FFN_CHUNK = 1280
ATTN_SPLIT = 4
ATTN_UNROLL = 16
VMEM_LIMIT = 52 * 1024 * 1024


def _dot(a, b):
    return jnp.dot(a, b, preferred_element_type=F32)


def _dot_nt(a, b):
    return lax.dot_general(a, b, (((1,), (1,)), ((), ())), preferred_element_type=F32)


def _dot_tn(a, b):
    return lax.dot_general(a, b, (((0,), (0,)), ((), ())), preferred_element_type=F32)


def _rms(x, g):
    ms = jnp.mean(x * x, axis=-1, keepdims=True)
    return x * lax.rsqrt(ms + EPS) * g


def _silu(x):
    return x * jax.nn.sigmoid(x)


def _group_mean_sq(x, avg):
    sq = x * x
    hi = sq.astype(BF16)
    lo = (sq - hi.astype(F32)).astype(BF16)
    return _dot(hi, avg) + _dot(lo, avg)


def _rope(x, cos, sin_signed):
    w = x.shape[-1]
    lane = lax.broadcasted_iota(jnp.int32, (1, w), 1)
    first_half = (lane % HEAD_DIM) < (HEAD_DIM // 2)
    rot = jnp.where(first_half, pltpu.roll(x, w - HEAD_DIM // 2, 1), pltpu.roll(x, HEAD_DIM // 2, 1))
    return x * cos + rot * sin_signed


def _ffn_half(x, g_ref, win_ref, wout_ref):
    h = _rms(x, g_ref[...]).astype(BF16)
    acc = None
    for c0 in range(0, FFN_MAIN, FFN_CHUNK):
        c1 = c0 + FFN_CHUNK
        gate = _dot(h, win_ref[:, c0:c1])
        up = _dot(h, win_ref[:, FFN_DIM + c0:FFN_DIM + c1])
        act = (_silu(gate) * up).astype(BF16)
        part = _dot(act, wout_ref[c0:c1, :])
        acc = part if acc is None else acc + part
    if FFN_REM:
        w_rem = jnp.concatenate([win_ref[:, FFN_MAIN:FFN_DIM], win_ref[:, FFN_DIM + FFN_MAIN:]], axis=1)
        gate_up = _dot(h, w_rem)
        act = (_silu(gate_up[:, :FFN_REM]) * gate_up[:, FFN_REM:]).astype(BF16)
        acc = acc + _dot(act, wout_ref[FFN_MAIN:, :])
    return x + 0.5 * acc


def _ffn_gla_in_kernel(x_ref, g_ffn_ref, win_ref, wout_ref, g_mix_ref, wq_ref, wk_ref, wv_ref, wg_ref,
                       wg2_ref, bg_ref, x1_ref, q_ref, k_ref, v_ref, la_ref):
    x1 = _ffn_half(x_ref[...], g_ffn_ref, win_ref, wout_ref)
    x1_ref[...] = x1
    h = _rms(x1, g_mix_ref[...]).astype(BF16)
    q_ref[...] = _dot(h, wq_ref[...]) * (GLA_DK ** -0.5)
    k_ref[...] = _dot(h, wk_ref[...])
    v_ref[...] = _dot(h, wv_ref[...]).astype(BF16)
    g_lr = _dot(h, wg_ref[...])
    z = _dot(g_lr.astype(BF16), wg2_ref[...]) + bg_ref[...]
    log_sig = jnp.minimum(z, 0.0) - jnp.log1p(jnp.exp(-jnp.abs(z)))
    la_ref[...] = log_sig * (1.0 / GLA_TAU)


def _gla_out_ffn_kernel(o_ref, x1_ref, g_mix_ref, wr_ref, gn_ref, wo_ref, g_ffn_ref, win_ref, wout_ref, x3_ref):
    x1 = x1_ref[...]
    r = _dot(_rms(x1, g_mix_ref[...]).astype(BF16), wr_ref[...])
    o = o_ref[...]
    on = jnp.concatenate([_rms(o[:, h * GLA_DV:(h + 1) * GLA_DV], gn_ref[...]) for h in range(GLA_HEADS)],
                         axis=1)
    x2 = x1 + _dot((on * _silu(r)).astype(BF16), wo_ref[...])
    x3_ref[...] = _ffn_half(x2, g_ffn_ref, win_ref, wout_ref)


def _kv_ffn_q_kernel(x3_ref, g_kv_ref, wkv_ref, kn_ref, avg_ref, g_ffn_ref, win_ref, wout_ref, g_mix_ref,
                     wq_ref, qn_ref, cos_ref, sin_ref, x4_ref, k_ref, v_ref, q0_ref, q1_ref, q2_ref):
    x3 = x3_ref[...]
    cos, sin, avg = cos_ref[...], sin_ref[...], avg_ref[...]
    kv = _dot(_rms(x3, g_kv_ref[...]).astype(BF16), wkv_ref[...])
    k = kv[:, :KV_DIM]
    kn = k * lax.rsqrt(_group_mean_sq(k, avg) + EPS) * kn_ref[...]
    k_ref[...] = _rope(kn, cos, sin)
    v_ref[...] = kv[:, KV_DIM:]
    x4 = _ffn_half(x3, g_ffn_ref, win_ref, wout_ref)
    x4_ref[...] = x4
    h = _rms(x4, g_mix_ref[...]).astype(BF16)
    for g, q_ref in enumerate((q0_ref, q1_ref, q2_ref)):
        q = _dot(h, wq_ref[:, g * KV_DIM:(g + 1) * KV_DIM])
        qn = q * lax.rsqrt(_group_mean_sq(q, avg) + EPS) * qn_ref[...]
        q_ref[...] = _rope(qn, cos, sin) * (HEAD_DIM ** -0.5)


def _attn_out_ffn_kernel(o0_ref, o1_ref, o2_ref, x4_ref, wo_ref, g_ffn_ref, win_ref, wout_ref, y_ref):
    x5 = x4_ref[...]
    for g, o_ref in enumerate((o0_ref, o1_ref, o2_ref)):
        x5 = x5 + _dot(o_ref[...].astype(BF16), wo_ref[g * KV_DIM:(g + 1) * KV_DIM, :])
    y_ref[...] = _ffn_half(x5, g_ffn_ref, win_ref, wout_ref)


def _gla_rec_kernel(*refs, chunk, has_state):
    if has_state:
        q_ref, k_ref, v_ref, la_ref, s0_ref, o_ref, st_ref = refs
    else:
        q_ref, k_ref, v_ref, la_ref, o_ref, st_ref = refs
    c = pl.program_id(1)

    @pl.when(c == 0)
    def _():
        if has_state:
            st_ref[...] = s0_ref[...]
        else:
            st_ref[...] = jnp.zeros_like(st_ref)

    q = q_ref[0]
    k = k_ref[0]
    la = la_ref[0]
    heads = [slice(h * GLA_DK, (h + 1) * GLA_DK) for h in range(GLA_HEADS)]
    row = lax.broadcasted_iota(jnp.int32, (chunk, 1), 0)
    ri = lax.broadcasted_iota(jnp.int32, (chunk, chunk), 0)
    ci = lax.broadcasted_iota(jnp.int32, (chunk, chunk), 1)

    qb = q.astype(BF16)
    kb = k.astype(BF16)
    attn = [jnp.where(ri == ci, _dot_nt(qb[:, hs], kb[:, hs]), 0.0) for hs in heads]

    lsum = la
    rsum = jnp.zeros_like(la)
    s = 1
    while s < chunk:
        qt = (q * jnp.exp(lsum)).astype(BF16)
        kt = (k * jnp.exp(rsum)).astype(BF16)
        mask = ((ri ^ ci) < 2 * s) & ((ri & s) != 0) & ((ci & s) == 0)
        for h, hs in enumerate(heads):
            attn[h] = attn[h] + jnp.where(mask, _dot_nt(qt[:, hs], kt[:, hs]), 0.0)
        btot = lsum + rsum
        odd = (row & s) != 0
        lsum = lsum + jnp.where(odd, pltpu.roll(btot, s, 0), 0.0)
        rsum = rsum + jnp.where(odd, 0.0, pltpu.roll(btot, chunk - s, 0))
        s *= 2

    qt = (q * jnp.exp(lsum)).astype(BF16)
    kd = (k * jnp.exp(rsum)).astype(BF16)
    total = lsum[0:1, :] + rsum[0:1, :]
    for h, hs in enumerate(heads):
        vs = slice(h * GLA_DV, (h + 1) * GLA_DV)
        v = v_ref[0, :, vs]
        st = st_ref[0, h]
        o_ref[0, :, vs] = _dot(attn[h].astype(BF16), v) + _dot(qt[:, hs], st.astype(BF16))
        tot_col = jnp.broadcast_to(total[:, hs], (GLA_DK, GLA_DK)).T[:, 0:1]
        st_ref[0, h] = jnp.exp(tot_col) * st + _dot_tn(kd[:, hs], v)


def _pair_masks():
    lane = lax.broadcasted_iota(jnp.int32, (1, LANES), 1)
    first = lane < HEAD_DIM
    return first, jnp.logical_not(first)


def _attn_prompt_kernel(q0_ref, q1_ref, q2_ref, k_ref, v_ref, o0_ref, o1_ref, o2_ref,
                        qd_ref, kd_ref, vd_ref, od_ref, lse_ref, *, n):
    q_refs = (q0_ref, q1_ref, q2_ref)
    o_refs = (o0_ref, o1_ref, o2_ref)
    head_masks = _pair_masks()
    sub = n // ATTN_SPLIT
    split_groups = [g for g, (_, dil) in enumerate(DIL_GROUPS) if dil % ATTN_SPLIT == 0]

    for cls in range(ATTN_SPLIT):
        src = pl.ds(cls, sub, stride=ATTN_SPLIT)
        dst = pl.ds(cls * sub, sub)
        kd_ref[dst, :] = k_ref[0, src, :]
        vd_ref[dst, :] = v_ref[0, src, :]
        for slot, g in enumerate(split_groups):
            qd_ref[slot, dst, :] = q_refs[g][0, src, :]

    def attend(qt, kt, vt, blk, u0, qblk, nkeys, reach):
        row = lax.broadcasted_iota(jnp.int32, (2 * qblk, 1), 0)
        uq = blk * qblk + (row & (qblk - 1))
        uk = u0 + lax.broadcasted_iota(jnp.int32, (1, nkeys), 1)
        delta = uq - uk
        valid = (delta >= 0) & (delta <= reach)
        qm = jnp.concatenate([jnp.where(hm, qt, 0.0) for hm in head_masks], axis=0).astype(BF16)
        s = jnp.where(valid, _dot_nt(qm, kt), -jnp.inf)
        m = jnp.max(s, axis=-1, keepdims=True)
        p = jnp.exp(s - m)
        den = jnp.sum(p, axis=-1, keepdims=True)
        out = _dot(p.astype(BF16), vt) / den
        lse = m + jnp.log(den)
        return (jnp.where(head_masks[0], out[:qblk], out[qblk:]),
                jnp.where(head_masks[0], lse[:qblk], lse[qblk:]))

    for g, (win, dil) in enumerate(DIL_GROUPS):
        length = n // dil
        qblk = min(ATTN_QBLOCK, length)
        nblk = length // qblk
        nkeys = min(2 * qblk, length)
        reach = win // dil

        if g in split_groups:
            slot = split_groups.index(g)
            step = dil // ATTN_SPLIT

            def tile(idx, carry, g=g, slot=slot, step=step, dil=dil, qblk=qblk, nkeys=nkeys, reach=reach):
                cls = idx % ATTN_SPLIT
                off = (idx // ATTN_SPLIT) % step
                blk = idx // dil
                u0 = jnp.maximum(blk - 1, 0) * qblk
                base = cls * sub + off
                q_rows = pl.ds(base + step * blk * qblk, qblk, stride=step)
                k_rows = pl.ds(base + step * u0, nkeys, stride=step)
                o, lse = attend(qd_ref[slot, q_rows, :], kd_ref[k_rows, :].astype(BF16),
                                vd_ref[k_rows, :].astype(BF16), blk, u0, qblk, nkeys, reach)
                od_ref[slot, q_rows, :] = o
                lse_ref[g, q_rows, :] = lse
                return carry
        else:
            assert dil == 1

            def tile(idx, carry, g=g, qblk=qblk, nkeys=nkeys, reach=reach):
                u0 = jnp.maximum(idx - 1, 0) * qblk
                q_rows = pl.ds(pl.multiple_of(idx * qblk, qblk), qblk)
                k_rows = pl.ds(pl.multiple_of(u0, qblk), nkeys)
                o, lse = attend(q_refs[g][0, q_rows, :], k_ref[0, k_rows, :].astype(BF16),
                                v_ref[0, k_rows, :].astype(BF16), idx, u0, qblk, nkeys, reach)
                o_refs[g][0, q_rows, :] = o
                lse_ref[g, q_rows, :] = lse
                return carry

        lax.fori_loop(0, dil * nblk, tile, 0, unroll=ATTN_UNROLL)

    rows_per_step = 256
    steps_per_class = sub // rows_per_step

    def reweight(t, carry):
        cls = t // steps_per_class
        start = (t % steps_per_class) * rows_per_step
        nat = pl.ds(cls + ATTN_SPLIT * start, rows_per_step, stride=ATTN_SPLIT)
        grp = pl.ds(pl.multiple_of(cls * sub + start, rows_per_step), rows_per_step)
        lse = [lse_ref[g, grp if g in split_groups else nat, :] for g in range(N_GROUPS)]
        top = jnp.maximum(jnp.maximum(lse[0], lse[1]), lse[2])
        e = [jnp.exp(l - top) for l in lse]
        tot = e[0] + e[1] + e[2]
        for g in range(N_GROUPS):
            if g in split_groups:
                o = od_ref[split_groups.index(g), grp, :]
            else:
                o = o_refs[g][0, nat, :]
            o_refs[g][0, nat, :] = o * (e[g] / tot)
        return carry

    lax.fori_loop(0, ATTN_SPLIT * steps_per_class, reweight, 0)


def _attn_sample_kernel(q0_ref, q1_ref, q2_ref, kn_ref, vn_ref, kc_ref, vc_ref, o0_ref, o1_ref, o2_ref,
                        *, n, cache_len):
    q_refs = (q0_ref, q1_ref, q2_ref)
    o_refs = (o0_ref, o1_ref, o2_ref)
    per_head = N_GROUPS * n
    rows = 2 * per_head
    r = lax.broadcasted_iota(jnp.int32, (rows, 1), 0)
    grp = (r % per_head) // n
    tq = r % n
    dil = jnp.zeros_like(r)
    win = jnp.zeros_like(r)
    for g, (w_g, d_g) in enumerate(DIL_GROUPS):
        dil = jnp.where(grp == g, d_g, dil)
        win = jnp.where(grp == g, w_g, win)
    d_cache = (cache_len + tq) - lax.broadcasted_iota(jnp.int32, (1, cache_len), 1)
    valid_c = ((d_cache & (dil - 1)) == 0) & (d_cache <= win)
    d_new = tq - lax.broadcasted_iota(jnp.int32, (1, n), 1)
    valid_n = (d_new >= 0) & ((d_new & (dil - 1)) == 0) & (d_new <= win)
    own_lanes = (lax.broadcasted_iota(jnp.int32, (1, LANES), 1) // HEAD_DIM) == (r // per_head)
    first = _pair_masks()[0]

    for pair in range(KV_DIM // LANES):
        ls = slice(pair * LANES, (pair + 1) * LANES)
        q = jnp.concatenate([q_ref[0, :, ls] for q_ref in q_refs] * 2, axis=0)
        q = jnp.where(own_lanes, q, 0.0).astype(BF16)
        sc = jnp.where(valid_c, _dot_nt(q, kc_ref[0, :, ls]), -jnp.inf)
        sn = jnp.where(valid_n, _dot_nt(q, kn_ref[0, :, ls].astype(BF16)), -jnp.inf)
        m = jnp.maximum(jnp.max(sc, axis=-1, keepdims=True), jnp.max(sn, axis=-1, keepdims=True))
        pc = jnp.exp(sc - m)
        pn = jnp.exp(sn - m)
        den = jnp.sum(pc, axis=-1, keepdims=True) + jnp.sum(pn, axis=-1, keepdims=True)
        out = (_dot(pc.astype(BF16), vc_ref[0, :, ls])
               + _dot(pn.astype(BF16), vn_ref[0, :, ls].astype(BF16))) / den
        lse = m + jnp.log(den)
        weighted = []
        for h in range(2):
            blocks = [slice(h * per_head + g * n, h * per_head + (g + 1) * n) for g in range(N_GROUPS)]
            lses = [lse[b] for b in blocks]
            top = jnp.maximum(jnp.maximum(lses[0], lses[1]), lses[2])
            e = [jnp.exp(l - top) for l in lses]
            tot = e[0] + e[1] + e[2]
            weighted.append([out[b] * (e[g] / tot) for g, b in enumerate(blocks)])
        for g in range(N_GROUPS):
            o_refs[g][0, :, ls] = jnp.where(first, weighted[0][g], weighted[1][g])


def _params(semantics):
    return pltpu.CompilerParams(dimension_semantics=semantics, vmem_limit_bytes=VMEM_LIMIT)


def _row_spec(tm, width):
    return pl.BlockSpec((tm, width), lambda i: (i, 0))


def _const_spec(shape):
    return pl.BlockSpec(shape, lambda i: (0,) * len(shape), pipeline_mode=pl.Buffered(1))


def _rowwise_call(kernel_fn, name, rows, row_inputs, consts, tables, outs):
    tm = ROW_TILE if rows % ROW_TILE == 0 else rows
    if tables:
        tm = min(tm, tables[0].shape[0])
    assert rows % tm == 0 and all(t.shape[0] % tm == 0 for t in tables)
    in_specs = [_row_spec(tm, a.shape[1]) for a in row_inputs]
    in_specs += [_const_spec(a.shape) for a in consts]
    for t in tables:
        period = t.shape[0] // tm
        in_specs.append(pl.BlockSpec((tm, t.shape[1]), lambda i, period=period: (i % period, 0)))
    return pl.pallas_call(
        kernel_fn,
        grid=(rows // tm,),
        in_specs=in_specs,
        out_specs=[_row_spec(tm, width) for width, _ in outs],
        out_shape=[jax.ShapeDtypeStruct((rows, width), dtype) for width, dtype in outs],
        compiler_params=_params(("parallel",)),
        name=name,
    )(*row_inputs, *consts, *tables)


def _gla_rec_call(q, k, v, la, s0):
    batch, n, _ = q.shape
    chunk = min(GLA_CHUNK, n)
    assert n % chunk == 0 and chunk & (chunk - 1) == 0
    has_state = s0 is not None
    seq_spec = lambda width: pl.BlockSpec((1, chunk, width), lambda b, c: (b, c, 0))
    state_spec = pl.BlockSpec((1, GLA_HEADS, GLA_DK, GLA_DV), lambda b, c: (b, 0, 0, 0))
    in_specs = [seq_spec(GLA_QK), seq_spec(GLA_QK), seq_spec(GLA_V), seq_spec(GLA_QK)]
    args = [q, k, v, la]
    if has_state:
        in_specs.append(state_spec)
        args.append(s0)
    return pl.pallas_call(
        functools.partial(_gla_rec_kernel, chunk=chunk, has_state=has_state),
        grid=(batch, n // chunk),
        in_specs=in_specs,
        out_specs=[seq_spec(GLA_V), state_spec],
        out_shape=[jax.ShapeDtypeStruct((batch, n, GLA_V), F32),
                   jax.ShapeDtypeStruct((batch, GLA_HEADS, GLA_DK, GLA_DV), F32)],
        compiler_params=_params(("parallel", "arbitrary")),
        name="gla_rec",
    )(*args)


def _attn_prompt_call(qs, k, v):
    batch, n, _ = k.shape
    assert n % (ATTN_QBLOCK * max(d for _, d in DIL_GROUPS)) == 0
    n_split = sum(1 for _, d in DIL_GROUPS if d % ATTN_SPLIT == 0)
    spec = pl.BlockSpec((1, n, LANES), lambda b, p: (b, 0, p))
    return pl.pallas_call(
        functools.partial(_attn_prompt_kernel, n=n),
        grid=(batch, KV_DIM // LANES),
        in_specs=[spec] * (N_GROUPS + 2),
        out_specs=[spec] * N_GROUPS,
        out_shape=[jax.ShapeDtypeStruct((batch, n, KV_DIM), F32)] * N_GROUPS,
        scratch_shapes=[pltpu.VMEM((n_split, n, LANES), F32), pltpu.VMEM((n, LANES), F32),
                        pltpu.VMEM((n, LANES), F32), pltpu.VMEM((n_split, n, LANES), F32),
                        pltpu.VMEM((N_GROUPS, n, LANES), F32)],
        compiler_params=_params(("parallel", "parallel")),
        name="attn_prompt",
    )(*qs, k, v)


def _attn_sample_call(qs, k_new, v_new, k_cache, v_cache):
    batch, n, _ = k_new.shape
    cache_len = k_cache.shape[1]
    k_cache = k_cache.reshape(batch, cache_len, KV_DIM).astype(BF16)
    v_cache = v_cache.reshape(batch, cache_len, KV_DIM).astype(BF16)
    new_spec = pl.BlockSpec((1, n, KV_DIM), lambda b: (b, 0, 0))
    cache_spec = pl.BlockSpec((1, cache_len, KV_DIM), lambda b: (b, 0, 0))
    return pl.pallas_call(
        functools.partial(_attn_sample_kernel, n=n, cache_len=cache_len),
        grid=(batch,),
        in_specs=[new_spec] * (N_GROUPS + 2) + [cache_spec] * 2,
        out_specs=[new_spec] * N_GROUPS,
        out_shape=[jax.ShapeDtypeStruct((batch, n, KV_DIM), F32)] * N_GROUPS,
        compiler_params=_params(("parallel",)),
        name="attn_sample",
    )(*qs, k_new, v_new, k_cache, v_cache)


def _rope_tables(pos0, n, reps):
    half = HEAD_DIM // 2
    inv = ROPE_THETA ** (-np.arange(half, dtype=np.float64) / half)
    ang = (pos0 + np.arange(n, dtype=np.float64))[:, None] * inv[None, :]
    cos = np.concatenate([np.cos(ang), np.cos(ang)], axis=-1)
    sin = np.concatenate([-np.sin(ang), np.sin(ang)], axis=-1)
    cos = np.tile(cos, (reps, N_KV_HEADS)).astype(np.float32)
    sin = np.tile(sin, (reps, N_KV_HEADS)).astype(np.float32)
    return jnp.asarray(cos), jnp.asarray(sin)


def _head_mean_matrix():
    idx = np.arange(KV_DIM) // HEAD_DIM
    return jnp.asarray((idx[:, None] == idx[None, :]).astype(np.float32) / HEAD_DIM, dtype=BF16)


def _run_group(x, pos0, s0, caches, w):
    batch, n, _ = x.shape
    rows = batch * n
    x = x.reshape(rows, D_MODEL)
    reps = 1 if n % ROW_TILE == 0 else batch
    cos, sin = _rope_tables(pos0, n, reps)
    avg = _head_mean_matrix()

    gain, ffn_in, ffn_out = w["gain"], w["ffn_in"], w["ffn_out"]
    seq = lambda a: a.reshape(batch, n, a.shape[-1])
    wide, qk, kv = (D_MODEL, F32), (GLA_QK, F32), (KV_DIM, F32)

    x1, q, k, v, la = _rowwise_call(
        _ffn_gla_in_kernel, "ffn_gla_in", rows, [x],
        [gain[0][0], ffn_in[0][0], ffn_out[0][0], gain[0][1], *w["gla_qkvg"], w["gla_g2"], w["gla_bg"]], [],
        [wide, qk, qk, (GLA_V, BF16), qk])
    o, s_fin = _gla_rec_call(seq(q), seq(k), seq(v), seq(la), s0)
    (x3,) = _rowwise_call(
        _gla_out_ffn_kernel, "gla_out_ffn", rows, [o.reshape(rows, GLA_V), x1],
        [gain[0][1], w["gla_r"], w["gla_norm"], w["gla_out"], gain[0][2], ffn_in[0][1], ffn_out[0][1]], [],
        [wide])

    x4, k_new, v_new, *qs = _rowwise_call(
        _kv_ffn_q_kernel, "kv_ffn_q", rows, [x3],
        [w["kv_gain"], w["kv_w"], w["k_norm"], avg, gain[1][0], ffn_in[1][0], ffn_out[1][0], gain[1][1],
         w["attn_q"], w["q_norm"]], [cos, sin],
        [wide, kv, kv] + [kv] * N_GROUPS)

    qs = [seq(a) for a in qs]
    if caches is None:
        os_ = _attn_prompt_call(qs, seq(k_new), seq(v_new))
    else:
        os_ = _attn_sample_call(qs, seq(k_new), seq(v_new), *caches)
    (x,) = _rowwise_call(
        _attn_out_ffn_kernel, "attn_out_ffn", rows, [a.reshape(rows, KV_DIM) for a in os_] + [x4],
        [w["attn_out"], gain[1][2], ffn_in[1][1], ffn_out[1][1]], [],
        [wide])

    kv_shape = (batch, n, N_KV_HEADS, HEAD_DIM)
    return x.reshape(batch, n, D_MODEL), s_fin[None], k_new.reshape(kv_shape), v_new.reshape(kv_shape)


def kernel(x_prompt, x_sample, state_gla, cache_k_win, cache_v_win, norm_gains, ffn_w_in, ffn_w_out,
           gla_w_in, gla_w_gate2, gla_b_gate, gla_out_norm, gla_w_out, kv_norm, kv_w, k_norm,
           attn_w_q, q_norm, attn_w_out):
    assert norm_gains.shape[0] == 2 and gla_w_in.shape[0] == 1 and attn_w_q.shape[0] == 1
    row = lambda a: a.reshape(1, -1)
    gw = gla_w_in[0]
    cuts = (0, GLA_QK, 2 * GLA_QK, 2 * GLA_QK + GLA_V, 2 * GLA_QK + GLA_V + GLA_RANK, gw.shape[1])
    wq, wk, wv, wg, wr = (gw[:, a:b].astype(BF16) for a, b in zip(cuts[:-1], cuts[1:]))
    pad = LANES - GLA_RANK
    w = {
        "gain": [[row(norm_gains[l, i]) for i in range(3)] for l in range(2)],
        "ffn_in": [[ffn_w_in[l, i].astype(BF16) for i in range(2)] for l in range(2)],
        "ffn_out": [[ffn_w_out[l, i].astype(BF16) for i in range(2)] for l in range(2)],
        "gla_qkvg": (wq, wk, wv, jnp.pad(wg, ((0, 0), (0, pad)))),
        "gla_r": wr,
        "gla_g2": jnp.pad(gla_w_gate2[0].astype(BF16), ((0, pad), (0, 0))),
        "gla_bg": row(gla_b_gate[0]),
        "gla_norm": row(gla_out_norm[0]),
        "gla_out": gla_w_out[0].astype(BF16),
        "kv_gain": row(kv_norm),
        "kv_w": kv_w.astype(BF16),
        "k_norm": row(jnp.tile(k_norm, N_KV_HEADS)),
        "attn_q": attn_w_q[0].astype(BF16),
        "q_norm": row(jnp.tile(q_norm[0], N_KV_HEADS)),
        "attn_out": attn_w_out[0].astype(BF16),
    }
    caches = (cache_k_win, cache_v_win)

    y_p, s_p, k_p, v_p = _run_group(x_prompt, 0, None, None, w)
    y_s, s_s, k_s, v_s = _run_group(x_sample, PAST_LEN, state_gla[0], caches, w)
    keep = min(MAX_WINDOW, x_prompt.shape[1])
    return (y_p, y_s, s_p, s_s, k_p[:, -keep:], v_p[:, -keep:], k_s, v_s)
```

```python
import functools

import numpy as np
import jax
import jax.numpy as jnp
from jax import lax
from jax.experimental import pallas as pl
from jax.experimental.pallas import tpu as pltpu

F32 = jnp.float32
BF16 = jnp.bfloat16

D_MODEL = 1024
FFN_DIM = 2688
EPS = 1e-6
PAST_LEN = 8192
GLA_HEADS = 4
GLA_DK = 128
GLA_DV = 256
GLA_RANK = 16
GLA_TAU = 16.0
GLA_QK = GLA_HEADS * GLA_DK
GLA_V = GLA_HEADS * GLA_DV
HEAD_DIM = 64
N_KV_HEADS = 4
KV_DIM = N_KV_HEADS * HEAD_DIM
DIL_GROUPS = ((128, 1), (512, 4), (2048, 16))
N_GROUPS = len(DIL_GROUPS)
MAX_WINDOW = max(w for w, _ in DIL_GROUPS)
ROPE_THETA = 10000.0

LANES = 128
GLA_CHUNK = 128
ATTN_QBLOCK = 128
ROW_TILE = 512
MXU_TILE = 256
FFN_MAIN = (FFN_DIM // MXU_TILE) * MXU_TILE
FFN_REM = FFN_DIM - FFN_MAIN
FFN_CHUNK = 1280
ATTN_SPLIT = 4
ATTN_UNROLL = 16
VMEM_LIMIT = 52 * 1024 * 1024


def _dot(a, b):
    return jnp.dot(a, b, preferred_element_type=F32)


def _dot_nt(a, b):
    return lax.dot_general(a, b, (((1,), (1,)), ((), ())), preferred_element_type=F32)


def _dot_tn(a, b):
    return lax.dot_general(a, b, (((0,), (0,)), ((), ())), preferred_element_type=F32)


def _rms(x, g):
    ms = jnp.mean(x * x, axis=-1, keepdims=True)
    return x * lax.rsqrt(ms + EPS) * g


def _silu(x):
    return x * jax.nn.sigmoid(x)


def _group_mean_sq(x, avg):
    sq = x * x
    hi = sq.astype(BF16)
    lo = (sq - hi.astype(F32)).astype(BF16)
    return _dot(hi, avg) + _dot(lo, avg)


def _rope(x, cos, sin_signed):
    w = x.shape[-1]
    lane = lax.broadcasted_iota(jnp.int32, (1, w), 1)
    first_half = (lane % HEAD_DIM) < (HEAD_DIM // 2)
    rot = jnp.where(first_half, pltpu.roll(x, w - HEAD_DIM // 2, 1), pltpu.roll(x, HEAD_DIM // 2, 1))
    return x * cos + rot * sin_signed


def _ffn_half(x, g_ref, win_ref, wout_ref):
    h = _rms(x, g_ref[...]).astype(BF16)
    acc = None
    for c0 in range(0, FFN_MAIN, FFN_CHUNK):
        c1 = c0 + FFN_CHUNK
        gate = _dot(h, win_ref[:, c0:c1])
        up = _dot(h, win_ref[:, FFN_DIM + c0:FFN_DIM + c1])
        act = (_silu(gate) * up).astype(BF16)
        part = _dot(act, wout_ref[c0:c1, :])
        acc = part if acc is None else acc + part
    if FFN_REM:
        w_rem = jnp.concatenate([win_ref[:, FFN_MAIN:FFN_DIM], win_ref[:, FFN_DIM + FFN_MAIN:]], axis=1)
        gate_up = _dot(h, w_rem)
        act = (_silu(gate_up[:, :FFN_REM]) * gate_up[:, FFN_REM:]).astype(BF16)
        acc = acc + _dot(act, wout_ref[FFN_MAIN:, :])
    return x + 0.5 * acc


def _ffn_gla_in_kernel(x_ref, g_ffn_ref, win_ref, wout_ref, g_mix_ref, wq_ref, wk_ref, wv_ref, wg_ref,
                       wg2_ref, bg_ref, x1_ref, q_ref, k_ref, v_ref, la_ref):
    x1 = _ffn_half(x_ref[...], g_ffn_ref, win_ref, wout_ref)
    x1_ref[...] = x1
    h = _rms(x1, g_mix_ref[...]).astype(BF16)
    q_ref[...] = _dot(h, wq_ref[...]) * (GLA_DK ** -0.5)
    k_ref[...] = _dot(h, wk_ref[...])
    v_ref[...] = _dot(h, wv_ref[...]).astype(BF16)
    g_lr = _dot(h, wg_ref[...])
    z = _dot(g_lr.astype(BF16), wg2_ref[...]) + bg_ref[...]
    log_sig = jnp.minimum(z, 0.0) - jnp.log1p(jnp.exp(-jnp.abs(z)))
    la_ref[...] = log_sig * (1.0 / GLA_TAU)


def _gla_out_ffn_kernel(o_ref, x1_ref, g_mix_ref, wr_ref, gn_ref, wo_ref, g_ffn_ref, win_ref, wout_ref, x3_ref):
    x1 = x1_ref[...]
    r = _dot(_rms(x1, g_mix_ref[...]).astype(BF16), wr_ref[...])
    o = o_ref[...]
    on = jnp.concatenate([_rms(o[:, h * GLA_DV:(h + 1) * GLA_DV], gn_ref[...]) for h in range(GLA_HEADS)],
                         axis=1)
    x2 = x1 + _dot((on * _silu(r)).astype(BF16), wo_ref[...])
    x3_ref[...] = _ffn_half(x2, g_ffn_ref, win_ref, wout_ref)


def _kv_ffn_q_kernel(x3_ref, g_kv_ref, wkv_ref, kn_ref, avg_ref, g_ffn_ref, win_ref, wout_ref, g_mix_ref,
                     wq_ref, qn_ref, cos_ref, sin_ref, x4_ref, k_ref, v_ref, q0_ref, q1_ref, q2_ref,
                     *kv_t_refs):
    x3 = x3_ref[...]
    cos, sin, avg = cos_ref[...], sin_ref[...], avg_ref[...]
    kv = _dot(_rms(x3, g_kv_ref[...]).astype(BF16), wkv_ref[...])
    k = kv[:, :KV_DIM]
    kn = k * lax.rsqrt(_group_mean_sq(k, avg) + EPS) * kn_ref[...]
    k_rot = _rope(kn, cos, sin)
    k_ref[...] = k_rot
    v_ref[...] = kv[:, KV_DIM:]
    if kv_t_refs:
        kt_ref, vt_ref = kv_t_refs
        kt_ref[0] = k_rot.T
        vt_ref[0] = kv[:, KV_DIM:].T
    x4 = _ffn_half(x3, g_ffn_ref, win_ref, wout_ref)
    x4_ref[...] = x4
    h = _rms(x4, g_mix_ref[...]).astype(BF16)
    for g, q_ref in enumerate((q0_ref, q1_ref, q2_ref)):
        q = _dot(h, wq_ref[:, g * KV_DIM:(g + 1) * KV_DIM])
        qn = q * lax.rsqrt(_group_mean_sq(q, avg) + EPS) * qn_ref[...]
        q_ref[...] = _rope(qn, cos, sin) * (HEAD_DIM ** -0.5)


def _attn_out_ffn_kernel(o0_ref, o1_ref, o2_ref, x4_ref, wo_ref, g_ffn_ref, win_ref, wout_ref, y_ref):
    x5 = x4_ref[...]
    for g, o_ref in enumerate((o0_ref, o1_ref, o2_ref)):
        x5 = x5 + _dot(o_ref[...].astype(BF16), wo_ref[g * KV_DIM:(g + 1) * KV_DIM, :])
    y_ref[...] = _ffn_half(x5, g_ffn_ref, win_ref, wout_ref)


def _gla_rec_kernel(*refs, chunk, has_state):
    if has_state:
        q_ref, k_ref, v_ref, la_ref, s0_ref, o_ref, st_ref = refs
    else:
        q_ref, k_ref, v_ref, la_ref, o_ref, st_ref = refs
    c = pl.program_id(1)

    @pl.when(c == 0)
    def _():
        if has_state:
            st_ref[...] = s0_ref[...]
        else:
            st_ref[...] = jnp.zeros_like(st_ref)

    q = q_ref[0]
    k = k_ref[0]
    la = la_ref[0]
    heads = [slice(h * GLA_DK, (h + 1) * GLA_DK) for h in range(GLA_HEADS)]
    row = lax.broadcasted_iota(jnp.int32, (chunk, 1), 0)
    ri = lax.broadcasted_iota(jnp.int32, (chunk, chunk), 0)
    ci = lax.broadcasted_iota(jnp.int32, (chunk, chunk), 1)

    qb = q.astype(BF16)
    kb = k.astype(BF16)
    attn = [jnp.where(ri == ci, _dot_nt(qb[:, hs], kb[:, hs]), 0.0) for hs in heads]

    lsum = la
    rsum = jnp.zeros_like(la)
    s = 1
    while s < chunk:
        qt = (q * jnp.exp(lsum)).astype(BF16)
        kt = (k * jnp.exp(rsum)).astype(BF16)
        mask = ((ri ^ ci) < 2 * s) & ((ri & s) != 0) & ((ci & s) == 0)
        for h, hs in enumerate(heads):
            attn[h] = attn[h] + jnp.where(mask, _dot_nt(qt[:, hs], kt[:, hs]), 0.0)
        btot = lsum + rsum
        odd = (row & s) != 0
        lsum = lsum + jnp.where(odd, pltpu.roll(btot, s, 0), 0.0)
        rsum = rsum + jnp.where(odd, 0.0, pltpu.roll(btot, chunk - s, 0))
        s *= 2

    qt = (q * jnp.exp(lsum)).astype(BF16)
    kd = (k * jnp.exp(rsum)).astype(BF16)
    total = lsum[0:1, :] + rsum[0:1, :]
    for h, hs in enumerate(heads):
        vs = slice(h * GLA_DV, (h + 1) * GLA_DV)
        v = v_ref[0, :, vs]
        st = st_ref[0, h]
        o_ref[0, :, vs] = _dot(attn[h].astype(BF16), v) + _dot(qt[:, hs], st.astype(BF16))
        tot_col = jnp.broadcast_to(total[:, hs], (GLA_DK, GLA_DK)).T[:, 0:1]
        st_ref[0, h] = jnp.exp(tot_col) * st + _dot_tn(kd[:, hs], v)


def _pair_masks():
    lane = lax.broadcasted_iota(jnp.int32, (1, LANES), 1)
    first = lane < HEAD_DIM
    return first, jnp.logical_not(first)


def _attn_prompt_kernel(q0_ref, q1_ref, q2_ref, k_ref, v_ref, o0_ref, o1_ref, o2_ref,
                        qd_ref, kd_ref, vd_ref, od_ref, lse_ref, *, n):
    q_refs = (q0_ref, q1_ref, q2_ref)
    o_refs = (o0_ref, o1_ref, o2_ref)
    head_masks = _pair_masks()
    sub = n // ATTN_SPLIT
    split_groups = [g for g, (_, dil) in enumerate(DIL_GROUPS) if dil % ATTN_SPLIT == 0]

    for cls in range(ATTN_SPLIT):
        src = pl.ds(cls, sub, stride=ATTN_SPLIT)
        dst = pl.ds(cls * sub, sub)
        kd_ref[dst, :] = k_ref[0, src, :]
        vd_ref[dst, :] = v_ref[0, src, :]
        for slot, g in enumerate(split_groups):
            qd_ref[slot, dst, :] = q_refs[g][0, src, :]

    def attend(qt, kt, vt, blk, u0, qblk, nkeys, reach):
        row = lax.broadcasted_iota(jnp.int32, (2 * qblk, 1), 0)
        uq = blk * qblk + (row & (qblk - 1))
        uk = u0 + lax.broadcasted_iota(jnp.int32, (1, nkeys), 1)
        delta = uq - uk
        valid = (delta >= 0) & (delta <= reach)
        qm = jnp.concatenate([jnp.where(hm, qt, 0.0) for hm in head_masks], axis=0).astype(BF16)
        s = jnp.where(valid, _dot_nt(qm, kt), -jnp.inf)
        m = jnp.max(s, axis=-1, keepdims=True)
        p = jnp.exp(s - m)
        den = jnp.sum(p, axis=-1, keepdims=True)
        out = _dot(p.astype(BF16), vt) / den
        lse = m + jnp.log(den)
        return (jnp.where(head_masks[0], out[:qblk], out[qblk:]),
                jnp.where(head_masks[0], lse[:qblk], lse[qblk:]))

    for g, (win, dil) in enumerate(DIL_GROUPS):
        length = n // dil
        qblk = min(ATTN_QBLOCK, length)
        nblk = length // qblk
        nkeys = min(2 * qblk, length)
        reach = win // dil

        if g in split_groups:
            slot = split_groups.index(g)
            step = dil // ATTN_SPLIT

            def tile(idx, carry, g=g, slot=slot, step=step, dil=dil, qblk=qblk, nkeys=nkeys, reach=reach):
                cls = idx % ATTN_SPLIT
                off = (idx // ATTN_SPLIT) % step
                blk = idx // dil
                u0 = jnp.maximum(blk - 1, 0) * qblk
                base = cls * sub + off
                q_rows = pl.ds(base + step * blk * qblk, qblk, stride=step)
                k_rows = pl.ds(base + step * u0, nkeys, stride=step)
                o, lse = attend(qd_ref[slot, q_rows, :], kd_ref[k_rows, :].astype(BF16),
                                vd_ref[k_rows, :].astype(BF16), blk, u0, qblk, nkeys, reach)
                od_ref[slot, q_rows, :] = o
                lse_ref[g, q_rows, :] = lse
                return carry
        else:
            assert dil == 1

            def tile(idx, carry, g=g, qblk=qblk, nkeys=nkeys, reach=reach):
                u0 = jnp.maximum(idx - 1, 0) * qblk
                q_rows = pl.ds(pl.multiple_of(idx * qblk, qblk), qblk)
                k_rows = pl.ds(pl.multiple_of(u0, qblk), nkeys)
                o, lse = attend(q_refs[g][0, q_rows, :], k_ref[0, k_rows, :].astype(BF16),
                                v_ref[0, k_rows, :].astype(BF16), idx, u0, qblk, nkeys, reach)
                o_refs[g][0, q_rows, :] = o
                lse_ref[g, q_rows, :] = lse
                return carry

        lax.fori_loop(0, dil * nblk, tile, 0, unroll=ATTN_UNROLL)

    rows_per_step = 256
    steps_per_class = sub // rows_per_step

    def reweight(t, carry):
        cls = t // steps_per_class
        start = (t % steps_per_class) * rows_per_step
        nat = pl.ds(cls + ATTN_SPLIT * start, rows_per_step, stride=ATTN_SPLIT)
        grp = pl.ds(pl.multiple_of(cls * sub + start, rows_per_step), rows_per_step)
        lse = [lse_ref[g, grp if g in split_groups else nat, :] for g in range(N_GROUPS)]
        top = jnp.maximum(jnp.maximum(lse[0], lse[1]), lse[2])
        e = [jnp.exp(l - top) for l in lse]
        tot = e[0] + e[1] + e[2]
        for g in range(N_GROUPS):
            if g in split_groups:
                o = od_ref[split_groups.index(g), grp, :]
            else:
                o = o_refs[g][0, nat, :]
            o_refs[g][0, nat, :] = o * (e[g] / tot)
        return carry

    lax.fori_loop(0, ATTN_SPLIT * steps_per_class, reweight, 0)


def _attn_sample_kernel(q0_ref, q1_ref, q2_ref, kn_ref, vn_ref, kc_ref, vc_ref, o0_ref, o1_ref, o2_ref,
                        *, n, cache_len):
    q_refs = (q0_ref, q1_ref, q2_ref)
    o_refs = (o0_ref, o1_ref, o2_ref)
    per_head = N_GROUPS * n
    rows = N_KV_HEADS * per_head
    r = lax.broadcasted_iota(jnp.int32, (rows, 1), 0)
    grp = (r % per_head) // n
    tq = r % n
    dil = jnp.zeros_like(r)
    win = jnp.zeros_like(r)
    for g, (w_g, d_g) in enumerate(DIL_GROUPS):
        dil = jnp.where(grp == g, d_g, dil)
        win = jnp.where(grp == g, w_g, win)
    d_cache = (cache_len + tq) - lax.broadcasted_iota(jnp.int32, (1, cache_len), 1)
    valid_c = ((d_cache & (dil - 1)) == 0) & (d_cache <= win)
    d_new = tq - lax.broadcasted_iota(jnp.int32, (1, n), 1)
    valid_n = (d_new >= 0) & ((d_new & (dil - 1)) == 0) & (d_new <= win)
    own_lanes = (lax.broadcasted_iota(jnp.int32, (1, KV_DIM), 1) // HEAD_DIM) == (r // per_head)

    q = jnp.concatenate([q_ref[0] for q_ref in q_refs] * N_KV_HEADS, axis=0)
    q = jnp.where(own_lanes, q, 0.0).astype(BF16)
    sc = jnp.where(valid_c, _dot(q, kc_ref[0].astype(BF16)), -jnp.inf)
    sn = jnp.where(valid_n, _dot_nt(q, kn_ref[0].astype(BF16)), -jnp.inf)
    m = jnp.maximum(jnp.max(sc, axis=-1, keepdims=True), jnp.max(sn, axis=-1, keepdims=True))
    pc = jnp.exp(sc - m)
    pn = jnp.exp(sn - m)
    den = jnp.sum(pc, axis=-1, keepdims=True) + jnp.sum(pn, axis=-1, keepdims=True)
    out = (_dot_nt(pc.astype(BF16), vc_ref[0].astype(BF16)) + _dot(pn.astype(BF16), vn_ref[0].astype(BF16))) / den
    lse = m + jnp.log(den)
    acc = [None] * N_GROUPS
    for h in range(N_KV_HEADS):
        blocks = [slice(h * per_head + g * n, h * per_head + (g + 1) * n) for g in range(N_GROUPS)]
        lses = [lse[b] for b in blocks]
        top = jnp.maximum(jnp.maximum(lses[0], lses[1]), lses[2])
        e = [jnp.exp(l - top) for l in lses]
        tot = e[0] + e[1] + e[2]
        for g, b in enumerate(blocks):
            term = jnp.where(own_lanes[b], out[b] * (e[g] / tot), 0.0)
            acc[g] = term if acc[g] is None else acc[g] + term
    for g in range(N_GROUPS):
        o_refs[g][0] = acc[g]


def _params(semantics):
    return pltpu.CompilerParams(dimension_semantics=semantics, vmem_limit_bytes=VMEM_LIMIT)


def _row_spec(tm, width):
    return pl.BlockSpec((tm, width), lambda i: (i, 0))


def _const_spec(shape):
    return pl.BlockSpec(shape, lambda i: (0,) * len(shape), pipeline_mode=pl.Buffered(1))


def _member_spec(stack, index):
    lead = len(index)
    shape = (None,) * lead + stack.shape[lead:]
    return pl.BlockSpec(shape, lambda i: tuple(index) + (0,) * (stack.ndim - lead), pipeline_mode=pl.Buffered(1))


def _rowwise_call(kernel_fn, name, rows, row_inputs, consts, tables, outs, seq_len=None, n_transposed=0):
    tm = ROW_TILE if rows % ROW_TILE == 0 else rows
    if tables:
        tm = min(tm, tables[0].shape[0])
    assert rows % tm == 0 and all(t.shape[0] % tm == 0 for t in tables)
    in_specs = [_row_spec(tm, a.shape[1]) for a in row_inputs]
    in_specs += [_member_spec(*c) if isinstance(c, tuple) else _const_spec(c.shape) for c in consts]
    for t in tables:
        period = t.shape[0] // tm
        in_specs.append(pl.BlockSpec((tm, t.shape[1]), lambda i, period=period: (i % period, 0)))
    out_specs = [_row_spec(tm, width) for width, _ in outs]
    out_shape = [jax.ShapeDtypeStruct((rows, width), dtype) for width, dtype in outs]
    if n_transposed:
        assert seq_len % tm == 0
        per_seq = seq_len // tm
        out_specs += [pl.BlockSpec((1, KV_DIM, tm), lambda i: (i // per_seq, 0, i % per_seq))] * n_transposed
        out_shape += [jax.ShapeDtypeStruct((rows // seq_len, KV_DIM, seq_len), F32)] * n_transposed
    return pl.pallas_call(
        kernel_fn,
        grid=(rows // tm,),
        in_specs=in_specs,
        out_specs=out_specs,
        out_shape=out_shape,
        compiler_params=_params(("parallel",)),
        name=name,
    )(*row_inputs, *[c[0] if isinstance(c, tuple) else c for c in consts], *tables)


def _gla_rec_call(q, k, v, la, s0):
    batch, n, _ = q.shape
    chunk = min(GLA_CHUNK, n)
    assert n % chunk == 0 and chunk & (chunk - 1) == 0
    has_state = s0 is not None
    seq_spec = lambda width: pl.BlockSpec((1, chunk, width), lambda b, c: (b, c, 0))
    state_spec = pl.BlockSpec((1, GLA_HEADS, GLA_DK, GLA_DV), lambda b, c: (b, 0, 0, 0))
    in_specs = [seq_spec(GLA_QK), seq_spec(GLA_QK), seq_spec(GLA_V), seq_spec(GLA_QK)]
    args = [q, k, v, la]
    if has_state:
        in_specs.append(state_spec)
        args.append(s0)
    return pl.pallas_call(
        functools.partial(_gla_rec_kernel, chunk=chunk, has_state=has_state),
        grid=(batch, n // chunk),
        in_specs=in_specs,
        out_specs=[seq_spec(GLA_V), state_spec],
        out_shape=[jax.ShapeDtypeStruct((batch, n, GLA_V), F32),
                   jax.ShapeDtypeStruct((batch, GLA_HEADS, GLA_DK, GLA_DV), F32)],
        compiler_params=_params(("parallel", "arbitrary")),
        name="gla_rec",
    )(*args)


def _attn_prompt_call(qs, k, v):
    batch, n, _ = k.shape
    assert n % (ATTN_QBLOCK * max(d for _, d in DIL_GROUPS)) == 0
    n_split = sum(1 for _, d in DIL_GROUPS if d % ATTN_SPLIT == 0)
    spec = pl.BlockSpec((1, n, LANES), lambda b, p: (b, 0, p))
    return pl.pallas_call(
        functools.partial(_attn_prompt_kernel, n=n),
        grid=(batch, KV_DIM // LANES),
        in_specs=[spec] * (N_GROUPS + 2),
        out_specs=[spec] * N_GROUPS,
        out_shape=[jax.ShapeDtypeStruct((batch, n, KV_DIM), F32)] * N_GROUPS,
        scratch_shapes=[pltpu.VMEM((n_split, n, LANES), F32), pltpu.VMEM((n, LANES), F32),
                        pltpu.VMEM((n, LANES), F32), pltpu.VMEM((n_split, n, LANES), F32),
                        pltpu.VMEM((N_GROUPS, n, LANES), F32)],
        compiler_params=_params(("parallel", "parallel")),
        name="attn_prompt",
    )(*qs, k, v)


def _attn_sample_call(qs, k_new, v_new, k_cache, v_cache):
    batch, n, _ = k_new.shape
    cache_len = k_cache.shape[2]
    new_spec = pl.BlockSpec((1, n, KV_DIM), lambda b: (b, 0, 0))
    cache_spec = pl.BlockSpec((1, KV_DIM, cache_len), lambda b: (b, 0, 0))
    return pl.pallas_call(
        functools.partial(_attn_sample_kernel, n=n, cache_len=cache_len),
        grid=(batch,),
        in_specs=[new_spec] * (N_GROUPS + 2) + [cache_spec] * 2,
        out_specs=[new_spec] * N_GROUPS,
        out_shape=[jax.ShapeDtypeStruct((batch, n, KV_DIM), F32)] * N_GROUPS,
        compiler_params=_params(("parallel",)),
        name="attn_sample",
    )(*qs, k_new, v_new, k_cache, v_cache)


def _rope_tables(pos0, n, reps):
    half = HEAD_DIM // 2
    inv = ROPE_THETA ** (-np.arange(half, dtype=np.float64) / half)
    ang = (pos0 + np.arange(n, dtype=np.float64))[:, None] * inv[None, :]
    cos = np.concatenate([np.cos(ang), np.cos(ang)], axis=-1)
    sin = np.concatenate([-np.sin(ang), np.sin(ang)], axis=-1)
    cos = np.tile(cos, (reps, N_KV_HEADS)).astype(np.float32)
    sin = np.tile(sin, (reps, N_KV_HEADS)).astype(np.float32)
    return jnp.asarray(cos), jnp.asarray(sin)


def _head_mean_matrix():
    idx = np.arange(KV_DIM) // HEAD_DIM
    return jnp.asarray((idx[:, None] == idx[None, :]).astype(np.float32) / HEAD_DIM, dtype=BF16)


def _run_group(x, pos0, s0, caches, w):
    batch, n, _ = x.shape
    rows = batch * n
    x = x.reshape(rows, D_MODEL)
    reps = 1 if n % ROW_TILE == 0 else batch
    cos, sin = _rope_tables(pos0, n, reps)
    avg = _head_mean_matrix()

    gain, ffn_in, ffn_out = w["gain"], w["ffn_in"], w["ffn_out"]
    seq = lambda a: a.reshape(batch, n, a.shape[-1])
    wide, qk, kv = (D_MODEL, F32), (GLA_QK, F32), (KV_DIM, F32)

    x1, q, k, v, la = _rowwise_call(
        _ffn_gla_in_kernel, "ffn_gla_in", rows, [x],
        [gain[0][0], ffn_in[0][0], ffn_out[0][0], gain[0][1], *w["gla_qkvg"], w["gla_g2"], w["gla_bg"]], [],
        [wide, qk, qk, (GLA_V, BF16), qk])
    o, s_fin = _gla_rec_call(seq(q), seq(k), seq(v), seq(la), s0)
    (x3,) = _rowwise_call(
        _gla_out_ffn_kernel, "gla_out_ffn", rows, [o.reshape(rows, GLA_V), x1],
        [gain[0][1], w["gla_r"], w["gla_norm"], w["gla_out"], gain[0][2], ffn_in[0][1], ffn_out[0][1]], [],
        [wide])

    transposed = n % ROW_TILE == 0
    x4, k_new, v_new, *rest = _rowwise_call(
        _kv_ffn_q_kernel, "kv_ffn_q", rows, [x3],
        [w["kv_gain"], w["kv_w"], w["k_norm"], avg, gain[1][0], ffn_in[1][0], ffn_out[1][0], gain[1][1],
         w["attn_q"], w["q_norm"]], [cos, sin],
        [wide, kv, kv] + [kv] * N_GROUPS, seq_len=n, n_transposed=2 if transposed else 0)
    qs = rest[:N_GROUPS]

    qs = [seq(a) for a in qs]
    if caches is None:
        os_ = _attn_prompt_call(qs, seq(k_new), seq(v_new))
    else:
        os_ = _attn_sample_call(qs, seq(k_new), seq(v_new), *caches)
    (x,) = _rowwise_call(
        _attn_out_ffn_kernel, "attn_out_ffn", rows, [a.reshape(rows, KV_DIM) for a in os_] + [x4],
        [w["attn_out"], gain[1][2], ffn_in[1][1], ffn_out[1][1]], [],
        [wide])

    if transposed:
        k_out, v_out = (jnp.transpose(a.reshape(batch, N_KV_HEADS, HEAD_DIM, n), (0, 3, 1, 2))
                        for a in rest[N_GROUPS:])
    else:
        k_out, v_out = (a.reshape(batch, n, N_KV_HEADS, HEAD_DIM) for a in (k_new, v_new))
    return x.reshape(batch, n, D_MODEL), s_fin[None], k_out, v_out


def kernel(x_prompt, x_sample, state_gla, cache_k_win, cache_v_win, norm_gains, ffn_w_in, ffn_w_out,
           gla_w_in, gla_w_gate2, gla_b_gate, gla_out_norm, gla_w_out, kv_norm, kv_w, k_norm,
           attn_w_q, q_norm, attn_w_out):
    assert norm_gains.shape[0] == 2 and gla_w_in.shape[0] == 1 and attn_w_q.shape[0] == 1
    row = lambda a: a.reshape(1, -1)
    ffn_in_all, ffn_out_all = ffn_w_in.astype(BF16), ffn_w_out.astype(BF16)
    gw = gla_w_in[0]
    cuts = (0, GLA_QK, 2 * GLA_QK, 2 * GLA_QK + GLA_V, 2 * GLA_QK + GLA_V + GLA_RANK, gw.shape[1])
    wq, wk, wv, wg, wr = (gw[:, a:b].astype(BF16) for a, b in zip(cuts[:-1], cuts[1:]))
    pad = LANES - GLA_RANK
    w = {
        "gain": [[row(norm_gains[l, i]) for i in range(3)] for l in range(2)],
        "ffn_in": [[(ffn_in_all, (l, i)) for i in range(2)] for l in range(2)],
        "ffn_out": [[(ffn_out_all, (l, i)) for i in range(2)] for l in range(2)],
        "gla_qkvg": (wq, wk, wv, jnp.pad(wg, ((0, 0), (0, pad)))),
        "gla_r": wr,
        "gla_g2": jnp.pad(gla_w_gate2[0].astype(BF16), ((0, pad), (0, 0))),
        "gla_bg": row(gla_b_gate[0]),
        "gla_norm": row(gla_out_norm[0]),
        "gla_out": gla_w_out[0].astype(BF16),
        "kv_gain": row(kv_norm),
        "kv_w": kv_w.astype(BF16),
        "k_norm": row(jnp.tile(k_norm, N_KV_HEADS)),
        "attn_q": attn_w_q[0].astype(BF16),
        "q_norm": row(jnp.tile(q_norm[0], N_KV_HEADS)),
        "attn_out": attn_w_out[0].astype(BF16),
    }
    caches = tuple(jnp.transpose(c, (0, 2, 3, 1)).reshape(c.shape[0], KV_DIM, c.shape[1])
                   for c in (cache_k_win, cache_v_win))

    y_p, s_p, k_p, v_p = _run_group(x_prompt, 0, None, None, w)
    y_s, s_s, k_s, v_s = _run_group(x_sample, PAST_LEN, state_gla[0], caches, w)
    keep = min(MAX_WINDOW, x_prompt.shape[1])
    return (y_p, y_s, s_p, s_s, k_p[:, -keep:], v_p[:, -keep:], k_s, v_s)
```

```python
import functools

import numpy as np
import jax
import jax.numpy as jnp
from jax import lax
from jax.experimental import pallas as pl
from jax.experimental.pallas import tpu as pltpu

F32 = jnp.float32
BF16 = jnp.bfloat16

D_MODEL = 1024
FFN_DIM = 2688
EPS = 1e-6
PAST_LEN = 8192
GLA_HEADS = 4
GLA_DK = 128
GLA_DV = 256
GLA_RANK = 16
GLA_TAU = 16.0
GLA_QK = GLA_HEADS * GLA_DK
GLA_V = GLA_HEADS * GLA_DV
HEAD_DIM = 64
N_KV_HEADS = 4
KV_DIM = N_KV_HEADS * HEAD_DIM
DIL_GROUPS = ((128, 1), (512, 4), (2048, 16))
N_GROUPS = len(DIL_GROUPS)
MAX_WINDOW = max(w for w, _ in DIL_GROUPS)
ROPE_THETA = 10000.0

LANES = 128
SUBLANES = 8
GLA_CHUNK = 128
GLA_BATCH_BLOCK = 2
LOG2_E = 1.4426950408889634
ATTN_QBLOCK = 128
ROW_TILE = 512
MXU_TILE = 256
FFN_MAIN = (FFN_DIM // MXU_TILE) * MXU_TILE
FFN_REM = FFN_DIM - FFN_MAIN
FFN_CHUNK = 1280
ATTN_SPLIT = 4
ATTN_UNROLL = 16
VMEM_LIMIT = 52 * 1024 * 1024


def _dot(a, b):
    return jnp.dot(a, b, preferred_element_type=F32)


def _dot_nt(a, b):
    return lax.dot_general(a, b, (((1,), (1,)), ((), ())), preferred_element_type=F32)


def _dot_tn(a, b):
    return lax.dot_general(a, b, (((0,), (0,)), ((), ())), preferred_element_type=F32)


def _rms(x, g):
    ms = jnp.mean(x * x, axis=-1, keepdims=True)
    return x * lax.rsqrt(ms + EPS) * g


def _silu(x):
    return x * jax.nn.sigmoid(x)


def _group_mean_sq(x, avg):
    sq = x * x
    hi = sq.astype(BF16)
    lo = (sq - hi.astype(F32)).astype(BF16)
    return _dot(hi, avg) + _dot(lo, avg)


def _rope(x, cos, sin_signed):
    w = x.shape[-1]
    lane = lax.broadcasted_iota(jnp.int32, (1, w), 1)
    first_half = (lane % HEAD_DIM) < (HEAD_DIM // 2)
    rot = jnp.where(first_half, pltpu.roll(x, w - HEAD_DIM // 2, 1), pltpu.roll(x, HEAD_DIM // 2, 1))
    return x * cos + rot * sin_signed


def _ffn_half(x, g_ref, win_ref, wout_ref):
    h = _rms(x, g_ref[...]).astype(BF16)
    acc = None
    for c0 in range(0, FFN_MAIN, FFN_CHUNK):
        c1 = c0 + FFN_CHUNK
        gate = _dot(h, win_ref[:, c0:c1])
        up = _dot(h, win_ref[:, FFN_DIM + c0:FFN_DIM + c1])
        act = (_silu(gate) * up).astype(BF16)
        part = _dot(act, wout_ref[c0:c1, :])
        acc = part if acc is None else acc + part
    if FFN_REM:
        w_rem = jnp.concatenate([win_ref[:, FFN_MAIN:FFN_DIM], win_ref[:, FFN_DIM + FFN_MAIN:]], axis=1)
        gate_up = _dot(h, w_rem)
        act = (_silu(gate_up[:, :FFN_REM]) * gate_up[:, FFN_REM:]).astype(BF16)
        acc = acc + _dot(act, wout_ref[FFN_MAIN:, :])
    return x + 0.5 * acc


def _ffn_gla_in_kernel(x_ref, g_ffn_ref, win_ref, wout_ref, g_mix_ref, wq_ref, wk_ref, wv_ref, wg_ref,
                       wg2_ref, bg_ref, x1_ref, q_ref, k_ref, v_ref, la_ref):
    x1 = _ffn_half(x_ref[...], g_ffn_ref, win_ref, wout_ref)
    x1_ref[...] = x1
    h = _rms(x1, g_mix_ref[...]).astype(BF16)
    q_ref[...] = _dot(h, wq_ref[...]) * (GLA_DK ** -0.5)
    k_ref[...] = _dot(h, wk_ref[...])
    v_ref[...] = _dot(h, wv_ref[...]).astype(BF16)
    g_lr = _dot(h, wg_ref[...])
    z = _dot(g_lr.astype(BF16), wg2_ref[...]) + bg_ref[...]
    log_sig = jnp.minimum(z, 0.0) - jnp.log1p(jnp.exp(-jnp.abs(z)))
    la_ref[...] = log_sig * (1.0 / GLA_TAU)


def _gla_out_ffn_kernel(o_ref, x1_ref, g_mix_ref, wr_ref, gn_ref, wo_ref, g_ffn_ref, win_ref, wout_ref, x3_ref):
    x1 = x1_ref[...]
    r = _dot(_rms(x1, g_mix_ref[...]).astype(BF16), wr_ref[...])
    o = o_ref[...]
    on = jnp.concatenate([_rms(o[:, h * GLA_DV:(h + 1) * GLA_DV], gn_ref[...]) for h in range(GLA_HEADS)],
                         axis=1)
    x2 = x1 + _dot((on * _silu(r)).astype(BF16), wo_ref[...])
    x3_ref[...] = _ffn_half(x2, g_ffn_ref, win_ref, wout_ref)


def _kv_ffn_q_kernel(x3_ref, g_kv_ref, wkv_ref, kn_ref, avg_ref, g_ffn_ref, win_ref, wout_ref, g_mix_ref,
                     wq_ref, qn_ref, cos_ref, sin_ref, x4_ref, k_ref, v_ref, q0_ref, q1_ref, q2_ref,
                     *kv_t_refs):
    x3 = x3_ref[...]
    cos, sin, avg = cos_ref[...], sin_ref[...], avg_ref[...]
    kv = _dot(_rms(x3, g_kv_ref[...]).astype(BF16), wkv_ref[...])
    k = kv[:, :KV_DIM]
    kn = k * lax.rsqrt(_group_mean_sq(k, avg) + EPS) * kn_ref[...]
    k_rot = _rope(kn, cos, sin)
    k_ref[...] = k_rot
    v_ref[...] = kv[:, KV_DIM:]
    if kv_t_refs:
        kt_ref, vt_ref = kv_t_refs
        kt_ref[0] = k_rot.T
        vt_ref[0] = kv[:, KV_DIM:].T
    x4 = _ffn_half(x3, g_ffn_ref, win_ref, wout_ref)
    x4_ref[...] = x4
    h = _rms(x4, g_mix_ref[...]).astype(BF16)
    for g, q_ref in enumerate((q0_ref, q1_ref, q2_ref)):
        q = _dot(h, wq_ref[:, g * KV_DIM:(g + 1) * KV_DIM])
        qn = q * lax.rsqrt(_group_mean_sq(q, avg) + EPS) * qn_ref[...]
        q_ref[...] = _rope(qn, cos, sin) * (HEAD_DIM ** -0.5)


def _attn_out_ffn_kernel(o0_ref, o1_ref, o2_ref, x4_ref, wo_ref, g_ffn_ref, win_ref, wout_ref, y_ref):
    x5 = x4_ref[...]
    for g, o_ref in enumerate((o0_ref, o1_ref, o2_ref)):
        x5 = x5 + _dot(o_ref[...].astype(BF16), wo_ref[g * KV_DIM:(g + 1) * KV_DIM, :])
    y_ref[...] = _ffn_half(x5, g_ffn_ref, win_ref, wout_ref)


def _gla_rec_kernel(*refs, chunk, has_state):
    if has_state:
        q_ref, k_ref, v_ref, la_ref, m_ref, s0_ref, o_ref, st_ref = refs
    else:
        q_ref, k_ref, v_ref, la_ref, m_ref, o_ref, st_ref = refs
    c = pl.program_id(1)

    @pl.when(c == 0)
    def _():
        if has_state:
            st_ref[...] = s0_ref[...]
        else:
            st_ref[...] = jnp.zeros_like(st_ref)

    heads = [slice(h * GLA_DK, (h + 1) * GLA_DK) for h in range(GLA_HEADS)]
    values = [slice(h * GLA_DV, (h + 1) * GLA_DV) for h in range(GLA_HEADS)]
    row = lax.broadcasted_iota(jnp.int32, (chunk, 1), 0)

    def one_batch_row(b):
        q = q_ref[b]
        k = k_ref[b]
        attn = [None] * GLA_HEADS

        def add_level(qt, kt, level_index):
            for h, hs in enumerate(heads):
                scores = _dot_nt(qt[:, hs], kt[:, hs])
                if level_index is not None:
                    scores = scores * m_ref[level_index]
                attn[h] = scores if attn[h] is None else attn[h] + scores

        add_level(q.astype(BF16), k.astype(BF16), 0)

        lsum = la_ref[b] * LOG2_E
        rsum = jnp.zeros_like(lsum)
        s = 1
        level_index = 0
        while s < chunk:
            level_index += 1
            btot = lsum + rsum
            if s < SUBLANES:
                qt = (q * jnp.exp2(lsum)).astype(BF16)
                kt = (k * jnp.exp2(rsum)).astype(BF16)
                odd = (row & s) != 0
                lsum = lsum + jnp.where(odd, pltpu.roll(btot, s, 0), 0.0)
                rsum = rsum + jnp.where(odd, 0.0, pltpu.roll(btot, chunk - s, 0))
            else:
                zero = jnp.zeros((s, q.shape[1]), F32)
                q_slabs, k_slabs, l_slabs, r_slabs = [], [], [], []
                for lo in range(0, chunk, 2 * s):
                    ev, od = slice(lo, lo + s), slice(lo + s, lo + 2 * s)
                    q_slabs += [zero, q[od] * jnp.exp2(lsum[od])]
                    k_slabs += [k[ev] * jnp.exp2(rsum[ev]), zero]
                    l_slabs += [lsum[ev], lsum[od] + btot[ev]]
                    r_slabs += [rsum[ev] + btot[od], rsum[od]]
                qt = jnp.concatenate(q_slabs, axis=0).astype(BF16)
                kt = jnp.concatenate(k_slabs, axis=0).astype(BF16)
                lsum, rsum = jnp.concatenate(l_slabs, axis=0), jnp.concatenate(r_slabs, axis=0)
            add_level(qt, kt, None if s >= SUBLANES and 2 * s == chunk else level_index)
            s *= 2

        qt = (q * jnp.exp2(lsum)).astype(BF16)
        kd = (k * jnp.exp2(rsum)).astype(BF16)
        total = lsum[0:1, :] + rsum[0:1, :]
        for h, hs in enumerate(heads):
            st = st_ref[b, h]
            v = v_ref[b, :, values[h]]
            o_ref[b, :, values[h]] = _dot(attn[h].astype(BF16), v) + _dot(qt[:, hs], st.astype(BF16))
            tot_col = jnp.broadcast_to(total[:, hs], (GLA_DK, GLA_DK)).T[:, 0:1]
            st_ref[b, h] = jnp.exp2(tot_col) * st + _dot_tn(kd[:, hs], v)

    for b in range(q_ref.shape[0]):
        one_batch_row(b)


def _pair_masks():
    lane = lax.broadcasted_iota(jnp.int32, (1, LANES), 1)
    first = lane < HEAD_DIM
    return first, jnp.logical_not(first)


def _attn_prompt_kernel(q0_ref, q1_ref, q2_ref, k_ref, v_ref, o0_ref, o1_ref, o2_ref,
                        qd_ref, kd_ref, vd_ref, od_ref, lse_ref, *, n):
    q_refs = (q0_ref, q1_ref, q2_ref)
    o_refs = (o0_ref, o1_ref, o2_ref)
    head_masks = _pair_masks()
    sub = n // ATTN_SPLIT
    split_groups = [g for g, (_, dil) in enumerate(DIL_GROUPS) if dil % ATTN_SPLIT == 0]

    for cls in range(ATTN_SPLIT):
        src = pl.ds(cls, sub, stride=ATTN_SPLIT)
        dst = pl.ds(cls * sub, sub)
        kd_ref[dst, :] = k_ref[0, src, :]
        vd_ref[dst, :] = v_ref[0, src, :]
        for slot, g in enumerate(split_groups):
            qd_ref[slot, dst, :] = q_refs[g][0, src, :]

    def attend(qt, kt, vt, blk, u0, qblk, nkeys, reach):
        row = lax.broadcasted_iota(jnp.int32, (2 * qblk, 1), 0)
        uq = blk * qblk + (row & (qblk - 1))
        uk = u0 + lax.broadcasted_iota(jnp.int32, (1, nkeys), 1)
        delta = uq - uk
        valid = (delta >= 0) & (delta <= reach)
        qm = jnp.concatenate([jnp.where(hm, qt, 0.0) for hm in head_masks], axis=0).astype(BF16)
        s = jnp.where(valid, _dot_nt(qm, kt), -jnp.inf)
        m = jnp.max(s, axis=-1, keepdims=True)
        p = jnp.exp(s - m)
        den = jnp.sum(p, axis=-1, keepdims=True)
        out = _dot(p.astype(BF16), vt) / den
        lse = m + jnp.log(den)
        return (jnp.where(head_masks[0], out[:qblk], out[qblk:]),
                jnp.where(head_masks[0], lse[:qblk], lse[qblk:]))

    for g, (win, dil) in enumerate(DIL_GROUPS):
        length = n // dil
        qblk = min(ATTN_QBLOCK, length)
        nblk = length // qblk
        nkeys = min(2 * qblk, length)
        reach = win // dil

        if g in split_groups:
            slot = split_groups.index(g)
            step = dil // ATTN_SPLIT

            def tile(idx, carry, g=g, slot=slot, step=step, dil=dil, qblk=qblk, nkeys=nkeys, reach=reach):
                cls = idx % ATTN_SPLIT
                off = (idx // ATTN_SPLIT) % step
                blk = idx // dil
                u0 = jnp.maximum(blk - 1, 0) * qblk
                base = cls * sub + off
                q_rows = pl.ds(base + step * blk * qblk, qblk, stride=step)
                k_rows = pl.ds(base + step * u0, nkeys, stride=step)
                o, lse = attend(qd_ref[slot, q_rows, :], kd_ref[k_rows, :].astype(BF16),
                                vd_ref[k_rows, :].astype(BF16), blk, u0, qblk, nkeys, reach)
                od_ref[slot, q_rows, :] = o
                lse_ref[g, q_rows, :] = lse
                return carry
        else:
            assert dil == 1

            def tile(idx, carry, g=g, qblk=qblk, nkeys=nkeys, reach=reach):
                u0 = jnp.maximum(idx - 1, 0) * qblk
                q_rows = pl.ds(pl.multiple_of(idx * qblk, qblk), qblk)
                k_rows = pl.ds(pl.multiple_of(u0, qblk), nkeys)
                o, lse = attend(q_refs[g][0, q_rows, :], k_ref[0, k_rows, :].astype(BF16),
                                v_ref[0, k_rows, :].astype(BF16), idx, u0, qblk, nkeys, reach)
                o_refs[g][0, q_rows, :] = o
                lse_ref[g, q_rows, :] = lse
                return carry

        lax.fori_loop(0, dil * nblk, tile, 0, unroll=ATTN_UNROLL)

    rows_per_step = 256
    steps_per_class = sub // rows_per_step

    def reweight(t, carry):
        cls = t // steps_per_class
        start = (t % steps_per_class) * rows_per_step
        nat = pl.ds(cls + ATTN_SPLIT * start, rows_per_step, stride=ATTN_SPLIT)
        grp = pl.ds(pl.multiple_of(cls * sub + start, rows_per_step), rows_per_step)
        lse = [lse_ref[g, grp if g in split_groups else nat, :] for g in range(N_GROUPS)]
        top = jnp.maximum(jnp.maximum(lse[0], lse[1]), lse[2])
        e = [jnp.exp(l - top) for l in lse]
        tot = e[0] + e[1] + e[2]
        for g in range(N_GROUPS):
            if g in split_groups:
                o = od_ref[split_groups.index(g), grp, :]
            else:
                o = o_refs[g][0, nat, :]
            o_refs[g][0, nat, :] = o * (e[g] / tot)
        return carry

    lax.fori_loop(0, ATTN_SPLIT * steps_per_class, reweight, 0)


def _attn_sample_kernel(q0_ref, q1_ref, q2_ref, kn_ref, vn_ref, kc_ref, vc_ref, o0_ref, o1_ref, o2_ref,
                        *, n, cache_len):
    q_refs = (q0_ref, q1_ref, q2_ref)
    o_refs = (o0_ref, o1_ref, o2_ref)
    per_head = N_GROUPS * n
    rows = N_KV_HEADS * per_head
    r = lax.broadcasted_iota(jnp.int32, (rows, 1), 0)
    grp = (r % per_head) // n
    tq = r % n
    dil = jnp.zeros_like(r)
    win = jnp.zeros_like(r)
    for g, (w_g, d_g) in enumerate(DIL_GROUPS):
        dil = jnp.where(grp == g, d_g, dil)
        win = jnp.where(grp == g, w_g, win)
    d_cache = (cache_len + tq) - lax.broadcasted_iota(jnp.int32, (1, cache_len), 1)
    valid_c = ((d_cache & (dil - 1)) == 0) & (d_cache <= win)
    d_new = tq - lax.broadcasted_iota(jnp.int32, (1, n), 1)
    valid_n = (d_new >= 0) & ((d_new & (dil - 1)) == 0) & (d_new <= win)
    own_lanes = (lax.broadcasted_iota(jnp.int32, (1, KV_DIM), 1) // HEAD_DIM) == (r // per_head)

    q = jnp.concatenate([q_ref[0] for q_ref in q_refs] * N_KV_HEADS, axis=0)
    q = jnp.where(own_lanes, q, 0.0).astype(BF16)
    sc = jnp.where(valid_c, _dot(q, kc_ref[0].astype(BF16)), -jnp.inf)
    sn = jnp.where(valid_n, _dot_nt(q, kn_ref[0].astype(BF16)), -jnp.inf)
    m = jnp.maximum(jnp.max(sc, axis=-1, keepdims=True), jnp.max(sn, axis=-1, keepdims=True))
    pc = jnp.exp(sc - m)
    pn = jnp.exp(sn - m)
    den = jnp.sum(pc, axis=-1, keepdims=True) + jnp.sum(pn, axis=-1, keepdims=True)
    out = (_dot_nt(pc.astype(BF16), vc_ref[0].astype(BF16)) + _dot(pn.astype(BF16), vn_ref[0].astype(BF16))) / den
    lse = m + jnp.log(den)
    acc = [None] * N_GROUPS
    for h in range(N_KV_HEADS):
        blocks = [slice(h * per_head + g * n, h * per_head + (g + 1) * n) for g in range(N_GROUPS)]
        lses = [lse[b] for b in blocks]
        top = jnp.maximum(jnp.maximum(lses[0], lses[1]), lses[2])
        e = [jnp.exp(l - top) for l in lses]
        tot = e[0] + e[1] + e[2]
        for g, b in enumerate(blocks):
            term = jnp.where(own_lanes[b], out[b] * (e[g] / tot), 0.0)
            acc[g] = term if acc[g] is None else acc[g] + term
    for g in range(N_GROUPS):
        o_refs[g][0] = acc[g]


def _params(semantics):
    return pltpu.CompilerParams(dimension_semantics=semantics, vmem_limit_bytes=VMEM_LIMIT)


def _row_spec(tm, width):
    return pl.BlockSpec((tm, width), lambda i: (i, 0))


def _const_spec(shape):
    return pl.BlockSpec(shape, lambda i: (0,) * len(shape), pipeline_mode=pl.Buffered(1))


def _member_spec(stack, index):
    lead = len(index)
    shape = (None,) * lead + stack.shape[lead:]
    return pl.BlockSpec(shape, lambda i: tuple(index) + (0,) * (stack.ndim - lead), pipeline_mode=pl.Buffered(1))


def _rowwise_call(kernel_fn, name, rows, row_inputs, consts, tables, outs, seq_len=None, n_transposed=0):
    tm = ROW_TILE if rows % ROW_TILE == 0 else rows
    if tables:
        tm = min(tm, tables[0].shape[0])
    assert rows % tm == 0 and all(t.shape[0] % tm == 0 for t in tables)
    in_specs = [_row_spec(tm, a.shape[1]) for a in row_inputs]
    in_specs += [_member_spec(*c) if isinstance(c, tuple) else _const_spec(c.shape) for c in consts]
    for t in tables:
        period = t.shape[0] // tm
        in_specs.append(pl.BlockSpec((tm, t.shape[1]), lambda i, period=period: (i % period, 0)))
    out_specs = [_row_spec(tm, width) for width, _ in outs]
    out_shape = [jax.ShapeDtypeStruct((rows, width), dtype) for width, dtype in outs]
    if n_transposed:
        assert seq_len % tm == 0
        per_seq = seq_len // tm
        out_specs += [pl.BlockSpec((1, KV_DIM, tm), lambda i: (i // per_seq, 0, i % per_seq))] * n_transposed
        out_shape += [jax.ShapeDtypeStruct((rows // seq_len, KV_DIM, seq_len), F32)] * n_transposed
    return pl.pallas_call(
        kernel_fn,
        grid=(rows // tm,),
        in_specs=in_specs,
        out_specs=out_specs,
        out_shape=out_shape,
        compiler_params=_params(("parallel",)),
        name=name,
    )(*row_inputs, *[c[0] if isinstance(c, tuple) else c for c in consts], *tables)


def _gla_level_masks(chunk):
    i = np.arange(chunk)[:, None]
    j = np.arange(chunk)[None, :]
    levels = [i == j]
    s = 1
    while s < chunk:
        levels.append(((i ^ j) < 2 * s) & ((i & s) != 0) & ((j & s) == 0))
        s *= 2
    return jnp.asarray(np.stack(levels).astype(np.float32))


def _gla_rec_call(q, k, v, la, s0):
    batch, n, _ = q.shape
    chunk = min(GLA_CHUNK, n)
    assert n % chunk == 0 and chunk & (chunk - 1) == 0
    has_state = s0 is not None
    bb = GLA_BATCH_BLOCK if batch % GLA_BATCH_BLOCK == 0 else 1
    seq_spec = lambda width: pl.BlockSpec((bb, chunk, width), lambda b, c: (b, c, 0))
    state_spec = pl.BlockSpec((bb, GLA_HEADS, GLA_DK, GLA_DV), lambda b, c: (b, 0, 0, 0))
    masks = _gla_level_masks(chunk)
    in_specs = [seq_spec(GLA_QK), seq_spec(GLA_QK), seq_spec(GLA_V), seq_spec(GLA_QK),
                pl.BlockSpec(masks.shape, lambda b, c: (0, 0, 0), pipeline_mode=pl.Buffered(1))]
    args = [q, k, v, la, masks]
    if has_state:
        in_specs.append(state_spec)
        args.append(s0)
    return pl.pallas_call(
        functools.partial(_gla_rec_kernel, chunk=chunk, has_state=has_state),
        grid=(batch // bb, n // chunk),
        in_specs=in_specs,
        out_specs=[seq_spec(GLA_V), state_spec],
        out_shape=[jax.ShapeDtypeStruct((batch, n, GLA_V), F32),
                   jax.ShapeDtypeStruct((batch, GLA_HEADS, GLA_DK, GLA_DV), F32)],
        compiler_params=_params(("parallel", "arbitrary")),
        name="gla_rec",
    )(*args)


def _attn_prompt_call(qs, k, v):
    batch, n, _ = k.shape
    assert n % (ATTN_QBLOCK * max(d for _, d in DIL_GROUPS)) == 0
    n_split = sum(1 for _, d in DIL_GROUPS if d % ATTN_SPLIT == 0)
    spec = pl.BlockSpec((1, n, LANES), lambda b, p: (b, 0, p))
    return pl.pallas_call(
        functools.partial(_attn_prompt_kernel, n=n),
        grid=(batch, KV_DIM // LANES),
        in_specs=[spec] * (N_GROUPS + 2),
        out_specs=[spec] * N_GROUPS,
        out_shape=[jax.ShapeDtypeStruct((batch, n, KV_DIM), F32)] * N_GROUPS,
        scratch_shapes=[pltpu.VMEM((n_split, n, LANES), F32), pltpu.VMEM((n, LANES), F32),
                        pltpu.VMEM((n, LANES), F32), pltpu.VMEM((n_split, n, LANES), F32),
                        pltpu.VMEM((N_GROUPS, n, LANES), F32)],
        compiler_params=_params(("parallel", "parallel")),
        name="attn_prompt",
    )(*qs, k, v)


def _attn_sample_call(qs, k_new, v_new, k_cache, v_cache):
    batch, n, _ = k_new.shape
    cache_len = k_cache.shape[2]
    new_spec = pl.BlockSpec((1, n, KV_DIM), lambda b: (b, 0, 0))
    cache_spec = pl.BlockSpec((1, KV_DIM, cache_len), lambda b: (b, 0, 0))
    return pl.pallas_call(
        functools.partial(_attn_sample_kernel, n=n, cache_len=cache_len),
        grid=(batch,),
        in_specs=[new_spec] * (N_GROUPS + 2) + [cache_spec] * 2,
        out_specs=[new_spec] * N_GROUPS,
        out_shape=[jax.ShapeDtypeStruct((batch, n, KV_DIM), F32)] * N_GROUPS,
        compiler_params=_params(("parallel",)),
        name="attn_sample",
    )(*qs, k_new, v_new, k_cache, v_cache)


def _rope_tables(pos0, n, reps):
    half = HEAD_DIM // 2
    inv = ROPE_THETA ** (-np.arange(half, dtype=np.float64) / half)
    ang = (pos0 + np.arange(n, dtype=np.float64))[:, None] * inv[None, :]
    cos = np.concatenate([np.cos(ang), np.cos(ang)], axis=-1)
    sin = np.concatenate([-np.sin(ang), np.sin(ang)], axis=-1)
    cos = np.tile(cos, (reps, N_KV_HEADS)).astype(np.float32)
    sin = np.tile(sin, (reps, N_KV_HEADS)).astype(np.float32)
    return jnp.asarray(cos), jnp.asarray(sin)


def _head_mean_matrix():
    idx = np.arange(KV_DIM) // HEAD_DIM
    return jnp.asarray((idx[:, None] == idx[None, :]).astype(np.float32) / HEAD_DIM, dtype=BF16)


def _run_group(x, pos0, s0, caches, w):
    batch, n, _ = x.shape
    rows = batch * n
    x = x.reshape(rows, D_MODEL)
    reps = 1 if n % ROW_TILE == 0 else batch
    cos, sin = _rope_tables(pos0, n, reps)
    avg = _head_mean_matrix()

    gain, ffn_in, ffn_out = w["gain"], w["ffn_in"], w["ffn_out"]
    seq = lambda a: a.reshape(batch, n, a.shape[-1])
    wide, qk, kv = (D_MODEL, F32), (GLA_QK, F32), (KV_DIM, F32)

    x1, q, k, v, la = _rowwise_call(
        _ffn_gla_in_kernel, "ffn_gla_in", rows, [x],
        [gain[0][0], ffn_in[0][0], ffn_out[0][0], gain[0][1], *w["gla_qkvg"], w["gla_g2"], w["gla_bg"]], [],
        [wide, qk, qk, (GLA_V, BF16), qk])
    o, s_fin = _gla_rec_call(seq(q), seq(k), seq(v), seq(la), s0)
    (x3,) = _rowwise_call(
        _gla_out_ffn_kernel, "gla_out_ffn", rows, [o.reshape(rows, GLA_V), x1],
        [gain[0][1], w["gla_r"], w["gla_norm"], w["gla_out"], gain[0][2], ffn_in[0][1], ffn_out[0][1]], [],
        [wide])

    transposed = n % ROW_TILE == 0
    x4, k_new, v_new, *rest = _rowwise_call(
        _kv_ffn_q_kernel, "kv_ffn_q", rows, [x3],
        [w["kv_gain"], w["kv_w"], w["k_norm"], avg, gain[1][0], ffn_in[1][0], ffn_out[1][0], gain[1][1],
         w["attn_q"], w["q_norm"]], [cos, sin],
        [wide, kv, kv] + [kv] * N_GROUPS, seq_len=n, n_transposed=2 if transposed else 0)
    qs = rest[:N_GROUPS]

    qs = [seq(a) for a in qs]
    if caches is None:
        os_ = _attn_prompt_call(qs, seq(k_new), seq(v_new))
    else:
        os_ = _attn_sample_call(qs, seq(k_new), seq(v_new), *caches)
    (x,) = _rowwise_call(
        _attn_out_ffn_kernel, "attn_out_ffn", rows, [a.reshape(rows, KV_DIM) for a in os_] + [x4],
        [w["attn_out"], gain[1][2], ffn_in[1][1], ffn_out[1][1]], [],
        [wide])

    if transposed:
        k_out, v_out = (jnp.transpose(a.reshape(batch, N_KV_HEADS, HEAD_DIM, n), (0, 3, 1, 2))
                        for a in rest[N_GROUPS:])
    else:
        k_out, v_out = (a.reshape(batch, n, N_KV_HEADS, HEAD_DIM) for a in (k_new, v_new))
    return x.reshape(batch, n, D_MODEL), s_fin[None], k_out, v_out


def kernel(x_prompt, x_sample, state_gla, cache_k_win, cache_v_win, norm_gains, ffn_w_in, ffn_w_out,
           gla_w_in, gla_w_gate2, gla_b_gate, gla_out_norm, gla_w_out, kv_norm, kv_w, k_norm,
           attn_w_q, q_norm, attn_w_out):
    assert norm_gains.shape[0] == 2 and gla_w_in.shape[0] == 1 and attn_w_q.shape[0] == 1
    row = lambda a: a.reshape(1, -1)
    ffn_in_all, ffn_out_all = ffn_w_in.astype(BF16), ffn_w_out.astype(BF16)
    gw = gla_w_in[0]
    cuts = (0, GLA_QK, 2 * GLA_QK, 2 * GLA_QK + GLA_V, 2 * GLA_QK + GLA_V + GLA_RANK, gw.shape[1])
    wq, wk, wv, wg, wr = (gw[:, a:b].astype(BF16) for a, b in zip(cuts[:-1], cuts[1:]))
    pad = LANES - GLA_RANK
    w = {
        "gain": [[row(norm_gains[l, i]) for i in range(3)] for l in range(2)],
        "ffn_in": [[(ffn_in_all, (l, i)) for i in range(2)] for l in range(2)],
        "ffn_out": [[(ffn_out_all, (l, i)) for i in range(2)] for l in range(2)],
        "gla_qkvg": (wq, wk, wv, jnp.pad(wg, ((0, 0), (0, pad)))),
        "gla_r": wr,
        "gla_g2": jnp.pad(gla_w_gate2[0].astype(BF16), ((0, pad), (0, 0))),
        "gla_bg": row(gla_b_gate[0]),
        "gla_norm": row(gla_out_norm[0]),
        "gla_out": gla_w_out[0].astype(BF16),
        "kv_gain": row(kv_norm),
        "kv_w": kv_w.astype(BF16),
        "k_norm": row(jnp.tile(k_norm, N_KV_HEADS)),
        "attn_q": attn_w_q[0].astype(BF16),
        "q_norm": row(jnp.tile(q_norm[0], N_KV_HEADS)),
        "attn_out": attn_w_out[0].astype(BF16),
    }
    caches = tuple(jnp.transpose(c, (0, 2, 3, 1)).reshape(c.shape[0], KV_DIM, c.shape[1])
                   for c in (cache_k_win, cache_v_win))

    y_p, s_p, k_p, v_p = _run_group(x_prompt, 0, None, None, w)
    y_s, s_s, k_s, v_s = _run_group(x_sample, PAST_LEN, state_gla[0], caches, w)
    keep = min(MAX_WINDOW, x_prompt.shape[1])
    return (y_p, y_s, s_p, s_s, k_p[:, -keep:], v_p[:, -keep:], k_s, v_s)
```

```python
import functools

import numpy as np
import jax
import jax.numpy as jnp
from jax import lax
from jax.experimental import pallas as pl
from jax.experimental.pallas import tpu as pltpu

F32 = jnp.float32
BF16 = jnp.bfloat16

D_MODEL = 1024
FFN_DIM = 2688
EPS = 1e-6
PAST_LEN = 8192
GLA_HEADS = 4
GLA_DK = 128
GLA_DV = 256
GLA_RANK = 16
GLA_TAU = 16.0
GLA_QK = GLA_HEADS * GLA_DK
GLA_V = GLA_HEADS * GLA_DV
HEAD_DIM = 64
N_KV_HEADS = 4
KV_DIM = N_KV_HEADS * HEAD_DIM
DIL_GROUPS = ((128, 1), (512, 4), (2048, 16))
N_GROUPS = len(DIL_GROUPS)
MAX_WINDOW = max(w for w, _ in DIL_GROUPS)
ROPE_THETA = 10000.0

LANES = 128
SUBLANES = 8
BF16_ROWS = 16
GLA_CHUNK = 128
GLA_BATCH_BLOCK = 2
LOG2_E = 1.4426950408889634
ATTN_QBLOCK = 128
ROW_TILE = 512
LAST_ROW_TILE = 1024
MXU_TILE = 256
FFN_MAIN = (FFN_DIM // MXU_TILE) * MXU_TILE
FFN_REM = FFN_DIM - FFN_MAIN
FFN_CHUNK = 1280
ATTN_SPLIT = 4
ATTN_UNROLL = 16
VMEM_LIMIT = 52 * 1024 * 1024


def _dot(a, b):
    return jnp.dot(a, b, preferred_element_type=F32)


def _dot_nt(a, b):
    return lax.dot_general(a, b, (((1,), (1,)), ((), ())), preferred_element_type=F32)


def _dot_tn(a, b):
    return lax.dot_general(a, b, (((0,), (0,)), ((), ())), preferred_element_type=F32)


def _rms(x, g):
    ms = jnp.mean(x * x, axis=-1, keepdims=True)
    return x * lax.rsqrt(ms + EPS) * g


def _silu(x):
    return x * jax.nn.sigmoid(x)


def _group_mean_sq(x, avg):
    sq = x * x
    hi = sq.astype(BF16)
    lo = (sq - hi.astype(F32)).astype(BF16)
    return _dot(hi, avg) + _dot(lo, avg)


def _rope(x, cos, sin_signed):
    w = x.shape[-1]
    lane = lax.broadcasted_iota(jnp.int32, (1, w), 1)
    first_half = (lane % HEAD_DIM) < (HEAD_DIM // 2)
    rot = jnp.where(first_half, pltpu.roll(x, w - HEAD_DIM // 2, 1), pltpu.roll(x, HEAD_DIM // 2, 1))
    return x * cos + rot * sin_signed


def _ffn_half(x, g_ref, win_ref, wout_ref):
    h = _rms(x, g_ref[...]).astype(BF16)
    acc = None
    for c0 in range(0, FFN_MAIN, FFN_CHUNK):
        c1 = c0 + FFN_CHUNK
        gate = _dot(h, win_ref[:, c0:c1])
        up = _dot(h, win_ref[:, FFN_DIM + c0:FFN_DIM + c1])
        act = (_silu(gate) * up).astype(BF16)
        part = _dot(act, wout_ref[c0:c1, :])
        acc = part if acc is None else acc + part
    if FFN_REM:
        w_rem = jnp.concatenate([win_ref[:, FFN_MAIN:FFN_DIM], win_ref[:, FFN_DIM + FFN_MAIN:]], axis=1)
        gate_up = _dot(h, w_rem)
        act = (_silu(gate_up[:, :FFN_REM]) * gate_up[:, FFN_REM:]).astype(BF16)
        acc = acc + _dot(act, wout_ref[FFN_MAIN:, :])
    return x + 0.5 * acc


def _ffn_gla_in_kernel(x_ref, g_ffn_ref, win_ref, wout_ref, g_mix_ref, wq_ref, wk_ref, wv_ref, wg_ref,
                       wg2_ref, bg_ref, x1_ref, q_ref, k_ref, v_ref, la_ref):
    x1 = _ffn_half(x_ref[...], g_ffn_ref, win_ref, wout_ref)
    x1_ref[...] = x1
    h = _rms(x1, g_mix_ref[...]).astype(BF16)
    g_lr = _dot(h, wg_ref[...])
    z = _dot(g_lr.astype(BF16), wg2_ref[...]) + bg_ref[...]
    log_sig = jnp.minimum(z, 0.0) - jnp.log1p(jnp.exp(-jnp.abs(z)))
    la_ref[...] = log_sig * (1.0 / GLA_TAU)
    q_ref[...] = _dot(h, wq_ref[...]) * (GLA_DK ** -0.5)
    k_ref[...] = _dot(h, wk_ref[...])
    v_ref[...] = _dot(h, wv_ref[...]).astype(BF16)


def _gla_out_ffn_kernel(o_ref, x1_ref, g_mix_ref, wr_ref, gn_ref, wo_ref, g_ffn_ref, win_ref, wout_ref, x3_ref):
    x1 = x1_ref[...]
    r = _dot(_rms(x1, g_mix_ref[...]).astype(BF16), wr_ref[...])
    o = o_ref[...]
    on = jnp.concatenate([_rms(o[:, h * GLA_DV:(h + 1) * GLA_DV], gn_ref[...]) for h in range(GLA_HEADS)],
                         axis=1)
    x2 = x1 + _dot((on * _silu(r)).astype(BF16), wo_ref[...])
    x3_ref[...] = _ffn_half(x2, g_ffn_ref, win_ref, wout_ref)


def _kv_ffn_q_kernel(x3_ref, g_kv_ref, wkv_ref, kn_ref, avg_ref, g_ffn_ref, win_ref, wout_ref, g_mix_ref,
                     wq_ref, qn_ref, cos_ref, sin_ref, x4_ref, k_ref, v_ref, q0_ref, q1_ref, q2_ref,
                     *kv_t_refs):
    x3 = x3_ref[...]
    cos, sin, avg = cos_ref[...], sin_ref[...], avg_ref[...]
    kv = _dot(_rms(x3, g_kv_ref[...]).astype(BF16), wkv_ref[...])
    k = kv[:, :KV_DIM]
    kn = k * lax.rsqrt(_group_mean_sq(k, avg) + EPS) * kn_ref[...]
    k_rot = _rope(kn, cos, sin)
    k_ref[...] = k_rot
    v_ref[...] = kv[:, KV_DIM:]
    if kv_t_refs:
        kt_ref, vt_ref = kv_t_refs
        kt_ref[0] = k_rot.T
        vt_ref[0] = kv[:, KV_DIM:].T
    x4 = _ffn_half(x3, g_ffn_ref, win_ref, wout_ref)
    x4_ref[...] = x4
    h = _rms(x4, g_mix_ref[...]).astype(BF16)
    for g, q_ref in enumerate((q0_ref, q1_ref, q2_ref)):
        q = _dot(h, wq_ref[:, g * KV_DIM:(g + 1) * KV_DIM])
        qn = q * lax.rsqrt(_group_mean_sq(q, avg) + EPS) * qn_ref[...]
        q_ref[...] = _rope(qn, cos, sin) * (HEAD_DIM ** -0.5)


def _attn_out_ffn_kernel(o0_ref, o1_ref, o2_ref, x4_ref, wo_ref, g_ffn_ref, win_ref, wout_ref, y_ref):
    x5 = x4_ref[...]
    for g, o_ref in enumerate((o0_ref, o1_ref, o2_ref)):
        x5 = x5 + _dot(o_ref[...].astype(BF16), wo_ref[g * KV_DIM:(g + 1) * KV_DIM, :])
    y_ref[...] = _ffn_half(x5, g_ffn_ref, win_ref, wout_ref)


def _gla_rec_kernel(*refs, chunk, has_state, n_cast):
    q_ref, k_ref, v_ref, la_ref, m_ref = refs[:5]
    refs = refs[5:]
    if has_state:
        s0_ref, refs = refs[0], refs[1:]
    cast_src, (o_ref, st_ref), cast_dst = refs[:n_cast], refs[n_cast:n_cast + 2], refs[n_cast + 2:]
    c = pl.program_id(1)

    for src_ref, dst_ref in zip(cast_src, cast_dst):
        dst_ref[...] = src_ref[...].astype(BF16)

    @pl.when(c == 0)
    def _():
        if has_state:
            st_ref[...] = s0_ref[...]
        else:
            st_ref[...] = jnp.zeros_like(st_ref)

    heads = [slice(h * GLA_DK, (h + 1) * GLA_DK) for h in range(GLA_HEADS)]
    values = [slice(h * GLA_DV, (h + 1) * GLA_DV) for h in range(GLA_HEADS)]
    row = lax.broadcasted_iota(jnp.int32, (chunk, 1), 0)

    def one_batch_row(b):
        q = q_ref[b]
        k = k_ref[b]
        attn = [None] * GLA_HEADS

        def add_level(qt, kt, level_index):
            for h, hs in enumerate(heads):
                scores = _dot_nt(qt[:, hs], kt[:, hs])
                if level_index is not None:
                    scores = scores * m_ref[level_index]
                attn[h] = scores if attn[h] is None else attn[h] + scores

        add_level(q.astype(BF16), k.astype(BF16), 0)

        lsum = la_ref[b] * LOG2_E
        rsum = jnp.zeros_like(lsum)
        s = 1
        level_index = 0
        while s < chunk:
            level_index += 1
            btot = lsum + rsum
            if s < SUBLANES:
                qt = (q * jnp.exp2(lsum)).astype(BF16)
                kt = (k * jnp.exp2(rsum)).astype(BF16)
                odd = (row & s) != 0
                lsum = lsum + jnp.where(odd, pltpu.roll(btot, s, 0), 0.0)
                rsum = rsum + jnp.where(odd, 0.0, pltpu.roll(btot, chunk - s, 0))
            else:
                zero = jnp.zeros((s, q.shape[1]), F32)
                q_slabs, k_slabs, l_slabs, r_slabs = [], [], [], []
                for lo in range(0, chunk, 2 * s):
                    ev, od = slice(lo, lo + s), slice(lo + s, lo + 2 * s)
                    q_slabs += [zero, q[od] * jnp.exp2(lsum[od])]
                    k_slabs += [k[ev] * jnp.exp2(rsum[ev]), zero]
                    l_slabs += [lsum[ev], lsum[od] + btot[ev]]
                    r_slabs += [rsum[ev] + btot[od], rsum[od]]
                qt = jnp.concatenate(q_slabs, axis=0).astype(BF16)
                kt = jnp.concatenate(k_slabs, axis=0).astype(BF16)
                lsum, rsum = jnp.concatenate(l_slabs, axis=0), jnp.concatenate(r_slabs, axis=0)
            add_level(qt, kt, None if s >= SUBLANES and 2 * s == chunk else level_index)
            s *= 2

        qt = (q * jnp.exp2(lsum)).astype(BF16)
        kd = (k * jnp.exp2(rsum)).astype(BF16)
        total = lsum[0:1, :] + rsum[0:1, :]
        for h, hs in enumerate(heads):
            st = st_ref[b, h]
            v = v_ref[b, :, values[h]]
            o_ref[b, :, values[h]] = _dot(attn[h].astype(BF16), v) + _dot(qt[:, hs], st.astype(BF16))
            tot_col = jnp.broadcast_to(total[:, hs], (GLA_DK, GLA_DK)).T[:, 0:1]
            st_ref[b, h] = jnp.exp2(tot_col) * st + _dot_tn(kd[:, hs], v)

    for b in range(q_ref.shape[0]):
        one_batch_row(b)


def _pair_masks():
    lane = lax.broadcasted_iota(jnp.int32, (1, LANES), 1)
    first = lane < HEAD_DIM
    return first, jnp.logical_not(first)


def _attn_prompt_kernel(q0_ref, q1_ref, q2_ref, k_ref, v_ref, o0_ref, o1_ref, o2_ref,
                        qd_ref, kd_ref, vd_ref, od_ref, lse_ref, *, n):
    q_refs = (q0_ref, q1_ref, q2_ref)
    o_refs = (o0_ref, o1_ref, o2_ref)
    head_masks = _pair_masks()
    sub = n // ATTN_SPLIT
    split_groups = [g for g, (_, dil) in enumerate(DIL_GROUPS) if dil % ATTN_SPLIT == 0]

    for cls in range(ATTN_SPLIT):
        src = pl.ds(cls, sub, stride=ATTN_SPLIT)
        dst = pl.ds(cls * sub, sub)
        kd_ref[dst, :] = k_ref[0, src, :]
        vd_ref[dst, :] = v_ref[0, src, :]
        for slot, g in enumerate(split_groups):
            qd_ref[slot, dst, :] = q_refs[g][0, src, :]

    def attend(qt, kt, vt, blk, u0, qblk, nkeys, reach):
        row = lax.broadcasted_iota(jnp.int32, (2 * qblk, 1), 0)
        uq = blk * qblk + (row & (qblk - 1))
        uk = u0 + lax.broadcasted_iota(jnp.int32, (1, nkeys), 1)
        delta = uq - uk
        valid = (delta >= 0) & (delta <= reach)
        qm = jnp.concatenate([jnp.where(hm, qt, 0.0) for hm in head_masks], axis=0).astype(BF16)
        s = jnp.where(valid, _dot_nt(qm, kt), -jnp.inf)
        m = jnp.max(s, axis=-1, keepdims=True)
        p = jnp.exp(s - m)
        den = jnp.sum(p, axis=-1, keepdims=True)
        out = _dot(p.astype(BF16), vt) / den
        lse = m + jnp.log(den)
        return (jnp.where(head_masks[0], out[:qblk], out[qblk:]),
                jnp.where(head_masks[0], lse[:qblk], lse[qblk:]))

    for g, (win, dil) in enumerate(DIL_GROUPS):
        length = n // dil
        qblk = min(ATTN_QBLOCK, length)
        nblk = length // qblk
        nkeys = min(2 * qblk, length)
        reach = win // dil

        if g in split_groups:
            slot = split_groups.index(g)
            step = dil // ATTN_SPLIT

            def tile(idx, carry, g=g, slot=slot, step=step, dil=dil, qblk=qblk, nkeys=nkeys, reach=reach):
                cls = idx % ATTN_SPLIT
                off = (idx // ATTN_SPLIT) % step
                blk = idx // dil
                u0 = jnp.maximum(blk - 1, 0) * qblk
                base = cls * sub + off
                q_rows = pl.ds(base + step * blk * qblk, qblk, stride=step)
                k_rows = pl.ds(base + step * u0, nkeys, stride=step)
                o, lse = attend(qd_ref[slot, q_rows, :], kd_ref[k_rows, :].astype(BF16),
                                vd_ref[k_rows, :].astype(BF16), blk, u0, qblk, nkeys, reach)
                od_ref[slot, q_rows, :] = o
                lse_ref[g, q_rows, :] = lse
                return carry
        else:
            assert dil == 1

            def tile(idx, carry, g=g, qblk=qblk, nkeys=nkeys, reach=reach):
                u0 = jnp.maximum(idx - 1, 0) * qblk
                q_rows = pl.ds(pl.multiple_of(idx * qblk, qblk), qblk)
                k_rows = pl.ds(pl.multiple_of(u0, qblk), nkeys)
                o, lse = attend(q_refs[g][0, q_rows, :], k_ref[0, k_rows, :].astype(BF16),
                                v_ref[0, k_rows, :].astype(BF16), idx, u0, qblk, nkeys, reach)
                o_refs[g][0, q_rows, :] = o
                lse_ref[g, q_rows, :] = lse
                return carry

        lax.fori_loop(0, dil * nblk, tile, 0, unroll=ATTN_UNROLL)

    rows_per_step = 256
    steps_per_class = sub // rows_per_step

    def reweight(t, carry):
        cls = t // steps_per_class
        start = (t % steps_per_class) * rows_per_step
        nat = pl.ds(cls + ATTN_SPLIT * start, rows_per_step, stride=ATTN_SPLIT)
        grp = pl.ds(pl.multiple_of(cls * sub + start, rows_per_step), rows_per_step)
        lse = [lse_ref[g, grp if g in split_groups else nat, :] for g in range(N_GROUPS)]
        top = jnp.maximum(jnp.maximum(lse[0], lse[1]), lse[2])
        e = [jnp.exp(l - top) for l in lse]
        tot = e[0] + e[1] + e[2]
        for g in range(N_GROUPS):
            if g in split_groups:
                o = od_ref[split_groups.index(g), grp, :]
            else:
                o = o_refs[g][0, nat, :]
            o_refs[g][0, nat, :] = o * (e[g] / tot)
        return carry

    lax.fori_loop(0, ATTN_SPLIT * steps_per_class, reweight, 0)


def _attn_sample_kernel(q0_ref, q1_ref, q2_ref, kn_ref, vn_ref, kc_ref, vc_ref, o0_ref, o1_ref, o2_ref,
                        *, n, cache_len):
    q_refs = (q0_ref, q1_ref, q2_ref)
    o_refs = (o0_ref, o1_ref, o2_ref)
    per_head = N_GROUPS * n
    rows = N_KV_HEADS * per_head
    r = lax.broadcasted_iota(jnp.int32, (rows, 1), 0)
    grp = (r % per_head) // n
    tq = r % n
    dil = jnp.zeros_like(r)
    win = jnp.zeros_like(r)
    for g, (w_g, d_g) in enumerate(DIL_GROUPS):
        dil = jnp.where(grp == g, d_g, dil)
        win = jnp.where(grp == g, w_g, win)
    d_cache = (cache_len + tq) - lax.broadcasted_iota(jnp.int32, (1, cache_len), 1)
    valid_c = ((d_cache & (dil - 1)) == 0) & (d_cache <= win)
    d_new = tq - lax.broadcasted_iota(jnp.int32, (1, n), 1)
    valid_n = (d_new >= 0) & ((d_new & (dil - 1)) == 0) & (d_new <= win)
    own_lanes = (lax.broadcasted_iota(jnp.int32, (1, KV_DIM), 1) // HEAD_DIM) == (r // per_head)

    q = jnp.concatenate([q_ref[0] for q_ref in q_refs] * N_KV_HEADS, axis=0)
    q = jnp.where(own_lanes, q, 0.0).astype(BF16)
    sc = jnp.where(valid_c, _dot(q, kc_ref[0].astype(BF16)), -jnp.inf)
    sn = jnp.where(valid_n, _dot_nt(q, kn_ref[0].astype(BF16)), -jnp.inf)
    m = jnp.maximum(jnp.max(sc, axis=-1, keepdims=True), jnp.max(sn, axis=-1, keepdims=True))
    pc = jnp.exp(sc - m)
    pn = jnp.exp(sn - m)
    den = jnp.sum(pc, axis=-1, keepdims=True) + jnp.sum(pn, axis=-1, keepdims=True)
    out = (_dot_nt(pc.astype(BF16), vc_ref[0].astype(BF16)) + _dot(pn.astype(BF16), vn_ref[0].astype(BF16))) / den
    lse = m + jnp.log(den)
    acc = [None] * N_GROUPS
    for h in range(N_KV_HEADS):
        blocks = [slice(h * per_head + g * n, h * per_head + (g + 1) * n) for g in range(N_GROUPS)]
        lses = [lse[b] for b in blocks]
        top = jnp.maximum(jnp.maximum(lses[0], lses[1]), lses[2])
        e = [jnp.exp(l - top) for l in lses]
        tot = e[0] + e[1] + e[2]
        for g, b in enumerate(blocks):
            term = jnp.where(own_lanes[b], out[b] * (e[g] / tot), 0.0)
            acc[g] = term if acc[g] is None else acc[g] + term
    for g in range(N_GROUPS):
        o_refs[g][0] = acc[g]


def _params(semantics):
    return pltpu.CompilerParams(dimension_semantics=semantics, vmem_limit_bytes=VMEM_LIMIT)


def _row_spec(tm, width):
    return pl.BlockSpec((tm, width), lambda i: (i, 0))


def _const_spec(shape):
    return pl.BlockSpec(shape, lambda i: (0,) * len(shape), pipeline_mode=pl.Buffered(1))


def _member_spec(stack, index):
    lead = len(index)
    shape = (None,) * lead + stack.shape[lead:]
    return pl.BlockSpec(shape, lambda i: tuple(index) + (0,) * (stack.ndim - lead), pipeline_mode=pl.Buffered(1))


def _rowwise_call(kernel_fn, name, rows, row_inputs, consts, tables, outs, seq_len=None, n_transposed=0,
                  row_tile=ROW_TILE):
    tm = row_tile if rows % row_tile == 0 else rows
    if tables:
        tm = min(tm, tables[0].shape[0])
    assert rows % tm == 0 and all(t.shape[0] % tm == 0 for t in tables)
    in_specs = [_row_spec(tm, a.shape[1]) for a in row_inputs]
    in_specs += [_member_spec(*c) if isinstance(c, tuple) else _const_spec(c.shape) for c in consts]
    for t in tables:
        period = t.shape[0] // tm
        in_specs.append(pl.BlockSpec((tm, t.shape[1]), lambda i, period=period: (i % period, 0)))
    out_specs = [_row_spec(tm, width) for width, _ in outs]
    out_shape = [jax.ShapeDtypeStruct((rows, width), dtype) for width, dtype in outs]
    if n_transposed:
        assert seq_len % tm == 0
        per_seq = seq_len // tm
        out_specs += [pl.BlockSpec((1, KV_DIM, tm), lambda i: (i // per_seq, 0, i % per_seq))] * n_transposed
        out_shape += [jax.ShapeDtypeStruct((rows // seq_len, KV_DIM, seq_len), F32)] * n_transposed
    return pl.pallas_call(
        kernel_fn,
        grid=(rows // tm,),
        in_specs=in_specs,
        out_specs=out_specs,
        out_shape=out_shape,
        compiler_params=_params(("parallel",)),
        name=name,
    )(*row_inputs, *[c[0] if isinstance(c, tuple) else c for c in consts], *tables)


def _gla_level_masks(chunk):
    i = np.arange(chunk)[:, None]
    j = np.arange(chunk)[None, :]
    levels = [i == j]
    s = 1
    while s < chunk:
        levels.append(((i ^ j) < 2 * s) & ((i & s) != 0) & ((j & s) == 0))
        s *= 2
    return jnp.asarray(np.stack(levels).astype(np.float32))


def _gla_rec_call(q, k, v, la, s0, casts=()):
    batch, n, _ = q.shape
    chunk = min(GLA_CHUNK, n)
    assert n % chunk == 0 and chunk & (chunk - 1) == 0
    has_state = s0 is not None
    bb = GLA_BATCH_BLOCK if batch % GLA_BATCH_BLOCK == 0 else 1
    seq_spec = lambda width: pl.BlockSpec((bb, chunk, width), lambda b, c: (b, c, 0))
    state_spec = pl.BlockSpec((bb, GLA_HEADS, GLA_DK, GLA_DV), lambda b, c: (b, 0, 0, 0))
    masks = _gla_level_masks(chunk)
    in_specs = [seq_spec(GLA_QK), seq_spec(GLA_QK), seq_spec(GLA_V), seq_spec(GLA_QK),
                pl.BlockSpec(masks.shape, lambda b, c: (0, 0, 0), pipeline_mode=pl.Buffered(1))]
    args = [q, k, v, la, masks]
    if has_state:
        in_specs.append(state_spec)
        args.append(s0)
    n_chunks = n // chunk
    n_steps = (batch // bb) * n_chunks
    cast_specs = []
    for a in casts:
        rows = a.shape[-2]
        per_step = next(r for r in range(BF16_ROWS, rows + 1, BF16_ROWS)
                        if rows % r == 0 and rows // r <= n_steps)
        last = rows // per_step - 1
        lead = a.ndim - 2
        cast_specs.append(pl.BlockSpec(
            a.shape[:lead] + (per_step, a.shape[-1]),
            lambda b, c, lead=lead, last=last: (0,) * lead + (jnp.minimum(b * n_chunks + c, last), 0)))
    return pl.pallas_call(
        functools.partial(_gla_rec_kernel, chunk=chunk, has_state=has_state, n_cast=len(casts)),
        grid=(batch // bb, n_chunks),
        in_specs=in_specs + cast_specs,
        out_specs=[seq_spec(GLA_V), state_spec] + cast_specs,
        out_shape=[jax.ShapeDtypeStruct((batch, n, GLA_V), F32),
                   jax.ShapeDtypeStruct((batch, GLA_HEADS, GLA_DK, GLA_DV), F32)]
        + [jax.ShapeDtypeStruct(a.shape, BF16) for a in casts],
        compiler_params=_params(("arbitrary", "arbitrary")),
        name="gla_rec",
    )(*args, *casts)


def _attn_prompt_call(qs, k, v):
    batch, n, _ = k.shape
    assert n % (ATTN_QBLOCK * max(d for _, d in DIL_GROUPS)) == 0
    n_split = sum(1 for _, d in DIL_GROUPS if d % ATTN_SPLIT == 0)
    spec = pl.BlockSpec((1, n, LANES), lambda b, p: (b, 0, p))
    return pl.pallas_call(
        functools.partial(_attn_prompt_kernel, n=n),
        grid=(batch, KV_DIM // LANES),
        in_specs=[spec] * (N_GROUPS + 2),
        out_specs=[spec] * N_GROUPS,
        out_shape=[jax.ShapeDtypeStruct((batch, n, KV_DIM), F32)] * N_GROUPS,
        scratch_shapes=[pltpu.VMEM((n_split, n, LANES), F32), pltpu.VMEM((n, LANES), F32),
                        pltpu.VMEM((n, LANES), F32), pltpu.VMEM((n_split, n, LANES), F32),
                        pltpu.VMEM((N_GROUPS, n, LANES), F32)],
        compiler_params=_params(("parallel", "parallel")),
        name="attn_prompt",
    )(*qs, k, v)


def _attn_sample_call(qs, k_new, v_new, k_cache, v_cache):
    batch, n, _ = k_new.shape
    cache_len = k_cache.shape[2]
    new_spec = pl.BlockSpec((1, n, KV_DIM), lambda b: (b, 0, 0))
    cache_spec = pl.BlockSpec((1, KV_DIM, cache_len), lambda b: (b, 0, 0))
    return pl.pallas_call(
        functools.partial(_attn_sample_kernel, n=n, cache_len=cache_len),
        grid=(batch,),
        in_specs=[new_spec] * (N_GROUPS + 2) + [cache_spec] * 2,
        out_specs=[new_spec] * N_GROUPS,
        out_shape=[jax.ShapeDtypeStruct((batch, n, KV_DIM), F32)] * N_GROUPS,
        compiler_params=_params(("parallel",)),
        name="attn_sample",
    )(*qs, k_new, v_new, k_cache, v_cache)


def _rope_tables(pos0, n, reps):
    half = HEAD_DIM // 2
    inv = ROPE_THETA ** (-np.arange(half, dtype=np.float64) / half)
    ang = (pos0 + np.arange(n, dtype=np.float64))[:, None] * inv[None, :]
    cos = np.concatenate([np.cos(ang), np.cos(ang)], axis=-1)
    sin = np.concatenate([-np.sin(ang), np.sin(ang)], axis=-1)
    cos = np.tile(cos, (reps, N_KV_HEADS)).astype(np.float32)
    sin = np.tile(sin, (reps, N_KV_HEADS)).astype(np.float32)
    return jnp.asarray(cos), jnp.asarray(sin)


def _head_mean_matrix():
    idx = np.arange(KV_DIM) // HEAD_DIM
    return jnp.asarray((idx[:, None] == idx[None, :]).astype(np.float32) / HEAD_DIM, dtype=BF16)


def _run_group(x, pos0, s0, caches, w, ffn_bf16=None):
    batch, n, _ = x.shape
    rows = batch * n
    x = x.reshape(rows, D_MODEL)
    reps = 1 if n % ROW_TILE == 0 else batch
    cos, sin = _rope_tables(pos0, n, reps)
    avg = _head_mean_matrix()

    gain = w["gain"]
    members = lambda stack: [[(stack, (l, i)) for i in range(2)] for l in range(2)]
    first_in, first_out = w["ffn_first"] if ffn_bf16 is None else (members(s)[0][0] for s in ffn_bf16)
    seq = lambda a: a.reshape(batch, n, a.shape[-1])
    wide, qk, kv = (D_MODEL, F32), (GLA_QK, F32), (KV_DIM, F32)

    x1, q, k, v, la = _rowwise_call(
        _ffn_gla_in_kernel, "ffn_gla_in", rows, [x],
        [gain[0][0], first_in, first_out, gain[0][1], *w["gla_qkvg"], w["gla_g2"], w["gla_bg"]], [],
        [wide, qk, qk, (GLA_V, BF16), qk])
    o, s_fin, *converted = _gla_rec_call(seq(q), seq(k), seq(v), seq(la), s0,
                                         casts=w["ffn_f32"] if ffn_bf16 is None else ())
    ffn_bf16 = ffn_bf16 or converted
    ffn_in, ffn_out = (members(s) for s in ffn_bf16)
    (x3,) = _rowwise_call(
        _gla_out_ffn_kernel, "gla_out_ffn", rows, [o.reshape(rows, GLA_V), x1],
        [gain[0][1], w["gla_r"], w["gla_norm"], w["gla_out"], gain[0][2], ffn_in[0][1], ffn_out[0][1]], [],
        [wide])

    transposed = n % ROW_TILE == 0
    x4, k_new, v_new, *rest = _rowwise_call(
        _kv_ffn_q_kernel, "kv_ffn_q", rows, [x3],
        [w["kv_gain"], w["kv_w"], w["k_norm"], avg, gain[1][0], ffn_in[1][0], ffn_out[1][0], gain[1][1],
         w["attn_q"], w["q_norm"]], [cos, sin],
        [wide, kv, kv] + [kv] * N_GROUPS, seq_len=n, n_transposed=2 if transposed else 0)
    qs = rest[:N_GROUPS]

    qs = [seq(a) for a in qs]
    if caches is None:
        os_ = _attn_prompt_call(qs, seq(k_new), seq(v_new))
    else:
        os_ = _attn_sample_call(qs, seq(k_new), seq(v_new), *caches)
    (x,) = _rowwise_call(
        _attn_out_ffn_kernel, "attn_out_ffn", rows, [a.reshape(rows, KV_DIM) for a in os_] + [x4],
        [w["attn_out"], gain[1][2], ffn_in[1][1], ffn_out[1][1]], [],
        [wide], row_tile=LAST_ROW_TILE)

    if transposed:
        k_out, v_out = (jnp.transpose(a.reshape(batch, N_KV_HEADS, HEAD_DIM, n), (0, 3, 1, 2))
                        for a in rest[N_GROUPS:])
    else:
        k_out, v_out = (a.reshape(batch, n, N_KV_HEADS, HEAD_DIM) for a in (k_new, v_new))
    return (x.reshape(batch, n, D_MODEL), s_fin[None], k_out, v_out), ffn_bf16


def kernel(x_prompt, x_sample, state_gla, cache_k_win, cache_v_win, norm_gains, ffn_w_in, ffn_w_out,
           gla_w_in, gla_w_gate2, gla_b_gate, gla_out_norm, gla_w_out, kv_norm, kv_w, k_norm,
           attn_w_q, q_norm, attn_w_out):
    assert norm_gains.shape[0] == 2 and gla_w_in.shape[0] == 1 and attn_w_q.shape[0] == 1
    row = lambda a: a.reshape(1, -1)
    gw = gla_w_in[0]
    cuts = (0, GLA_QK, 2 * GLA_QK, 2 * GLA_QK + GLA_V, 2 * GLA_QK + GLA_V + GLA_RANK, gw.shape[1])
    wq, wk, wv, wg, wr = (gw[:, a:b].astype(BF16) for a, b in zip(cuts[:-1], cuts[1:]))
    pad = LANES - GLA_RANK
    w = {
        "gain": [[row(norm_gains[l, i]) for i in range(3)] for l in range(2)],
        "ffn_first": (ffn_w_in[0, 0].astype(BF16), ffn_w_out[0, 0].astype(BF16)),
        "ffn_f32": (ffn_w_in, ffn_w_out),
        "gla_qkvg": (wq, wk, wv, jnp.pad(wg, ((0, 0), (0, pad)))),
        "gla_r": wr,
        "gla_g2": jnp.pad(gla_w_gate2[0].astype(BF16), ((0, pad), (0, 0))),
        "gla_bg": row(gla_b_gate[0]),
        "gla_norm": row(gla_out_norm[0]),
        "gla_out": gla_w_out[0].astype(BF16),
        "kv_gain": row(kv_norm),
        "kv_w": kv_w.astype(BF16),
        "k_norm": row(jnp.tile(k_norm, N_KV_HEADS)),
        "attn_q": attn_w_q[0].astype(BF16),
        "q_norm": row(jnp.tile(q_norm[0], N_KV_HEADS)),
        "attn_out": attn_w_out[0].astype(BF16),
    }
    caches = tuple(jnp.transpose(c, (0, 2, 3, 1)).reshape(c.shape[0], KV_DIM, c.shape[1])
                   for c in (cache_k_win, cache_v_win))

    (y_p, s_p, k_p, v_p), ffn_bf16 = _run_group(x_prompt, 0, None, None, w)
    (y_s, s_s, k_s, v_s), _ = _run_group(x_sample, PAST_LEN, state_gla[0], caches, w, ffn_bf16)
    keep = min(MAX_WINDOW, x_prompt.shape[1])
    return (y_p, y_s, s_p, s_s, k_p[:, -keep:], v_p[:, -keep:], k_s, v_s)
```

```python
import functools

import numpy as np
import jax
import jax.numpy as jnp
from jax import lax
from jax.experimental import pallas as pl
from jax.experimental.pallas import tpu as pltpu

F32 = jnp.float32
BF16 = jnp.bfloat16

D_MODEL = 1024
FFN_DIM = 2688
EPS = 1e-6
PAST_LEN = 8192
GLA_HEADS = 4
GLA_DK = 128
GLA_DV = 256
GLA_RANK = 16
GLA_TAU = 16.0
GLA_QK = GLA_HEADS * GLA_DK
GLA_V = GLA_HEADS * GLA_DV
HEAD_DIM = 64
N_KV_HEADS = 4
KV_DIM = N_KV_HEADS * HEAD_DIM
DIL_GROUPS = ((128, 1), (512, 4), (2048, 16))
N_GROUPS = len(DIL_GROUPS)
MAX_WINDOW = max(w for w, _ in DIL_GROUPS)
ROPE_THETA = 10000.0

LANES = 128
SUBLANES = 8
BF16_ROWS = 16
GLA_CHUNK = 128
GLA_BATCH_BLOCK = 4
LOG2_E = 1.4426950408889634
ATTN_QBLOCK = 128
ROW_TILE = 512
LAST_ROW_TILE = 1024
MXU_TILE = 256
FFN_MAIN = (FFN_DIM // MXU_TILE) * MXU_TILE
FFN_REM = FFN_DIM - FFN_MAIN
FFN_CHUNK = 1280
ATTN_SPLIT = 4
VMEM_LIMIT = 52 * 1024 * 1024


def _dot(a, b):
    return jnp.dot(a, b, preferred_element_type=F32)


def _dot_nt(a, b):
    return lax.dot_general(a, b, (((1,), (1,)), ((), ())), preferred_element_type=F32)


def _dot_tn(a, b):
    return lax.dot_general(a, b, (((0,), (0,)), ((), ())), preferred_element_type=F32)


def _rms(x, g):
    ms = jnp.mean(x * x, axis=-1, keepdims=True)
    return x * lax.rsqrt(ms + EPS) * g


def _silu(x):
    return x * jax.nn.sigmoid(x)


def _group_mean_sq(x, avg):
    sq = x * x
    hi = sq.astype(BF16)
    lo = (sq - hi.astype(F32)).astype(BF16)
    return _dot(hi, avg) + _dot(lo, avg)


def _rope(x, cos, sin_signed):
    w = x.shape[-1]
    lane = lax.broadcasted_iota(jnp.int32, (1, w), 1)
    first_half = (lane % HEAD_DIM) < (HEAD_DIM // 2)
    rot = jnp.where(first_half, pltpu.roll(x, w - HEAD_DIM // 2, 1), pltpu.roll(x, HEAD_DIM // 2, 1))
    return x * cos + rot * sin_signed


def _ffn_half(x, g_ref, win_ref, wout_ref):
    h = _rms(x, g_ref[...]).astype(BF16)
    acc = None
    for c0 in range(0, FFN_MAIN, FFN_CHUNK):
        c1 = c0 + FFN_CHUNK
        gate = _dot(h, win_ref[:, c0:c1])
        up = _dot(h, win_ref[:, FFN_DIM + c0:FFN_DIM + c1])
        act = (_silu(gate) * up).astype(BF16)
        part = _dot(act, wout_ref[c0:c1, :])
        acc = part if acc is None else acc + part
    if FFN_REM:
        w_rem = jnp.concatenate([win_ref[:, FFN_MAIN:FFN_DIM], win_ref[:, FFN_DIM + FFN_MAIN:]], axis=1)
        gate_up = _dot(h, w_rem)
        act = (_silu(gate_up[:, :FFN_REM]) * gate_up[:, FFN_REM:]).astype(BF16)
        acc = acc + _dot(act, wout_ref[FFN_MAIN:, :])
    return x + 0.5 * acc


def _ffn_gla_in_kernel(x_ref, g_ffn_ref, win_ref, wout_ref, g_mix_ref, wq_ref, wk_ref, wv_ref, wg_ref,
                       wg2_ref, bg_ref, x1_ref, q_ref, k_ref, v_ref, la_ref):
    x1 = _ffn_half(x_ref[...], g_ffn_ref, win_ref, wout_ref)
    x1_ref[...] = x1
    h = _rms(x1, g_mix_ref[...]).astype(BF16)
    g_lr = _dot(h, wg_ref[...])
    z = _dot(g_lr.astype(BF16), wg2_ref[...]) + bg_ref[...]
    log_sig = jnp.minimum(z, 0.0) - jnp.log1p(jnp.exp(-jnp.abs(z)))
    la_ref[...] = log_sig * (1.0 / GLA_TAU)
    q_ref[...] = _dot(h, wq_ref[...]) * (GLA_DK ** -0.5)
    k_ref[...] = _dot(h, wk_ref[...])
    v_ref[...] = _dot(h, wv_ref[...]).astype(BF16)


def _gla_out_ffn_kernel(o_ref, x1_ref, g_mix_ref, wr_ref, gn_ref, wo_ref, g_ffn_ref, win_ref, wout_ref, x3_ref):
    x1 = x1_ref[...]
    r = _dot(_rms(x1, g_mix_ref[...]).astype(BF16), wr_ref[...])
    o = o_ref[...]
    on = jnp.concatenate([_rms(o[:, h * GLA_DV:(h + 1) * GLA_DV], gn_ref[...]) for h in range(GLA_HEADS)],
                         axis=1)
    x2 = x1 + _dot((on * _silu(r)).astype(BF16), wo_ref[...])
    x3_ref[...] = _ffn_half(x2, g_ffn_ref, win_ref, wout_ref)


def _kv_ffn_q_kernel(x3_ref, g_kv_ref, wkv_ref, kn_ref, avg_ref, g_ffn_ref, win_ref, wout_ref, g_mix_ref,
                     wq_ref, qn_ref, cos_ref, sin_ref, x4_ref, k_ref, v_ref, q0_ref, q1_ref, q2_ref,
                     *kv_t_refs):
    x3 = x3_ref[...]
    cos, sin, avg = cos_ref[...], sin_ref[...], avg_ref[...]
    kv = _dot(_rms(x3, g_kv_ref[...]).astype(BF16), wkv_ref[...])
    k = kv[:, :KV_DIM]
    kn = k * lax.rsqrt(_group_mean_sq(k, avg) + EPS) * kn_ref[...]
    k_rot = _rope(kn, cos, sin)
    k_ref[...] = k_rot
    v_ref[...] = kv[:, KV_DIM:]
    if kv_t_refs:
        kt_ref, vt_ref = kv_t_refs
        kt_ref[0] = k_rot.T
        vt_ref[0] = kv[:, KV_DIM:].T
    x4 = _ffn_half(x3, g_ffn_ref, win_ref, wout_ref)
    x4_ref[...] = x4
    h = _rms(x4, g_mix_ref[...]).astype(BF16)
    for g, q_ref in enumerate((q0_ref, q1_ref, q2_ref)):
        q = _dot(h, wq_ref[:, g * KV_DIM:(g + 1) * KV_DIM])
        qn = q * lax.rsqrt(_group_mean_sq(q, avg) + EPS) * qn_ref[...]
        q_ref[...] = _rope(qn, cos, sin) * (HEAD_DIM ** -0.5)


def _attn_out_ffn_kernel(o0_ref, o1_ref, o2_ref, x4_ref, wo_ref, g_ffn_ref, win_ref, wout_ref, y_ref):
    x5 = x4_ref[...]
    for g, o_ref in enumerate((o0_ref, o1_ref, o2_ref)):
        x5 = x5 + _dot(o_ref[...].astype(BF16), wo_ref[g * KV_DIM:(g + 1) * KV_DIM, :])
    y_ref[...] = _ffn_half(x5, g_ffn_ref, win_ref, wout_ref)


def _gla_rec_kernel(*refs, chunk, has_state):
    if has_state:
        q_ref, k_ref, v_ref, la_ref, m_ref, s0_ref, o_ref, st_ref = refs
    else:
        q_ref, k_ref, v_ref, la_ref, m_ref, o_ref, st_ref = refs
    c = pl.program_id(1)

    @pl.when(c == 0)
    def _():
        if has_state:
            st_ref[...] = s0_ref[...]
        else:
            st_ref[...] = jnp.zeros_like(st_ref)

    heads = [slice(h * GLA_DK, (h + 1) * GLA_DK) for h in range(GLA_HEADS)]
    values = [slice(h * GLA_DV, (h + 1) * GLA_DV) for h in range(GLA_HEADS)]
    row = lax.broadcasted_iota(jnp.int32, (chunk, 1), 0)

    def one_batch_row(b):
        q = q_ref[b]
        k = k_ref[b]
        attn = [None] * GLA_HEADS

        def add_level(qt, kt, level_index):
            for h, hs in enumerate(heads):
                scores = _dot_nt(qt[:, hs], kt[:, hs])
                if level_index is not None:
                    scores = scores * m_ref[level_index]
                attn[h] = scores if attn[h] is None else attn[h] + scores

        add_level(q.astype(BF16), k.astype(BF16), 0)

        lsum = la_ref[b] * LOG2_E
        rsum = jnp.zeros_like(lsum)
        s = 1
        level_index = 0
        while s < chunk:
            level_index += 1
            btot = lsum + rsum
            if s < SUBLANES:
                qt = (q * jnp.exp2(lsum)).astype(BF16)
                kt = (k * jnp.exp2(rsum)).astype(BF16)
                odd = (row & s) != 0
                lsum = lsum + jnp.where(odd, pltpu.roll(btot, s, 0), 0.0)
                rsum = rsum + jnp.where(odd, 0.0, pltpu.roll(btot, chunk - s, 0))
            else:
                zero = jnp.zeros((s, q.shape[1]), F32)
                q_slabs, k_slabs, l_slabs, r_slabs = [], [], [], []
                for lo in range(0, chunk, 2 * s):
                    ev, od = slice(lo, lo + s), slice(lo + s, lo + 2 * s)
                    q_slabs += [zero, q[od] * jnp.exp2(lsum[od])]
                    k_slabs += [k[ev] * jnp.exp2(rsum[ev]), zero]
                    l_slabs += [lsum[ev], lsum[od] + btot[ev]]
                    r_slabs += [rsum[ev] + btot[od], rsum[od]]
                qt = jnp.concatenate(q_slabs, axis=0).astype(BF16)
                kt = jnp.concatenate(k_slabs, axis=0).astype(BF16)
                lsum, rsum = jnp.concatenate(l_slabs, axis=0), jnp.concatenate(r_slabs, axis=0)
            add_level(qt, kt, None if s >= SUBLANES and 2 * s == chunk else level_index)
            s *= 2

        qt = (q * jnp.exp2(lsum)).astype(BF16)
        kd = (k * jnp.exp2(rsum)).astype(BF16)
        total = lsum[0:1, :] + rsum[0:1, :]
        for h, hs in enumerate(heads):
            st = st_ref[b, h]
            v = v_ref[b, :, values[h]]
            o_ref[b, :, values[h]] = _dot(attn[h].astype(BF16), v) + _dot(qt[:, hs], st.astype(BF16))
            tot_col = jnp.broadcast_to(total[:, hs], (GLA_DK, GLA_DK)).T[:, 0:1]
            st_ref[b, h] = jnp.exp2(tot_col) * st + _dot_tn(kd[:, hs], v)

    for b in range(q_ref.shape[0]):
        one_batch_row(b)


def _pair_masks():
    lane = lax.broadcasted_iota(jnp.int32, (1, LANES), 1)
    first = lane < HEAD_DIM
    return first, jnp.logical_not(first)


def _attn_prompt_kernel(q0_ref, q1_ref, q2_ref, k_ref, v_ref, bias_ref, o0_ref, o1_ref, o2_ref,
                        qd_ref, kd_ref, vd_ref, od_ref, lse_ref, *, n):
    q_refs = (q0_ref, q1_ref, q2_ref)
    o_refs = (o0_ref, o1_ref, o2_ref)
    head_masks = _pair_masks()
    sub = n // ATTN_SPLIT
    split_groups = [g for g, (_, dil) in enumerate(DIL_GROUPS) if dil % ATTN_SPLIT == 0]

    for cls in range(ATTN_SPLIT):
        src = pl.ds(cls, sub, stride=ATTN_SPLIT)
        dst = pl.ds(cls * sub, sub)
        kd_ref[dst, :] = k_ref[0, src, :]
        vd_ref[dst, :] = v_ref[0, src, :]
        for slot, g in enumerate(split_groups):
            qd_ref[slot, dst, :] = q_refs[g][0, src, :]

    def attend(qt, kt, vt, bias):
        qblk = qt.shape[0]
        qm = jnp.concatenate([jnp.where(hm, qt, 0.0) for hm in head_masks], axis=0).astype(BF16)
        s = _dot_nt(qm, kt) + bias
        m = jnp.max(s, axis=-1, keepdims=True)
        p = jnp.exp(s - m)
        den = jnp.sum(p, axis=-1, keepdims=True)
        pv = _dot(p.astype(BF16), vt)
        pick = lambda a: jnp.where(head_masks[0], a[:qblk], a[qblk:])
        den = pick(den)
        return pick(pv) / den, pick(m) + jnp.log(den)

    for g, (win, dil) in enumerate(DIL_GROUPS):
        length = n // dil
        qblk = min(ATTN_QBLOCK, length)
        nblk = length // qblk
        nkeys = min(2 * qblk, length)
        for idx in range(dil * nblk):
            blk = idx // dil
            u0 = max(blk - 1, 0) * qblk
            bias = bias_ref[2 * g + min(blk, 1), :, :nkeys]
            if g in split_groups:
                slot = split_groups.index(g)
                step = dil // ATTN_SPLIT
                base = (idx % ATTN_SPLIT) * sub + (idx // ATTN_SPLIT) % step
                q_rows = pl.ds(base + step * blk * qblk, qblk, stride=step)
                k_rows = pl.ds(base + step * u0, nkeys, stride=step)
                o, lse = attend(qd_ref[slot, q_rows, :], kd_ref[k_rows, :].astype(BF16),
                                vd_ref[k_rows, :].astype(BF16), bias)
                od_ref[slot, q_rows, :] = o
            else:
                assert dil == 1
                q_rows = pl.ds(blk * qblk, qblk)
                k_rows = pl.ds(u0, nkeys)
                o, lse = attend(q_refs[g][0, q_rows, :], k_ref[0, k_rows, :].astype(BF16),
                                v_ref[0, k_rows, :].astype(BF16), bias)
                o_refs[g][0, q_rows, :] = o
            lse_ref[g, q_rows, :] = lse

    rows_per_step = 256
    steps_per_class = sub // rows_per_step

    def reweight(t, carry):
        cls = t // steps_per_class
        start = (t % steps_per_class) * rows_per_step
        nat = pl.ds(cls + ATTN_SPLIT * start, rows_per_step, stride=ATTN_SPLIT)
        grp = pl.ds(pl.multiple_of(cls * sub + start, rows_per_step), rows_per_step)
        lse = [lse_ref[g, grp if g in split_groups else nat, :] for g in range(N_GROUPS)]
        top = jnp.maximum(jnp.maximum(lse[0], lse[1]), lse[2])
        e = [jnp.exp(l - top) for l in lse]
        tot = e[0] + e[1] + e[2]
        for g in range(N_GROUPS):
            if g in split_groups:
                o = od_ref[split_groups.index(g), grp, :]
            else:
                o = o_refs[g][0, nat, :]
            o_refs[g][0, nat, :] = o * (e[g] / tot)
        return carry

    lax.fori_loop(0, ATTN_SPLIT * steps_per_class, reweight, 0)


def _attn_sample_kernel(q0_ref, q1_ref, q2_ref, kn_ref, vn_ref, kc_ref, vc_ref, o0_ref, o1_ref, o2_ref,
                        *, n, cache_len):
    q_refs = (q0_ref, q1_ref, q2_ref)
    o_refs = (o0_ref, o1_ref, o2_ref)
    per_head = N_GROUPS * n
    rows = N_KV_HEADS * per_head
    r = lax.broadcasted_iota(jnp.int32, (rows, 1), 0)
    grp = (r % per_head) // n
    tq = r % n
    dil = jnp.zeros_like(r)
    win = jnp.zeros_like(r)
    for g, (w_g, d_g) in enumerate(DIL_GROUPS):
        dil = jnp.where(grp == g, d_g, dil)
        win = jnp.where(grp == g, w_g, win)
    d_cache = (cache_len + tq) - lax.broadcasted_iota(jnp.int32, (1, cache_len), 1)
    valid_c = ((d_cache & (dil - 1)) == 0) & (d_cache <= win)
    d_new = tq - lax.broadcasted_iota(jnp.int32, (1, n), 1)
    valid_n = (d_new >= 0) & ((d_new & (dil - 1)) == 0) & (d_new <= win)
    own_lanes = (lax.broadcasted_iota(jnp.int32, (1, KV_DIM), 1) // HEAD_DIM) == (r // per_head)

    q = jnp.concatenate([q_ref[0] for q_ref in q_refs] * N_KV_HEADS, axis=0)
    q = jnp.where(own_lanes, q, 0.0).astype(BF16)
    sc = jnp.where(valid_c, _dot(q, kc_ref[0].astype(BF16)), -jnp.inf)
    sn = jnp.where(valid_n, _dot_nt(q, kn_ref[0].astype(BF16)), -jnp.inf)
    m = jnp.maximum(jnp.max(sc, axis=-1, keepdims=True), jnp.max(sn, axis=-1, keepdims=True))
    pc = jnp.exp(sc - m)
    pn = jnp.exp(sn - m)
    den = jnp.sum(pc, axis=-1, keepdims=True) + jnp.sum(pn, axis=-1, keepdims=True)
    out = (_dot_nt(pc.astype(BF16), vc_ref[0].astype(BF16)) + _dot(pn.astype(BF16), vn_ref[0].astype(BF16))) / den
    lse = m + jnp.log(den)
    acc = [None] * N_GROUPS
    for h in range(N_KV_HEADS):
        blocks = [slice(h * per_head + g * n, h * per_head + (g + 1) * n) for g in range(N_GROUPS)]
        lses = [lse[b] for b in blocks]
        top = jnp.maximum(jnp.maximum(lses[0], lses[1]), lses[2])
        e = [jnp.exp(l - top) for l in lses]
        tot = e[0] + e[1] + e[2]
        for g, b in enumerate(blocks):
            term = jnp.where(own_lanes[b], out[b] * (e[g] / tot), 0.0)
            acc[g] = term if acc[g] is None else acc[g] + term
    for g in range(N_GROUPS):
        o_refs[g][0] = acc[g]


def _params(semantics):
    return pltpu.CompilerParams(dimension_semantics=semantics, vmem_limit_bytes=VMEM_LIMIT)


def _row_spec(tm, width):
    return pl.BlockSpec((tm, width), lambda i: (i, 0))


def _const_spec(shape):
    return pl.BlockSpec(shape, lambda i: (0,) * len(shape), pipeline_mode=pl.Buffered(1))


def _with_riders(kernel_fn, n_in, n_out, n_riders):
    def kernel(*refs):
        ins, srcs = refs[:n_in], refs[n_in:n_in + n_riders]
        outs, dsts = refs[n_in + n_riders:n_in + n_riders + n_out], refs[n_in + n_riders + n_out:]
        kernel_fn(*ins, *outs)
        for src_ref, dst_ref in zip(srcs, dsts):
            dst_ref[...] = src_ref[...].astype(BF16)
    return kernel


def _rowwise_call(kernel_fn, name, rows, row_inputs, consts, tables, outs, seq_len=None, n_transposed=0,
                  row_tile=ROW_TILE, riders=()):
    tm = row_tile if rows % row_tile == 0 else rows
    if tables:
        tm = min(tm, tables[0].shape[0])
    assert rows % tm == 0 and all(t.shape[0] % tm == 0 for t in tables)
    in_specs = [_row_spec(tm, a.shape[1]) for a in row_inputs]
    in_specs += [_const_spec(c.shape) for c in consts]
    for t in tables:
        period = t.shape[0] // tm
        in_specs.append(pl.BlockSpec((tm, t.shape[1]), lambda i, period=period: (i % period, 0)))
    out_specs = [_row_spec(tm, width) for width, _ in outs]
    out_shape = [jax.ShapeDtypeStruct((rows, width), dtype) for width, dtype in outs]
    if n_transposed:
        assert seq_len % tm == 0
        per_seq = seq_len // tm
        out_specs += [pl.BlockSpec((1, KV_DIM, tm), lambda i: (i // per_seq, 0, i % per_seq))] * n_transposed
        out_shape += [jax.ShapeDtypeStruct((rows // seq_len, KV_DIM, seq_len), F32)] * n_transposed
    operands = [*row_inputs, *consts, *tables]
    if riders:
        kernel_fn = _with_riders(kernel_fn, len(operands), len(out_specs), len(riders))
    n_steps = rows // tm
    for stack, index in riders:
        r_rows, width = stack.shape[-2:]
        per_step = next(r for r in range(BF16_ROWS, r_rows + 1, BF16_ROWS)
                        if r_rows % r == 0 and r_rows // r <= n_steps)
        last = r_rows // per_step - 1
        in_specs.append(pl.BlockSpec((None,) * len(index) + (per_step, width),
                                     lambda i, index=index, last=last: (*index, jnp.minimum(i, last), 0)))
        out_specs.append(pl.BlockSpec((per_step, width), lambda i, last=last: (jnp.minimum(i, last), 0)))
        out_shape.append(jax.ShapeDtypeStruct((r_rows, width), BF16))
        operands.append(stack)
    return pl.pallas_call(
        kernel_fn,
        grid=(n_steps,),
        in_specs=in_specs,
        out_specs=out_specs,
        out_shape=out_shape,
        compiler_params=_params(("arbitrary",) if riders else ("parallel",)),
        name=name,
    )(*operands)


def _gla_level_masks(chunk):
    i = np.arange(chunk)[:, None]
    j = np.arange(chunk)[None, :]
    levels = [i == j]
    s = 1
    while s < chunk:
        levels.append(((i ^ j) < 2 * s) & ((i & s) != 0) & ((j & s) == 0))
        s *= 2
    return jnp.asarray(np.stack(levels).astype(np.float32))


def _gla_rec_call(q, k, v, la, s0):
    batch, n, _ = q.shape
    chunk = min(GLA_CHUNK, n)
    assert n % chunk == 0 and chunk & (chunk - 1) == 0
    has_state = s0 is not None
    bb = GLA_BATCH_BLOCK if batch % GLA_BATCH_BLOCK == 0 else 1
    seq_spec = lambda width: pl.BlockSpec((bb, chunk, width), lambda b, c: (b, c, 0))
    state_spec = pl.BlockSpec((bb, GLA_HEADS, GLA_DK, GLA_DV), lambda b, c: (b, 0, 0, 0))
    masks = _gla_level_masks(chunk)
    in_specs = [seq_spec(GLA_QK), seq_spec(GLA_QK), seq_spec(GLA_V), seq_spec(GLA_QK),
                pl.BlockSpec(masks.shape, lambda b, c: (0, 0, 0), pipeline_mode=pl.Buffered(1))]
    args = [q, k, v, la, masks]
    if has_state:
        in_specs.append(state_spec)
        args.append(s0)
    return pl.pallas_call(
        functools.partial(_gla_rec_kernel, chunk=chunk, has_state=has_state),
        grid=(batch // bb, n // chunk),
        in_specs=in_specs,
        out_specs=[seq_spec(GLA_V), state_spec],
        out_shape=[jax.ShapeDtypeStruct((batch, n, GLA_V), F32),
                   jax.ShapeDtypeStruct((batch, GLA_HEADS, GLA_DK, GLA_DV), F32)],
        compiler_params=_params(("parallel", "arbitrary")),
        name="gla_rec",
    )(*args)


def _attn_window_bias(n):
    out = []
    for win, dil in DIL_GROUPS:
        qblk = min(ATTN_QBLOCK, n // dil)
        r = np.arange(2 * qblk)[:, None] % qblk
        c = np.arange(2 * ATTN_QBLOCK)[None, :]
        for shift in (0, qblk):
            delta = shift + r - c
            out.append(np.where((delta >= 0) & (delta <= win // dil), 0.0, -np.inf))
    return jnp.asarray(np.stack(out).astype(np.float32))


def _attn_prompt_call(qs, k, v):
    batch, n, _ = k.shape
    assert n % (ATTN_QBLOCK * max(d for _, d in DIL_GROUPS)) == 0
    n_split = sum(1 for _, d in DIL_GROUPS if d % ATTN_SPLIT == 0)
    spec = pl.BlockSpec((1, n, LANES), lambda b, p: (b, 0, p))
    bias = _attn_window_bias(n)
    bias_spec = pl.BlockSpec(bias.shape, lambda b, p: (0, 0, 0), pipeline_mode=pl.Buffered(1))
    return pl.pallas_call(
        functools.partial(_attn_prompt_kernel, n=n),
        grid=(batch, KV_DIM // LANES),
        in_specs=[spec] * (N_GROUPS + 2) + [bias_spec],
        out_specs=[spec] * N_GROUPS,
        out_shape=[jax.ShapeDtypeStruct((batch, n, KV_DIM), F32)] * N_GROUPS,
        scratch_shapes=[pltpu.VMEM((n_split, n, LANES), F32), pltpu.VMEM((n, LANES), F32),
                        pltpu.VMEM((n, LANES), F32), pltpu.VMEM((n_split, n, LANES), F32),
                        pltpu.VMEM((N_GROUPS, n, LANES), F32)],
        compiler_params=_params(("parallel", "parallel")),
        name="attn_prompt",
    )(*qs, k, v, bias)


def _attn_sample_call(qs, k_new, v_new, k_cache, v_cache):
    batch, n, _ = k_new.shape
    cache_len = k_cache.shape[2]
    new_spec = pl.BlockSpec((1, n, KV_DIM), lambda b: (b, 0, 0))
    cache_spec = pl.BlockSpec((1, KV_DIM, cache_len), lambda b: (b, 0, 0))
    return pl.pallas_call(
        functools.partial(_attn_sample_kernel, n=n, cache_len=cache_len),
        grid=(batch,),
        in_specs=[new_spec] * (N_GROUPS + 2) + [cache_spec] * 2,
        out_specs=[new_spec] * N_GROUPS,
        out_shape=[jax.ShapeDtypeStruct((batch, n, KV_DIM), F32)] * N_GROUPS,
        compiler_params=_params(("parallel",)),
        name="attn_sample",
    )(*qs, k_new, v_new, k_cache, v_cache)


def _rope_tables(pos0, n, reps):
    half = HEAD_DIM // 2
    inv = ROPE_THETA ** (-np.arange(half, dtype=np.float64) / half)
    ang = (pos0 + np.arange(n, dtype=np.float64))[:, None] * inv[None, :]
    cos = np.concatenate([np.cos(ang), np.cos(ang)], axis=-1)
    sin = np.concatenate([-np.sin(ang), np.sin(ang)], axis=-1)
    cos = np.tile(cos, (reps, N_KV_HEADS)).astype(np.float32)
    sin = np.tile(sin, (reps, N_KV_HEADS)).astype(np.float32)
    return jnp.asarray(cos), jnp.asarray(sin)


def _head_mean_matrix():
    idx = np.arange(KV_DIM) // HEAD_DIM
    return jnp.asarray((idx[:, None] == idx[None, :]).astype(np.float32) / HEAD_DIM, dtype=BF16)


def _run_group(x, pos0, s0, caches, w, ffn_bf16=None):
    batch, n, _ = x.shape
    rows = batch * n
    x = x.reshape(rows, D_MODEL)
    reps = 1 if n % ROW_TILE == 0 else batch
    cos, sin = _rope_tables(pos0, n, reps)
    avg = _head_mean_matrix()

    gain = w["gain"]
    later = [(l, i) for l in range(2) for i in range(2)][1:]
    riders = [(stack, m) for stack in w["ffn_f32"] for m in later] if ffn_bf16 is None else []
    seq = lambda a: a.reshape(batch, n, a.shape[-1])
    wide, qk, kv = (D_MODEL, F32), (GLA_QK, F32), (KV_DIM, F32)

    x1, q, k, v, la, *converted = _rowwise_call(
        _ffn_gla_in_kernel, "ffn_gla_in", rows, [x],
        [gain[0][0], *(ffn_bf16 or w["ffn_first"])[0, 0], gain[0][1], *w["gla_qkvg"], w["gla_g2"], w["gla_bg"]],
        [], [wide, qk, qk, (GLA_V, BF16), qk], riders=riders)
    if ffn_bf16 is None:
        ffn_bf16 = dict(w["ffn_first"])
        ffn_bf16.update({m: (converted[j], converted[len(later) + j]) for j, m in enumerate(later)})
    ffn_in = [[ffn_bf16[l, i][0] for i in range(2)] for l in range(2)]
    ffn_out = [[ffn_bf16[l, i][1] for i in range(2)] for l in range(2)]
    o, s_fin = _gla_rec_call(seq(q), seq(k), seq(v), seq(la), s0)
    (x3,) = _rowwise_call(
        _gla_out_ffn_kernel, "gla_out_ffn", rows, [o.reshape(rows, GLA_V), x1],
        [gain[0][1], w["gla_r"], w["gla_norm"], w["gla_out"], gain[0][2], ffn_in[0][1], ffn_out[0][1]], [],
        [wide])

    transposed = n % ROW_TILE == 0
    x4, k_new, v_new, *rest = _rowwise_call(
        _kv_ffn_q_kernel, "kv_ffn_q", rows, [x3],
        [w["kv_gain"], w["kv_w"], w["k_norm"], avg, gain[1][0], ffn_in[1][0], ffn_out[1][0], gain[1][1],
         w["attn_q"], w["q_norm"]], [cos, sin],
        [wide, kv, kv] + [kv] * N_GROUPS, seq_len=n, n_transposed=2 if transposed else 0)
    qs = rest[:N_GROUPS]

    qs = [seq(a) for a in qs]
    if caches is None:
        os_ = _attn_prompt_call(qs, seq(k_new), seq(v_new))
    else:
        os_ = _attn_sample_call(qs, seq(k_new), seq(v_new), *caches)
    (x,) = _rowwise_call(
        _attn_out_ffn_kernel, "attn_out_ffn", rows, [a.reshape(rows, KV_DIM) for a in os_] + [x4],
        [w["attn_out"], gain[1][2], ffn_in[1][1], ffn_out[1][1]], [],
        [wide], row_tile=LAST_ROW_TILE)

    if transposed:
        k_out, v_out = (jnp.transpose(a.reshape(batch, N_KV_HEADS, HEAD_DIM, n), (0, 3, 1, 2))
                        for a in rest[N_GROUPS:])
    else:
        k_out, v_out = (a.reshape(batch, n, N_KV_HEADS, HEAD_DIM) for a in (k_new, v_new))
    return (x.reshape(batch, n, D_MODEL), s_fin[None], k_out, v_out), ffn_bf16


def kernel(x_prompt, x_sample, state_gla, cache_k_win, cache_v_win, norm_gains, ffn_w_in, ffn_w_out,
           gla_w_in, gla_w_gate2, gla_b_gate, gla_out_norm, gla_w_out, kv_norm, kv_w, k_norm,
           attn_w_q, q_norm, attn_w_out):
    assert norm_gains.shape[0] == 2 and gla_w_in.shape[0] == 1 and attn_w_q.shape[0] == 1
    row = lambda a: a.reshape(1, -1)
    gw = gla_w_in[0]
    cuts = (0, GLA_QK, 2 * GLA_QK, 2 * GLA_QK + GLA_V, 2 * GLA_QK + GLA_V + GLA_RANK, gw.shape[1])
    wq, wk, wv, wg, wr = (gw[:, a:b].astype(BF16) for a, b in zip(cuts[:-1], cuts[1:]))
    pad = LANES - GLA_RANK
    w = {
        "gain": [[row(norm_gains[l, i]) for i in range(3)] for l in range(2)],
        "ffn_first": {(0, 0): (ffn_w_in[0, 0].astype(BF16), ffn_w_out[0, 0].astype(BF16))},
        "ffn_f32": (ffn_w_in, ffn_w_out),
        "gla_qkvg": (wq, wk, wv, jnp.pad(wg, ((0, 0), (0, pad)))),
        "gla_r": wr,
        "gla_g2": jnp.pad(gla_w_gate2[0].astype(BF16), ((0, pad), (0, 0))),
        "gla_bg": row(gla_b_gate[0]),
        "gla_norm": row(gla_out_norm[0]),
        "gla_out": gla_w_out[0].astype(BF16),
        "kv_gain": row(kv_norm),
        "kv_w": kv_w.astype(BF16),
        "k_norm": row(jnp.tile(k_norm, N_KV_HEADS)),
        "attn_q": attn_w_q[0].astype(BF16),
        "q_norm": row(jnp.tile(q_norm[0], N_KV_HEADS)),
        "attn_out": attn_w_out[0].astype(BF16),
    }
    caches = tuple(jnp.transpose(c, (0, 2, 3, 1)).reshape(c.shape[0], KV_DIM, c.shape[1])
                   for c in (cache_k_win, cache_v_win))

    (y_p, s_p, k_p, v_p), ffn_bf16 = _run_group(x_prompt, 0, None, None, w)
    (y_s, s_s, k_s, v_s), _ = _run_group(x_sample, PAST_LEN, state_gla[0], caches, w, ffn_bf16)
    keep = min(MAX_WINDOW, x_prompt.shape[1])
    return (y_p, y_s, s_p, s_s, k_p[:, -keep:], v_p[:, -keep:], k_s, v_s)
```

```python
import functools

import numpy as np
import jax
import jax.numpy as jnp
from jax import lax
from jax.experimental import pallas as pl
from jax.experimental.pallas import tpu as pltpu

F32 = jnp.float32
BF16 = jnp.bfloat16

D_MODEL = 1024
FFN_DIM = 2688
EPS = 1e-6
PAST_LEN = 8192
GLA_HEADS = 4
GLA_DK = 128
GLA_DV = 256
GLA_RANK = 16
GLA_TAU = 16.0
GLA_QK = GLA_HEADS * GLA_DK
GLA_V = GLA_HEADS * GLA_DV
HEAD_DIM = 64
N_KV_HEADS = 4
KV_DIM = N_KV_HEADS * HEAD_DIM
DIL_GROUPS = ((128, 1), (512, 4), (2048, 16))
N_GROUPS = len(DIL_GROUPS)
MAX_WINDOW = max(w for w, _ in DIL_GROUPS)
ROPE_THETA = 10000.0

LANES = 128
SUBLANES = 8
BF16_ROWS = 16
GLA_CHUNK = 128
GLA_BATCH_BLOCK = 4
LOG2_E = 1.4426950408889634
ATTN_QBLOCK = 128
ROW_TILE = 512
LAST_ROW_TILE = 1024
MXU_TILE = 256
FFN_MAIN = (FFN_DIM // MXU_TILE) * MXU_TILE
FFN_REM = FFN_DIM - FFN_MAIN
FFN_SPLITS = (1280,)
ATTN_SPLIT = 4
VMEM_LIMIT = 52 * 1024 * 1024


def _dot(a, b):
    return jnp.dot(a, b, preferred_element_type=F32)


def _dot_nt(a, b):
    return lax.dot_general(a, b, (((1,), (1,)), ((), ())), preferred_element_type=F32)


def _dot_tn(a, b):
    return lax.dot_general(a, b, (((0,), (0,)), ((), ())), preferred_element_type=F32)


def _rms(x, g):
    ms = jnp.mean(x * x, axis=-1, keepdims=True)
    return x * lax.rsqrt(ms + EPS) * g


def _silu(x):
    return x * jax.nn.sigmoid(x)


def _group_mean_sq(x, avg):
    sq = x * x
    hi = sq.astype(BF16)
    lo = (sq - hi.astype(F32)).astype(BF16)
    return _dot(hi, avg) + _dot(lo, avg)


def _rope(x, cos, sin_signed):
    w = x.shape[-1]
    lane = lax.broadcasted_iota(jnp.int32, (1, w), 1)
    first_half = (lane % HEAD_DIM) < (HEAD_DIM // 2)
    rot = jnp.where(first_half, pltpu.roll(x, w - HEAD_DIM // 2, 1), pltpu.roll(x, HEAD_DIM // 2, 1))
    return x * cos + rot * sin_signed


def _ffn_half(x, g_ref, win_ref, wout_ref):
    h = _rms(x, g_ref[...]).astype(BF16)
    acc = None
    for c0, c1 in zip((0,) + FFN_SPLITS, FFN_SPLITS + (FFN_MAIN,)):
        gate = _dot(h, win_ref[:, c0:c1])
        up = _dot(h, win_ref[:, FFN_DIM + c0:FFN_DIM + c1])
        act = (_silu(gate) * up).astype(BF16)
        part = _dot(act, wout_ref[c0:c1, :])
        acc = part if acc is None else acc + part
    if FFN_REM:
        w_rem = jnp.concatenate([win_ref[:, FFN_MAIN:FFN_DIM], win_ref[:, FFN_DIM + FFN_MAIN:]], axis=1)
        gate_up = _dot(h, w_rem)
        act = (_silu(gate_up[:, :FFN_REM]) * gate_up[:, FFN_REM:]).astype(BF16)
        acc = acc + _dot(act, wout_ref[FFN_MAIN:, :])
    return x + 0.5 * acc


def _ffn_gla_in_kernel(x_ref, g_ffn_ref, win_ref, wout_ref, g_mix_ref, wmix_ref, wg2_ref, bg_ref,
                       x1_ref, q_ref, k_ref, v_ref, la_ref):
    x1 = _ffn_half(x_ref[...], g_ffn_ref, win_ref, wout_ref)
    x1_ref[...] = x1
    h = _rms(x1, g_mix_ref[...]).astype(BF16)
    qk_end = 2 * GLA_QK
    g_lr = _dot(h, wmix_ref[:, qk_end + GLA_V:])
    z = _dot(g_lr.astype(BF16), wg2_ref[...]) + bg_ref[...]
    log_sig = jnp.minimum(z, 0.0) - jnp.log1p(jnp.exp(-jnp.abs(z)))
    la_ref[...] = log_sig * (1.0 / GLA_TAU)
    q_ref[...] = _dot(h, wmix_ref[:, :GLA_QK]) * (GLA_DK ** -0.5)
    k_ref[...] = _dot(h, wmix_ref[:, GLA_QK:qk_end])
    v_ref[...] = _dot(h, wmix_ref[:, qk_end:qk_end + GLA_V]).astype(BF16)


def _gla_out_ffn_kernel(o_ref, x1_ref, g_mix_ref, wr_ref, gn_ref, wo_ref, g_ffn_ref, win_ref, wout_ref, x3_ref):
    x1 = x1_ref[...]
    r = _dot(_rms(x1, g_mix_ref[...]).astype(BF16), wr_ref[...])
    o = o_ref[...]
    on = jnp.concatenate([_rms(o[:, h * GLA_DV:(h + 1) * GLA_DV], gn_ref[...]) for h in range(GLA_HEADS)],
                         axis=1)
    x2 = x1 + _dot((on * _silu(r)).astype(BF16), wo_ref[...])
    x3_ref[...] = _ffn_half(x2, g_ffn_ref, win_ref, wout_ref)


def _kv_ffn_q_kernel(x3_ref, g_kv_ref, wkv_ref, kn_ref, avg_ref, g_ffn_ref, win_ref, wout_ref, g_mix_ref,
                     wq_ref, qn_ref, cos_ref, sin_ref, x4_ref, k_ref, v_ref, q0_ref, q1_ref, q2_ref,
                     *kv_t_refs):
    x3 = x3_ref[...]
    cos, sin, avg = cos_ref[...], sin_ref[...], avg_ref[...]
    kv = _dot(_rms(x3, g_kv_ref[...]).astype(BF16), wkv_ref[...])
    k = kv[:, :KV_DIM]
    kn = k * lax.rsqrt(_group_mean_sq(k, avg) + EPS) * kn_ref[...]
    k_rot = _rope(kn, cos, sin)
    k_ref[...] = k_rot
    v_ref[...] = kv[:, KV_DIM:]
    if kv_t_refs:
        kt_ref, vt_ref = kv_t_refs
        kt_ref[0] = k_rot.T
        vt_ref[0] = kv[:, KV_DIM:].T
    x4 = _ffn_half(x3, g_ffn_ref, win_ref, wout_ref)
    x4_ref[...] = x4
    h = _rms(x4, g_mix_ref[...]).astype(BF16)
    for g, q_ref in enumerate((q0_ref, q1_ref, q2_ref)):
        q = _dot(h, wq_ref[:, g * KV_DIM:(g + 1) * KV_DIM])
        qn = q * lax.rsqrt(_group_mean_sq(q, avg) + EPS) * qn_ref[...]
        q_ref[...] = _rope(qn, cos, sin) * (HEAD_DIM ** -0.5)


def _attn_out_ffn_kernel(o0_ref, o1_ref, o2_ref, x4_ref, wo_ref, g_ffn_ref, win_ref, wout_ref, y_ref):
    x5 = x4_ref[...]
    for g, o_ref in enumerate((o0_ref, o1_ref, o2_ref)):
        x5 = x5 + _dot(o_ref[...].astype(BF16), wo_ref[g * KV_DIM:(g + 1) * KV_DIM, :])
    y_ref[...] = _ffn_half(x5, g_ffn_ref, win_ref, wout_ref)


def _gla_rec_kernel(*refs, chunk, has_state):
    if has_state:
        q_ref, k_ref, v_ref, la_ref, m_ref, s0_ref, o_ref, st_ref = refs
    else:
        q_ref, k_ref, v_ref, la_ref, m_ref, o_ref, st_ref = refs
    c = pl.program_id(1)

    @pl.when(c == 0)
    def _():
        if has_state:
            st_ref[...] = s0_ref[...]
        else:
            st_ref[...] = jnp.zeros_like(st_ref)

    heads = [slice(h * GLA_DK, (h + 1) * GLA_DK) for h in range(GLA_HEADS)]
    values = [slice(h * GLA_DV, (h + 1) * GLA_DV) for h in range(GLA_HEADS)]
    row = lax.broadcasted_iota(jnp.int32, (chunk, 1), 0)

    def one_batch_row(b):
        q = q_ref[b]
        k = k_ref[b]
        attn = [None] * GLA_HEADS

        def add_level(qt, kt, level_index):
            for h, hs in enumerate(heads):
                scores = _dot_nt(qt[:, hs], kt[:, hs])
                if level_index is not None:
                    scores = scores * m_ref[level_index]
                attn[h] = scores if attn[h] is None else attn[h] + scores

        add_level(q.astype(BF16), k.astype(BF16), 0)

        lsum = la_ref[b] * LOG2_E
        rsum = jnp.zeros_like(lsum)
        s = 1
        level_index = 0
        while s < chunk:
            level_index += 1
            btot = lsum + rsum
            if s < SUBLANES:
                qt = (q * jnp.exp2(lsum)).astype(BF16)
                kt = (k * jnp.exp2(rsum)).astype(BF16)
                odd = (row & s) != 0
                lsum = lsum + jnp.where(odd, pltpu.roll(btot, s, 0), 0.0)
                rsum = rsum + jnp.where(odd, 0.0, pltpu.roll(btot, chunk - s, 0))
            else:
                zero = jnp.zeros((s, q.shape[1]), F32)
                q_slabs, k_slabs, l_slabs, r_slabs = [], [], [], []
                for lo in range(0, chunk, 2 * s):
                    ev, od = slice(lo, lo + s), slice(lo + s, lo + 2 * s)
                    q_slabs += [zero, q[od] * jnp.exp2(lsum[od])]
                    k_slabs += [k[ev] * jnp.exp2(rsum[ev]), zero]
                    l_slabs += [lsum[ev], lsum[od] + btot[ev]]
                    r_slabs += [rsum[ev] + btot[od], rsum[od]]
                qt = jnp.concatenate(q_slabs, axis=0).astype(BF16)
                kt = jnp.concatenate(k_slabs, axis=0).astype(BF16)
                lsum, rsum = jnp.concatenate(l_slabs, axis=0), jnp.concatenate(r_slabs, axis=0)
            add_level(qt, kt, None if s >= SUBLANES and 2 * s == chunk else level_index)
            s *= 2

        qt = (q * jnp.exp2(lsum)).astype(BF16)
        kd = (k * jnp.exp2(rsum)).astype(BF16)
        total = lsum[0:1, :] + rsum[0:1, :]
        for h, hs in enumerate(heads):
            st = st_ref[b, h]
            v = v_ref[b, :, values[h]]
            o_ref[b, :, values[h]] = _dot(attn[h].astype(BF16), v) + _dot(qt[:, hs], st.astype(BF16))
            tot_col = jnp.broadcast_to(total[:, hs], (GLA_DK, GLA_DK)).T[:, 0:1]
            st_ref[b, h] = jnp.exp2(tot_col) * st + _dot_tn(kd[:, hs], v)

    for b in range(q_ref.shape[0]):
        one_batch_row(b)


def _pair_masks():
    lane = lax.broadcasted_iota(jnp.int32, (1, LANES), 1)
    first = lane < HEAD_DIM
    return first, jnp.logical_not(first)


def _attn_prompt_kernel(q0_ref, q1_ref, q2_ref, k_ref, v_ref, bias_ref, o0_ref, o1_ref, o2_ref,
                        qd_ref, kd_ref, vd_ref, od_ref, lse_ref, *, n):
    q_refs = (q0_ref, q1_ref, q2_ref)
    o_refs = (o0_ref, o1_ref, o2_ref)
    head_masks = _pair_masks()
    sub = n // ATTN_SPLIT
    split_groups = [g for g, (_, dil) in enumerate(DIL_GROUPS) if dil % ATTN_SPLIT == 0]

    for cls in range(ATTN_SPLIT):
        src = pl.ds(cls, sub, stride=ATTN_SPLIT)
        dst = pl.ds(cls * sub, sub)
        kd_ref[dst, :] = k_ref[0, src, :]
        vd_ref[dst, :] = v_ref[0, src, :]
        for slot, g in enumerate(split_groups):
            qd_ref[slot, dst, :] = q_refs[g][0, src, :]

    def attend(qt, kt, vt, bias):
        qblk = qt.shape[0]
        qm = jnp.concatenate([jnp.where(hm, qt, 0.0) for hm in head_masks], axis=0).astype(BF16)
        s = _dot_nt(qm, kt) + bias
        m = jnp.max(s, axis=-1, keepdims=True)
        p = jnp.exp(s - m)
        den = jnp.sum(p, axis=-1, keepdims=True)
        pv = _dot(p.astype(BF16), vt)
        pick = lambda a: jnp.where(head_masks[0], a[:qblk], a[qblk:])
        den = pick(den)
        return pick(pv) / den, pick(m) + jnp.log(den)

    for g, (win, dil) in enumerate(DIL_GROUPS):
        length = n // dil
        qblk = min(ATTN_QBLOCK, length)
        nblk = length // qblk
        nkeys = min(2 * qblk, length)
        for idx in range(dil * nblk):
            blk = idx // dil
            u0 = max(blk - 1, 0) * qblk
            bias = bias_ref[2 * g + min(blk, 1), :, :nkeys]
            if g in split_groups:
                slot = split_groups.index(g)
                step = dil // ATTN_SPLIT
                base = (idx % ATTN_SPLIT) * sub + (idx // ATTN_SPLIT) % step
                q_rows = pl.ds(base + step * blk * qblk, qblk, stride=step)
                k_rows = pl.ds(base + step * u0, nkeys, stride=step)
                o, lse = attend(qd_ref[slot, q_rows, :], kd_ref[k_rows, :].astype(BF16),
                                vd_ref[k_rows, :].astype(BF16), bias)
                od_ref[slot, q_rows, :] = o
            else:
                assert dil == 1
                q_rows = pl.ds(blk * qblk, qblk)
                k_rows = pl.ds(u0, nkeys)
                o, lse = attend(q_refs[g][0, q_rows, :], k_ref[0, k_rows, :].astype(BF16),
                                v_ref[0, k_rows, :].astype(BF16), bias)
                o_refs[g][0, q_rows, :] = o
            lse_ref[g, q_rows, :] = lse

    rows_per_step = 256
    steps_per_class = sub // rows_per_step

    def reweight(t, carry):
        cls = t // steps_per_class
        start = (t % steps_per_class) * rows_per_step
        nat = pl.ds(cls + ATTN_SPLIT * start, rows_per_step, stride=ATTN_SPLIT)
        grp = pl.ds(pl.multiple_of(cls * sub + start, rows_per_step), rows_per_step)
        lse = [lse_ref[g, grp if g in split_groups else nat, :] for g in range(N_GROUPS)]
        top = jnp.maximum(jnp.maximum(lse[0], lse[1]), lse[2])
        e = [jnp.exp(l - top) for l in lse]
        tot = e[0] + e[1] + e[2]
        for g in range(N_GROUPS):
            if g in split_groups:
                o = od_ref[split_groups.index(g), grp, :]
            else:
                o = o_refs[g][0, nat, :]
            o_refs[g][0, nat, :] = o * (e[g] / tot)
        return carry

    lax.fori_loop(0, ATTN_SPLIT * steps_per_class, reweight, 0)


def _attn_sample_kernel(q0_ref, q1_ref, q2_ref, kn_ref, vn_ref, kc_ref, vc_ref, o0_ref, o1_ref, o2_ref,
                        *, n, cache_len):
    q_refs = (q0_ref, q1_ref, q2_ref)
    o_refs = (o0_ref, o1_ref, o2_ref)
    per_head = N_GROUPS * n
    rows = N_KV_HEADS * per_head
    r = lax.broadcasted_iota(jnp.int32, (rows, 1), 0)
    grp = (r % per_head) // n
    tq = r % n
    dil = jnp.zeros_like(r)
    win = jnp.zeros_like(r)
    for g, (w_g, d_g) in enumerate(DIL_GROUPS):
        dil = jnp.where(grp == g, d_g, dil)
        win = jnp.where(grp == g, w_g, win)
    d_cache = (cache_len + tq) - lax.broadcasted_iota(jnp.int32, (1, cache_len), 1)
    valid_c = ((d_cache & (dil - 1)) == 0) & (d_cache <= win)
    d_new = tq - lax.broadcasted_iota(jnp.int32, (1, n), 1)
    valid_n = (d_new >= 0) & ((d_new & (dil - 1)) == 0) & (d_new <= win)
    own_lanes = (lax.broadcasted_iota(jnp.int32, (1, KV_DIM), 1) // HEAD_DIM) == (r // per_head)

    q = jnp.concatenate([q_ref[0] for q_ref in q_refs] * N_KV_HEADS, axis=0)
    q = jnp.where(own_lanes, q, 0.0).astype(BF16)
    sc = jnp.where(valid_c, _dot(q, kc_ref[0].astype(BF16)), -jnp.inf)
    sn = jnp.where(valid_n, _dot_nt(q, kn_ref[0].astype(BF16)), -jnp.inf)
    m = jnp.maximum(jnp.max(sc, axis=-1, keepdims=True), jnp.max(sn, axis=-1, keepdims=True))
    pc = jnp.exp(sc - m)
    pn = jnp.exp(sn - m)
    den = jnp.sum(pc, axis=-1, keepdims=True) + jnp.sum(pn, axis=-1, keepdims=True)
    out = (_dot_nt(pc.astype(BF16), vc_ref[0].astype(BF16)) + _dot(pn.astype(BF16), vn_ref[0].astype(BF16))) / den
    lse = m + jnp.log(den)
    acc = [None] * N_GROUPS
    for h in range(N_KV_HEADS):
        blocks = [slice(h * per_head + g * n, h * per_head + (g + 1) * n) for g in range(N_GROUPS)]
        lses = [lse[b] for b in blocks]
        top = jnp.maximum(jnp.maximum(lses[0], lses[1]), lses[2])
        e = [jnp.exp(l - top) for l in lses]
        tot = e[0] + e[1] + e[2]
        for g, b in enumerate(blocks):
            term = jnp.where(own_lanes[b], out[b] * (e[g] / tot), 0.0)
            acc[g] = term if acc[g] is None else acc[g] + term
    for g in range(N_GROUPS):
        o_refs[g][0] = acc[g]


def _params(semantics):
    return pltpu.CompilerParams(dimension_semantics=semantics, vmem_limit_bytes=VMEM_LIMIT)


def _row_spec(tm, width):
    return pl.BlockSpec((tm, width), lambda i: (i, 0))


def _const_spec(shape):
    return pl.BlockSpec(shape, lambda i: (0,) * len(shape), pipeline_mode=pl.Buffered(1))


def _with_riders(kernel_fn, n_in, n_out, n_riders):
    def kernel(*refs):
        ins, srcs = refs[:n_in], refs[n_in:n_in + n_riders]
        outs, dsts = refs[n_in + n_riders:n_in + n_riders + n_out], refs[n_in + n_riders + n_out:]
        kernel_fn(*ins, *outs)
        for src_ref, dst_ref in zip(srcs, dsts):
            dst_ref[...] = src_ref[...].astype(BF16)
    return kernel


def _rowwise_call(kernel_fn, name, rows, row_inputs, consts, tables, outs, seq_len=None, n_transposed=0,
                  row_tile=ROW_TILE, riders=()):
    tm = row_tile if rows % row_tile == 0 else rows
    if tables:
        tm = min(tm, tables[0].shape[0])
    assert rows % tm == 0 and all(t.shape[0] % tm == 0 for t in tables)
    in_specs = [_row_spec(tm, a.shape[1]) for a in row_inputs]
    in_specs += [_const_spec(c.shape) for c in consts]
    for t in tables:
        period = t.shape[0] // tm
        in_specs.append(pl.BlockSpec((tm, t.shape[1]), lambda i, period=period: (i % period, 0)))
    out_specs = [_row_spec(tm, width) for width, _ in outs]
    out_shape = [jax.ShapeDtypeStruct((rows, width), dtype) for width, dtype in outs]
    if n_transposed:
        assert seq_len % tm == 0
        per_seq = seq_len // tm
        out_specs += [pl.BlockSpec((1, KV_DIM, tm), lambda i: (i // per_seq, 0, i % per_seq))] * n_transposed
        out_shape += [jax.ShapeDtypeStruct((rows // seq_len, KV_DIM, seq_len), F32)] * n_transposed
    operands = [*row_inputs, *consts, *tables]
    if riders:
        kernel_fn = _with_riders(kernel_fn, len(operands), len(out_specs), len(riders))
    n_steps = rows // tm
    for stack, index in riders:
        r_rows, width = stack.shape[-2:]
        per_step = next(r for r in range(BF16_ROWS, r_rows + 1, BF16_ROWS)
                        if r_rows % r == 0 and r_rows // r <= n_steps)
        last = r_rows // per_step - 1
        in_specs.append(pl.BlockSpec((None,) * len(index) + (per_step, width),
                                     lambda i, index=index, last=last: (*index, jnp.minimum(i, last), 0)))
        out_specs.append(pl.BlockSpec((per_step, width), lambda i, last=last: (jnp.minimum(i, last), 0)))
        out_shape.append(jax.ShapeDtypeStruct((r_rows, width), BF16))
        operands.append(stack)
    return pl.pallas_call(
        kernel_fn,
        grid=(n_steps,),
        in_specs=in_specs,
        out_specs=out_specs,
        out_shape=out_shape,
        compiler_params=_params(("arbitrary",) if riders else ("parallel",)),
        name=name,
    )(*operands)


def _gla_level_masks(chunk):
    i = np.arange(chunk)[:, None]
    j = np.arange(chunk)[None, :]
    levels = [i == j]
    s = 1
    while s < chunk:
        levels.append(((i ^ j) < 2 * s) & ((i & s) != 0) & ((j & s) == 0))
        s *= 2
    return jnp.asarray(np.stack(levels).astype(np.float32))


def _gla_rec_call(q, k, v, la, s0):
    batch, n, _ = q.shape
    chunk = min(GLA_CHUNK, n)
    assert n % chunk == 0 and chunk & (chunk - 1) == 0
    has_state = s0 is not None
    bb = GLA_BATCH_BLOCK if batch % GLA_BATCH_BLOCK == 0 else 1
    seq_spec = lambda width: pl.BlockSpec((bb, chunk, width), lambda b, c: (b, c, 0))
    state_spec = pl.BlockSpec((bb, GLA_HEADS, GLA_DK, GLA_DV), lambda b, c: (b, 0, 0, 0))
    masks = _gla_level_masks(chunk)
    in_specs = [seq_spec(GLA_QK), seq_spec(GLA_QK), seq_spec(GLA_V), seq_spec(GLA_QK),
                pl.BlockSpec(masks.shape, lambda b, c: (0, 0, 0), pipeline_mode=pl.Buffered(1))]
    args = [q, k, v, la, masks]
    if has_state:
        in_specs.append(state_spec)
        args.append(s0)
    return pl.pallas_call(
        functools.partial(_gla_rec_kernel, chunk=chunk, has_state=has_state),
        grid=(batch // bb, n // chunk),
        in_specs=in_specs,
        out_specs=[seq_spec(GLA_V), state_spec],
        out_shape=[jax.ShapeDtypeStruct((batch, n, GLA_V), F32),
                   jax.ShapeDtypeStruct((batch, GLA_HEADS, GLA_DK, GLA_DV), F32)],
        compiler_params=_params(("parallel", "arbitrary")),
        name="gla_rec",
    )(*args)


def _attn_window_bias(n):
    out = []
    for win, dil in DIL_GROUPS:
        qblk = min(ATTN_QBLOCK, n // dil)
        r = np.arange(2 * qblk)[:, None] % qblk
        c = np.arange(2 * ATTN_QBLOCK)[None, :]
        for shift in (0, qblk):
            delta = shift + r - c
            out.append(np.where((delta >= 0) & (delta <= win // dil), 0.0, -np.inf))
    return jnp.asarray(np.stack(out).astype(np.float32))


def _attn_prompt_call(qs, k, v):
    batch, n, _ = k.shape
    assert n % (ATTN_QBLOCK * max(d for _, d in DIL_GROUPS)) == 0
    n_split = sum(1 for _, d in DIL_GROUPS if d % ATTN_SPLIT == 0)
    spec = pl.BlockSpec((1, n, LANES), lambda b, p: (b, 0, p))
    bias = _attn_window_bias(n)
    bias_spec = pl.BlockSpec(bias.shape, lambda b, p: (0, 0, 0), pipeline_mode=pl.Buffered(1))
    return pl.pallas_call(
        functools.partial(_attn_prompt_kernel, n=n),
        grid=(batch, KV_DIM // LANES),
        in_specs=[spec] * (N_GROUPS + 2) + [bias_spec],
        out_specs=[spec] * N_GROUPS,
        out_shape=[jax.ShapeDtypeStruct((batch, n, KV_DIM), F32)] * N_GROUPS,
        scratch_shapes=[pltpu.VMEM((n_split, n, LANES), F32), pltpu.VMEM((n, LANES), F32),
                        pltpu.VMEM((n, LANES), F32), pltpu.VMEM((n_split, n, LANES), F32),
                        pltpu.VMEM((N_GROUPS, n, LANES), F32)],
        compiler_params=_params(("parallel", "parallel")),
        name="attn_prompt",
    )(*qs, k, v, bias)


def _attn_sample_call(qs, k_new, v_new, k_cache, v_cache):
    batch, n, _ = k_new.shape
    cache_len = k_cache.shape[2]
    new_spec = pl.BlockSpec((1, n, KV_DIM), lambda b: (b, 0, 0))
    cache_spec = pl.BlockSpec((1, KV_DIM, cache_len), lambda b: (b, 0, 0))
    return pl.pallas_call(
        functools.partial(_attn_sample_kernel, n=n, cache_len=cache_len),
        grid=(batch,),
        in_specs=[new_spec] * (N_GROUPS + 2) + [cache_spec] * 2,
        out_specs=[new_spec] * N_GROUPS,
        out_shape=[jax.ShapeDtypeStruct((batch, n, KV_DIM), F32)] * N_GROUPS,
        compiler_params=_params(("parallel",)),
        name="attn_sample",
    )(*qs, k_new, v_new, k_cache, v_cache)


def _rope_tables(pos0, n, reps):
    half = HEAD_DIM // 2
    inv = ROPE_THETA ** (-np.arange(half, dtype=np.float64) / half)
    ang = (pos0 + np.arange(n, dtype=np.float64))[:, None] * inv[None, :]
    cos = np.concatenate([np.cos(ang), np.cos(ang)], axis=-1)
    sin = np.concatenate([-np.sin(ang), np.sin(ang)], axis=-1)
    cos = np.tile(cos, (reps, N_KV_HEADS)).astype(np.float32)
    sin = np.tile(sin, (reps, N_KV_HEADS)).astype(np.float32)
    return jnp.asarray(cos), jnp.asarray(sin)


def _head_mean_matrix():
    idx = np.arange(KV_DIM) // HEAD_DIM
    return jnp.asarray((idx[:, None] == idx[None, :]).astype(np.float32) / HEAD_DIM, dtype=BF16)


def _run_group(x, pos0, s0, caches, w, ffn_bf16=None):
    batch, n, _ = x.shape
    rows = batch * n
    x = x.reshape(rows, D_MODEL)
    reps = 1 if n % ROW_TILE == 0 else batch
    cos, sin = _rope_tables(pos0, n, reps)
    avg = _head_mean_matrix()

    gain = w["gain"]
    later = [(l, i) for l in range(2) for i in range(2)][1:]
    riders = [(stack, m) for stack in w["ffn_f32"] for m in later] if ffn_bf16 is None else []
    seq = lambda a: a.reshape(batch, n, a.shape[-1])
    wide, qk, kv = (D_MODEL, F32), (GLA_QK, F32), (KV_DIM, F32)

    x1, q, k, v, la, *converted = _rowwise_call(
        _ffn_gla_in_kernel, "ffn_gla_in", rows, [x],
        [gain[0][0], *(ffn_bf16 or w["ffn_first"])[0, 0], gain[0][1], w["gla_mix"], w["gla_g2"], w["gla_bg"]],
        [], [wide, qk, qk, (GLA_V, BF16), qk], riders=riders)
    if ffn_bf16 is None:
        ffn_bf16 = dict(w["ffn_first"])
        ffn_bf16.update({m: (converted[j], converted[len(later) + j]) for j, m in enumerate(later)})
    ffn_in = [[ffn_bf16[l, i][0] for i in range(2)] for l in range(2)]
    ffn_out = [[ffn_bf16[l, i][1] for i in range(2)] for l in range(2)]
    o, s_fin = _gla_rec_call(seq(q), seq(k), seq(v), seq(la), s0)
    (x3,) = _rowwise_call(
        _gla_out_ffn_kernel, "gla_out_ffn", rows, [o.reshape(rows, GLA_V), x1],
        [gain[0][1], w["gla_r"], w["gla_norm"], w["gla_out"], gain[0][2], ffn_in[0][1], ffn_out[0][1]], [],
        [wide])

    transposed = n % ROW_TILE == 0
    x4, k_new, v_new, *rest = _rowwise_call(
        _kv_ffn_q_kernel, "kv_ffn_q", rows, [x3],
        [w["kv_gain"], w["kv_w"], w["k_norm"], avg, gain[1][0], ffn_in[1][0], ffn_out[1][0], gain[1][1],
         w["attn_q"], w["q_norm"]], [cos, sin],
        [wide, kv, kv] + [kv] * N_GROUPS, seq_len=n, n_transposed=2 if transposed else 0)
    qs = rest[:N_GROUPS]

    qs = [seq(a) for a in qs]
    if caches is None:
        os_ = _attn_prompt_call(qs, seq(k_new), seq(v_new))
    else:
        os_ = _attn_sample_call(qs, seq(k_new), seq(v_new), *caches)
    (x,) = _rowwise_call(
        _attn_out_ffn_kernel, "attn_out_ffn", rows, [a.reshape(rows, KV_DIM) for a in os_] + [x4],
        [w["attn_out"], gain[1][2], ffn_in[1][1], ffn_out[1][1]], [],
        [wide], row_tile=LAST_ROW_TILE)

    if transposed:
        k_out, v_out = (jnp.transpose(a.reshape(batch, N_KV_HEADS, HEAD_DIM, n), (0, 3, 1, 2))
                        for a in rest[N_GROUPS:])
    else:
        k_out, v_out = (a.reshape(batch, n, N_KV_HEADS, HEAD_DIM) for a in (k_new, v_new))
    return (x.reshape(batch, n, D_MODEL), s_fin[None], k_out, v_out), ffn_bf16


def kernel(x_prompt, x_sample, state_gla, cache_k_win, cache_v_win, norm_gains, ffn_w_in, ffn_w_out,
           gla_w_in, gla_w_gate2, gla_b_gate, gla_out_norm, gla_w_out, kv_norm, kv_w, k_norm,
           attn_w_q, q_norm, attn_w_out):
    assert norm_gains.shape[0] == 2 and gla_w_in.shape[0] == 1 and attn_w_q.shape[0] == 1
    row = lambda a: a.reshape(1, -1)
    gw = gla_w_in[0]
    gate0 = 2 * GLA_QK + GLA_V
    pad = LANES - GLA_RANK
    w = {
        "gain": [[row(norm_gains[l, i]) for i in range(3)] for l in range(2)],
        "ffn_first": {(0, 0): (ffn_w_in[0, 0].astype(BF16), ffn_w_out[0, 0].astype(BF16))},
        "ffn_f32": (ffn_w_in, ffn_w_out),
        "gla_mix": gw[:, :gate0 + LANES].astype(BF16),
        "gla_r": gw[:, gate0 + GLA_RANK:].astype(BF16),
        "gla_g2": jnp.pad(gla_w_gate2[0].astype(BF16), ((0, pad), (0, 0))),
        "gla_bg": row(gla_b_gate[0]),
        "gla_norm": row(gla_out_norm[0]),
        "gla_out": gla_w_out[0].astype(BF16),
        "kv_gain": row(kv_norm),
        "kv_w": kv_w.astype(BF16),
        "k_norm": row(jnp.tile(k_norm, N_KV_HEADS)),
        "attn_q": attn_w_q[0].astype(BF16),
        "q_norm": row(jnp.tile(q_norm[0], N_KV_HEADS)),
        "attn_out": attn_w_out[0].astype(BF16),
    }
    caches = tuple(jnp.transpose(c, (0, 2, 3, 1)).reshape(c.shape[0], KV_DIM, c.shape[1])
                   for c in (cache_k_win, cache_v_win))

    (y_p, s_p, k_p, v_p), ffn_bf16 = _run_group(x_prompt, 0, None, None, w)
    (y_s, s_s, k_s, v_s), _ = _run_group(x_sample, PAST_LEN, state_gla[0], caches, w, ffn_bf16)
    keep = min(MAX_WINDOW, x_prompt.shape[1])
    return (y_p, y_s, s_p, s_s, k_p[:, -keep:], v_p[:, -keep:], k_s, v_s)
```

```python
import functools

import numpy as np
import jax
import jax.numpy as jnp
from jax import lax
from jax.experimental import pallas as pl
from jax.experimental.pallas import tpu as pltpu

F32 = jnp.float32
BF16 = jnp.bfloat16

D_MODEL = 1024
FFN_DIM = 2688
EPS = 1e-6
PAST_LEN = 8192
GLA_HEADS = 4
GLA_DK = 128
GLA_DV = 256
GLA_RANK = 16
GLA_TAU = 16.0
GLA_QK = GLA_HEADS * GLA_DK
GLA_V = GLA_HEADS * GLA_DV
HEAD_DIM = 64
N_KV_HEADS = 4
KV_DIM = N_KV_HEADS * HEAD_DIM
DIL_GROUPS = ((128, 1), (512, 4), (2048, 16))
N_GROUPS = len(DIL_GROUPS)
MAX_WINDOW = max(w for w, _ in DIL_GROUPS)
ROPE_THETA = 10000.0

LANES = 128
SUBLANES = 8
BF16_ROWS = 16
GLA_CHUNK = 128
GLA_BATCH_BLOCK = 8
LOG2_E = 1.4426950408889634
ATTN_QBLOCK = 128
ROW_TILE = 512
LAST_ROW_TILE = 1024
MXU_TILE = 256
FFN_MAIN = (FFN_DIM // MXU_TILE) * MXU_TILE
FFN_REM = FFN_DIM - FFN_MAIN
FFN_SPLITS = (1280,)
ATTN_SPLIT = 4
REWEIGHT_ROWS = 256
CAST_STEPS = 8
VMEM_LIMIT = 52 * 1024 * 1024


def _dot(a, b):
    return jnp.dot(a, b, preferred_element_type=F32)


def _dot_nt(a, b):
    return lax.dot_general(a, b, (((1,), (1,)), ((), ())), preferred_element_type=F32)


def _dot_tn(a, b):
    return lax.dot_general(a, b, (((0,), (0,)), ((), ())), preferred_element_type=F32)


def _rms(x, g):
    ms = jnp.mean(x * x, axis=-1, keepdims=True)
    return x * lax.rsqrt(ms + EPS) * g


def _silu(x):
    return x * jax.nn.sigmoid(x)


def _group_mean_sq(x, avg):
    sq = x * x
    hi = sq.astype(BF16)
    lo = (sq - hi.astype(F32)).astype(BF16)
    return _dot(hi, avg) + _dot(lo, avg)


def _rope(x, cos, sin_signed):
    w = x.shape[-1]
    lane = lax.broadcasted_iota(jnp.int32, (1, w), 1)
    first_half = (lane % HEAD_DIM) < (HEAD_DIM // 2)
    rot = jnp.where(first_half, pltpu.roll(x, w - HEAD_DIM // 2, 1), pltpu.roll(x, HEAD_DIM // 2, 1))
    return x * cos + rot * sin_signed


def _ffn_half(x, g_ref, win_ref, wout_ref):
    h = _rms(x, g_ref[...]).astype(BF16)
    acc = None
    for c0, c1 in zip((0,) + FFN_SPLITS, FFN_SPLITS + (FFN_MAIN,)):
        gate = _dot(h, win_ref[:, c0:c1])
        up = _dot(h, win_ref[:, FFN_DIM + c0:FFN_DIM + c1])
        act = (_silu(gate) * up).astype(BF16)
        part = _dot(act, wout_ref[c0:c1, :])
        acc = part if acc is None else acc + part
    if FFN_REM:
        w_rem = jnp.concatenate([win_ref[:, FFN_MAIN:FFN_DIM], win_ref[:, FFN_DIM + FFN_MAIN:]], axis=1)
        gate_up = _dot(h, w_rem)
        act = (_silu(gate_up[:, :FFN_REM]) * gate_up[:, FFN_REM:]).astype(BF16)
        acc = acc + _dot(act, wout_ref[FFN_MAIN:, :])
    return x + 0.5 * acc


def _ffn_gla_in_kernel(x_ref, g_ffn_ref, win_ref, wout_ref, g_mix_ref, wmix_ref, wg2_ref, bg_ref,
                       x1_ref, q_ref, k_ref, v_ref, la_ref):
    x1 = _ffn_half(x_ref[...], g_ffn_ref, win_ref, wout_ref)
    x1_ref[...] = x1
    h = _rms(x1, g_mix_ref[...]).astype(BF16)
    qk_end = 2 * GLA_QK
    g_lr = _dot(h, wmix_ref[:, qk_end + GLA_V:])
    z = _dot(g_lr.astype(BF16), wg2_ref[...]) + bg_ref[...]
    log_sig = jnp.minimum(z, 0.0) - jnp.log1p(jnp.exp(-jnp.abs(z)))
    la_ref[...] = log_sig * (1.0 / GLA_TAU)
    q_ref[...] = _dot(h, wmix_ref[:, :GLA_QK]) * (GLA_DK ** -0.5)
    k_ref[...] = _dot(h, wmix_ref[:, GLA_QK:qk_end])
    v_ref[...] = _dot(h, wmix_ref[:, qk_end:qk_end + GLA_V]).astype(BF16)


def _gla_out_ffn_kernel(o_ref, x1_ref, g_mix_ref, wr_ref, gn_ref, wo_ref, g_ffn_ref, win_ref, wout_ref, x3_ref):
    x1 = x1_ref[...]
    r = _dot(_rms(x1, g_mix_ref[...]).astype(BF16), wr_ref[...])
    o = o_ref[...]
    on = jnp.concatenate([_rms(o[:, h * GLA_DV:(h + 1) * GLA_DV], gn_ref[...]) for h in range(GLA_HEADS)],
                         axis=1)
    x2 = x1 + _dot((on * _silu(r)).astype(BF16), wo_ref[...])
    x3_ref[...] = _ffn_half(x2, g_ffn_ref, win_ref, wout_ref)


def _kv_ffn_q_kernel(x3_ref, g_kv_ref, wkv_ref, kn_ref, avg_ref, g_ffn_ref, win_ref, wout_ref, g_mix_ref,
                     wq_ref, qn_ref, cos_ref, sin_ref, x4_ref, k_ref, v_ref, q0_ref, q1_ref, q2_ref,
                     *kv_t_refs):
    x3 = x3_ref[...]
    cos, sin, avg = cos_ref[...], sin_ref[...], avg_ref[...]
    kv = _dot(_rms(x3, g_kv_ref[...]).astype(BF16), wkv_ref[...])
    k = kv[:, :KV_DIM]
    kn = k * lax.rsqrt(_group_mean_sq(k, avg) + EPS) * kn_ref[...]
    k_rot = _rope(kn, cos, sin)
    k_ref[...] = k_rot
    v_ref[...] = kv[:, KV_DIM:]
    if kv_t_refs:
        kt_ref, vt_ref = kv_t_refs
        kt_ref[0] = k_rot.T
        vt_ref[0] = kv[:, KV_DIM:].T
    x4 = _ffn_half(x3, g_ffn_ref, win_ref, wout_ref)
    x4_ref[...] = x4
    h = _rms(x4, g_mix_ref[...]).astype(BF16)
    for g, q_ref in enumerate((q0_ref, q1_ref, q2_ref)):
        q = _dot(h, wq_ref[:, g * KV_DIM:(g + 1) * KV_DIM])
        qn = q * lax.rsqrt(_group_mean_sq(q, avg) + EPS) * qn_ref[...]
        q_ref[...] = _rope(qn, cos, sin) * (HEAD_DIM ** -0.5)


def _attn_out_ffn_kernel(o0_ref, o1_ref, o2_ref, x4_ref, wo_ref, g_ffn_ref, win_ref, wout_ref, y_ref):
    x5 = x4_ref[...]
    for g, o_ref in enumerate((o0_ref, o1_ref, o2_ref)):
        x5 = x5 + _dot(o_ref[...].astype(BF16), wo_ref[g * KV_DIM:(g + 1) * KV_DIM, :])
    y_ref[...] = _ffn_half(x5, g_ffn_ref, win_ref, wout_ref)


def _gla_rec_kernel(*refs, chunk, has_state):
    if has_state:
        q_ref, k_ref, v_ref, la_ref, m_ref, s0_ref, o_ref, st_ref = refs
    else:
        q_ref, k_ref, v_ref, la_ref, m_ref, o_ref, st_ref = refs
    c = pl.program_id(1)

    @pl.when(c == 0)
    def _():
        if has_state:
            st_ref[...] = s0_ref[...]
        else:
            st_ref[...] = jnp.zeros_like(st_ref)

    heads = [slice(h * GLA_DK, (h + 1) * GLA_DK) for h in range(GLA_HEADS)]
    values = [slice(h * GLA_DV, (h + 1) * GLA_DV) for h in range(GLA_HEADS)]
    row = lax.broadcasted_iota(jnp.int32, (chunk, 1), 0)

    def one_batch_row(b):
        q = q_ref[b]
        k = k_ref[b]
        attn = [None] * GLA_HEADS

        def add_level(qt, kt, level_index):
            for h, hs in enumerate(heads):
                scores = _dot_nt(qt[:, hs], kt[:, hs])
                if level_index is not None:
                    scores = scores * m_ref[level_index]
                attn[h] = scores if attn[h] is None else attn[h] + scores

        add_level(q.astype(BF16), k.astype(BF16), 0)

        lsum = la_ref[b] * LOG2_E
        rsum = jnp.zeros_like(lsum)
        s = 1
        level_index = 0
        while s < chunk:
            level_index += 1
            btot = lsum + rsum
            if s < SUBLANES:
                qt = (q * jnp.exp2(lsum)).astype(BF16)
                kt = (k * jnp.exp2(rsum)).astype(BF16)
                odd = (row & s) != 0
                lsum = lsum + jnp.where(odd, pltpu.roll(btot, s, 0), 0.0)
                rsum = rsum + jnp.where(odd, 0.0, pltpu.roll(btot, chunk - s, 0))
            else:
                zero = jnp.zeros((s, q.shape[1]), F32)
                q_slabs, k_slabs, l_slabs, r_slabs = [], [], [], []
                for lo in range(0, chunk, 2 * s):
                    ev, od = slice(lo, lo + s), slice(lo + s, lo + 2 * s)
                    q_slabs += [zero, q[od] * jnp.exp2(lsum[od])]
                    k_slabs += [k[ev] * jnp.exp2(rsum[ev]), zero]
                    l_slabs += [lsum[ev], lsum[od] + btot[ev]]
                    r_slabs += [rsum[ev] + btot[od], rsum[od]]
                qt = jnp.concatenate(q_slabs, axis=0).astype(BF16)
                kt = jnp.concatenate(k_slabs, axis=0).astype(BF16)
                lsum, rsum = jnp.concatenate(l_slabs, axis=0), jnp.concatenate(r_slabs, axis=0)
            add_level(qt, kt, None if s >= SUBLANES and 2 * s == chunk else level_index)
            s *= 2

        qt = (q * jnp.exp2(lsum)).astype(BF16)
        kd = (k * jnp.exp2(rsum)).astype(BF16)
        total = lsum[0:1, :] + rsum[0:1, :]
        for h, hs in enumerate(heads):
            st = st_ref[b, h]
            v = v_ref[b, :, values[h]]
            o_ref[b, :, values[h]] = _dot(attn[h].astype(BF16), v) + _dot(qt[:, hs], st.astype(BF16))
            tot_col = jnp.broadcast_to(total[:, hs], (GLA_DK, GLA_DK)).T[:, 0:1]
            st_ref[b, h] = jnp.exp2(tot_col) * st + _dot_tn(kd[:, hs], v)

    for b in range(q_ref.shape[0]):
        one_batch_row(b)


def _pair_masks():
    lane = lax.broadcasted_iota(jnp.int32, (1, LANES), 1)
    first = lane < HEAD_DIM
    return first, jnp.logical_not(first)


def _attn_prompt_kernel(q0_ref, q1_ref, q2_ref, k_ref, v_ref, bias_ref, o0_ref, o1_ref, o2_ref,
                        qd_ref, kd_ref, vd_ref, od_ref, lse_ref, *, n):
    q_refs = (q0_ref, q1_ref, q2_ref)
    o_refs = (o0_ref, o1_ref, o2_ref)
    head_masks = _pair_masks()
    sub = n // ATTN_SPLIT
    split_groups = [g for g, (_, dil) in enumerate(DIL_GROUPS) if dil % ATTN_SPLIT == 0]

    for cls in range(ATTN_SPLIT):
        src = pl.ds(cls, sub, stride=ATTN_SPLIT)
        dst = pl.ds(cls * sub, sub)
        kd_ref[dst, :] = k_ref[0, src, :]
        vd_ref[dst, :] = v_ref[0, src, :]
        for slot, g in enumerate(split_groups):
            qd_ref[slot, dst, :] = q_refs[g][0, src, :]

    def attend(qt, kt, vt, bias):
        qblk = qt.shape[0]
        qm = jnp.concatenate([jnp.where(hm, qt, 0.0) for hm in head_masks], axis=0).astype(BF16)
        s = _dot_nt(qm, kt) + bias
        m = jnp.max(s, axis=-1, keepdims=True)
        p = jnp.exp(s - m)
        den = jnp.sum(p, axis=-1, keepdims=True)
        pv = _dot(p.astype(BF16), vt)
        pick = lambda a: jnp.where(head_masks[0], a[:qblk], a[qblk:])
        den = pick(den)
        return pick(pv) / den, pick(m) + jnp.log(den)

    for g, (win, dil) in enumerate(DIL_GROUPS):
        length = n // dil
        qblk = min(ATTN_QBLOCK, length)
        nblk = length // qblk
        nkeys = min(2 * qblk, length)
        for idx in range(dil * nblk):
            blk = idx // dil
            u0 = max(blk - 1, 0) * qblk
            bias = bias_ref[2 * g + min(blk, 1), :, :nkeys]
            if g in split_groups:
                slot = split_groups.index(g)
                step = dil // ATTN_SPLIT
                base = (idx % ATTN_SPLIT) * sub + (idx // ATTN_SPLIT) % step
                q_rows = pl.ds(base + step * blk * qblk, qblk, stride=step)
                k_rows = pl.ds(base + step * u0, nkeys, stride=step)
                o, lse = attend(qd_ref[slot, q_rows, :], kd_ref[k_rows, :].astype(BF16),
                                vd_ref[k_rows, :].astype(BF16), bias)
                od_ref[slot, q_rows, :] = o
            else:
                assert dil == 1
                q_rows = pl.ds(blk * qblk, qblk)
                k_rows = pl.ds(u0, nkeys)
                o, lse = attend(q_refs[g][0, q_rows, :], k_ref[0, k_rows, :].astype(BF16),
                                v_ref[0, k_rows, :].astype(BF16), bias)
                o_refs[g][0, q_rows, :] = o
            lse_ref[g, q_rows, :] = lse

    rows_per_step = REWEIGHT_ROWS
    steps_per_class = sub // rows_per_step

    def reweight(t, carry):
        cls = t // steps_per_class
        start = (t % steps_per_class) * rows_per_step
        nat = pl.ds(cls + ATTN_SPLIT * start, rows_per_step, stride=ATTN_SPLIT)
        grp = pl.ds(pl.multiple_of(cls * sub + start, rows_per_step), rows_per_step)
        lse = [lse_ref[g, grp if g in split_groups else nat, :] for g in range(N_GROUPS)]
        top = jnp.maximum(jnp.maximum(lse[0], lse[1]), lse[2])
        e = [jnp.exp(l - top) for l in lse]
        tot = e[0] + e[1] + e[2]
        for g in range(N_GROUPS):
            if g in split_groups:
                o = od_ref[split_groups.index(g), grp, :]
            else:
                o = o_refs[g][0, nat, :]
            o_refs[g][0, nat, :] = o * (e[g] / tot)
        return carry

    lax.fori_loop(0, ATTN_SPLIT * steps_per_class, reweight, 0)


def _attn_sample_kernel(q0_ref, q1_ref, q2_ref, kn_ref, vn_ref, kc_ref, vc_ref, o0_ref, o1_ref, o2_ref,
                        *, n, cache_len):
    q_refs = (q0_ref, q1_ref, q2_ref)
    o_refs = (o0_ref, o1_ref, o2_ref)
    per_head = N_GROUPS * n
    rows = N_KV_HEADS * per_head
    r = lax.broadcasted_iota(jnp.int32, (rows, 1), 0)
    grp = (r % per_head) // n
    tq = r % n
    dil = jnp.zeros_like(r)
    win = jnp.zeros_like(r)
    for g, (w_g, d_g) in enumerate(DIL_GROUPS):
        dil = jnp.where(grp == g, d_g, dil)
        win = jnp.where(grp == g, w_g, win)
    d_cache = (cache_len + tq) - lax.broadcasted_iota(jnp.int32, (1, cache_len), 1)
    valid_c = ((d_cache & (dil - 1)) == 0) & (d_cache <= win)
    d_new = tq - lax.broadcasted_iota(jnp.int32, (1, n), 1)
    valid_n = (d_new >= 0) & ((d_new & (dil - 1)) == 0) & (d_new <= win)
    own_lanes = (lax.broadcasted_iota(jnp.int32, (1, KV_DIM), 1) // HEAD_DIM) == (r // per_head)

    q = jnp.concatenate([q_ref[0] for q_ref in q_refs] * N_KV_HEADS, axis=0)
    q = jnp.where(own_lanes, q, 0.0).astype(BF16)
    sc = jnp.where(valid_c, _dot(q, kc_ref[0].astype(BF16)), -jnp.inf)
    sn = jnp.where(valid_n, _dot_nt(q, kn_ref[0].astype(BF16)), -jnp.inf)
    m = jnp.maximum(jnp.max(sc, axis=-1, keepdims=True), jnp.max(sn, axis=-1, keepdims=True))
    pc = jnp.exp(sc - m)
    pn = jnp.exp(sn - m)
    den = jnp.sum(pc, axis=-1, keepdims=True) + jnp.sum(pn, axis=-1, keepdims=True)
    out = (_dot_nt(pc.astype(BF16), vc_ref[0].astype(BF16)) + _dot(pn.astype(BF16), vn_ref[0].astype(BF16))) / den
    lse = m + jnp.log(den)
    acc = [None] * N_GROUPS
    for h in range(N_KV_HEADS):
        blocks = [slice(h * per_head + g * n, h * per_head + (g + 1) * n) for g in range(N_GROUPS)]
        lses = [lse[b] for b in blocks]
        top = jnp.maximum(jnp.maximum(lses[0], lses[1]), lses[2])
        e = [jnp.exp(l - top) for l in lses]
        tot = e[0] + e[1] + e[2]
        for g, b in enumerate(blocks):
            term = jnp.where(own_lanes[b], out[b] * (e[g] / tot), 0.0)
            acc[g] = term if acc[g] is None else acc[g] + term
    for g in range(N_GROUPS):
        o_refs[g][0] = acc[g]


def _params(semantics):
    return pltpu.CompilerParams(dimension_semantics=semantics, vmem_limit_bytes=VMEM_LIMIT)


def _row_spec(tm, width):
    return pl.BlockSpec((tm, width), lambda i: (i, 0))


def _const_spec(shape):
    return pl.BlockSpec(shape, lambda i: (0,) * len(shape), pipeline_mode=pl.Buffered(1))


def _with_riders(kernel_fn, n_in, n_out, n_riders):
    def kernel(*refs):
        ins, srcs = refs[:n_in], refs[n_in:n_in + n_riders]
        outs, dsts = refs[n_in + n_riders:n_in + n_riders + n_out], refs[n_in + n_riders + n_out:]
        kernel_fn(*ins, *outs)
        for src_ref, dst_ref in zip(srcs, dsts):
            dst_ref[...] = src_ref[...].astype(BF16)
    return kernel


def _rowwise_call(kernel_fn, name, rows, row_inputs, consts, tables, outs, seq_len=None, n_transposed=0,
                  row_tile=ROW_TILE, riders=()):
    tm = row_tile if rows % row_tile == 0 else rows
    if tables:
        tm = min(tm, tables[0].shape[0])
    assert rows % tm == 0 and all(t.shape[0] % tm == 0 for t in tables)
    in_specs = [_row_spec(tm, a.shape[1]) for a in row_inputs]
    in_specs += [_const_spec(c.shape) for c in consts]
    for t in tables:
        period = t.shape[0] // tm
        in_specs.append(pl.BlockSpec((tm, t.shape[1]), lambda i, period=period: (i % period, 0)))
    out_specs = [_row_spec(tm, width) for width, _ in outs]
    out_shape = [jax.ShapeDtypeStruct((rows, width), dtype) for width, dtype in outs]
    if n_transposed:
        assert seq_len % tm == 0
        per_seq = seq_len // tm
        out_specs += [pl.BlockSpec((1, KV_DIM, tm), lambda i: (i // per_seq, 0, i % per_seq))] * n_transposed
        out_shape += [jax.ShapeDtypeStruct((rows // seq_len, KV_DIM, seq_len), F32)] * n_transposed
    operands = [*row_inputs, *consts, *tables]
    if riders:
        kernel_fn = _with_riders(kernel_fn, len(operands), len(out_specs), len(riders))
    n_steps = rows // tm
    for stack, index in riders:
        r_rows, width = stack.shape[-2:]
        per_step = next(r for r in range(BF16_ROWS, r_rows + 1, BF16_ROWS)
                        if r_rows % r == 0 and r_rows // r <= n_steps)
        last = r_rows // per_step - 1
        in_specs.append(pl.BlockSpec((None,) * len(index) + (per_step, width),
                                     lambda i, index=index, last=last: (*index, jnp.minimum(i, last), 0)))
        out_specs.append(pl.BlockSpec((per_step, width), lambda i, last=last: (jnp.minimum(i, last), 0)))
        out_shape.append(jax.ShapeDtypeStruct((r_rows, width), BF16))
        operands.append(stack)
    return pl.pallas_call(
        kernel_fn,
        grid=(n_steps,),
        in_specs=in_specs,
        out_specs=out_specs,
        out_shape=out_shape,
        compiler_params=_params(("arbitrary",) if riders else ("parallel",)),
        name=name,
    )(*operands)


def _cast_kernel(*refs, windows):
    srcs, dsts = refs[:len(windows)], iter(refs[len(windows):])
    for src_ref, cols in zip(srcs, windows):
        for c0, c1 in cols:
            next(dsts)[...] = src_ref[:, c0:c1].astype(BF16)


def _cast_weights_call(items):
    in_specs, out_specs, out_shape, operands = [], [], [], []
    for arr, index, cols in items:
        rows, width = arr.shape[-2:]
        assert rows % (CAST_STEPS * BF16_ROWS) == 0
        blk = rows // CAST_STEPS
        in_specs.append(pl.BlockSpec((None,) * len(index) + (blk, width),
                                     lambda i, index=index: (*index, i, 0)))
        operands.append(arr)
        for c0, c1 in cols:
            out_specs.append(pl.BlockSpec((blk, c1 - c0), lambda i: (i, 0)))
            out_shape.append(jax.ShapeDtypeStruct((rows, c1 - c0), BF16))
    return pl.pallas_call(
        functools.partial(_cast_kernel, windows=[cols for _, _, cols in items]),
        grid=(CAST_STEPS,),
        in_specs=in_specs,
        out_specs=out_specs,
        out_shape=out_shape,
        compiler_params=_params(("parallel",)),
        name="cast_weights",
    )(*operands)


def _gla_level_masks(chunk):
    i = np.arange(chunk)[:, None]
    j = np.arange(chunk)[None, :]
    levels = [i == j]
    s = 1
    while s < chunk:
        levels.append(((i ^ j) < 2 * s) & ((i & s) != 0) & ((j & s) == 0))
        s *= 2
    return jnp.asarray(np.stack(levels).astype(np.float32))


def _gla_rec_call(q, k, v, la, s0):
    batch, n, _ = q.shape
    chunk = min(GLA_CHUNK, n)
    assert n % chunk == 0 and chunk & (chunk - 1) == 0
    has_state = s0 is not None
    bb = GLA_BATCH_BLOCK if batch % GLA_BATCH_BLOCK == 0 else 1
    seq_spec = lambda width: pl.BlockSpec((bb, chunk, width), lambda b, c: (b, c, 0))
    state_spec = pl.BlockSpec((bb, GLA_HEADS, GLA_DK, GLA_DV), lambda b, c: (b, 0, 0, 0))
    masks = _gla_level_masks(chunk)
    in_specs = [seq_spec(GLA_QK), seq_spec(GLA_QK), seq_spec(GLA_V), seq_spec(GLA_QK),
                pl.BlockSpec(masks.shape, lambda b, c: (0, 0, 0), pipeline_mode=pl.Buffered(1))]
    args = [q, k, v, la, masks]
    if has_state:
        in_specs.append(state_spec)
        args.append(s0)
    return pl.pallas_call(
        functools.partial(_gla_rec_kernel, chunk=chunk, has_state=has_state),
        grid=(batch // bb, n // chunk),
        in_specs=in_specs,
        out_specs=[seq_spec(GLA_V), state_spec],
        out_shape=[jax.ShapeDtypeStruct((batch, n, GLA_V), F32),
                   jax.ShapeDtypeStruct((batch, GLA_HEADS, GLA_DK, GLA_DV), F32)],
        compiler_params=_params(("parallel", "arbitrary")),
        name="gla_rec",
    )(*args)


def _attn_window_bias(n):
    out = []
    for win, dil in DIL_GROUPS:
        qblk = min(ATTN_QBLOCK, n // dil)
        r = np.arange(2 * qblk)[:, None] % qblk
        c = np.arange(2 * ATTN_QBLOCK)[None, :]
        for shift in (0, qblk):
            delta = shift + r - c
            out.append(np.where((delta >= 0) & (delta <= win // dil), 0.0, -np.inf))
    return jnp.asarray(np.stack(out).astype(np.float32))


def _attn_prompt_call(qs, k, v):
    batch, n, _ = k.shape
    assert n % (ATTN_QBLOCK * max(d for _, d in DIL_GROUPS)) == 0
    n_split = sum(1 for _, d in DIL_GROUPS if d % ATTN_SPLIT == 0)
    spec = pl.BlockSpec((1, n, LANES), lambda b, p: (b, 0, p))
    bias = _attn_window_bias(n)
    bias_spec = pl.BlockSpec(bias.shape, lambda b, p: (0, 0, 0), pipeline_mode=pl.Buffered(1))
    return pl.pallas_call(
        functools.partial(_attn_prompt_kernel, n=n),
        grid=(batch, KV_DIM // LANES),
        in_specs=[spec] * (N_GROUPS + 2) + [bias_spec],
        out_specs=[spec] * N_GROUPS,
        out_shape=[jax.ShapeDtypeStruct((batch, n, KV_DIM), F32)] * N_GROUPS,
        scratch_shapes=[pltpu.VMEM((n_split, n, LANES), F32), pltpu.VMEM((n, LANES), F32),
                        pltpu.VMEM((n, LANES), F32), pltpu.VMEM((n_split, n, LANES), F32),
                        pltpu.VMEM((N_GROUPS, n, LANES), F32)],
        compiler_params=_params(("parallel", "parallel")),
        name="attn_prompt",
    )(*qs, k, v, bias)


def _attn_sample_call(qs, k_new, v_new, k_cache, v_cache):
    batch, n, _ = k_new.shape
    cache_len = k_cache.shape[2]
    new_spec = pl.BlockSpec((1, n, KV_DIM), lambda b: (b, 0, 0))
    cache_spec = pl.BlockSpec((1, KV_DIM, cache_len), lambda b: (b, 0, 0))
    return pl.pallas_call(
        functools.partial(_attn_sample_kernel, n=n, cache_len=cache_len),
        grid=(batch,),
        in_specs=[new_spec] * (N_GROUPS + 2) + [cache_spec] * 2,
        out_specs=[new_spec] * N_GROUPS,
        out_shape=[jax.ShapeDtypeStruct((batch, n, KV_DIM), F32)] * N_GROUPS,
        compiler_params=_params(("parallel",)),
        name="attn_sample",
    )(*qs, k_new, v_new, k_cache, v_cache)


def _rope_tables(pos0, n, reps):
    half = HEAD_DIM // 2
    inv = ROPE_THETA ** (-np.arange(half, dtype=np.float64) / half)
    ang = (pos0 + np.arange(n, dtype=np.float64))[:, None] * inv[None, :]
    cos = np.concatenate([np.cos(ang), np.cos(ang)], axis=-1)
    sin = np.concatenate([-np.sin(ang), np.sin(ang)], axis=-1)
    cos = np.tile(cos, (reps, N_KV_HEADS)).astype(np.float32)
    sin = np.tile(sin, (reps, N_KV_HEADS)).astype(np.float32)
    return jnp.asarray(cos), jnp.asarray(sin)


def _head_mean_matrix():
    idx = np.arange(KV_DIM) // HEAD_DIM
    return jnp.asarray((idx[:, None] == idx[None, :]).astype(np.float32) / HEAD_DIM, dtype=BF16)


def _run_group(x, pos0, s0, caches, w, ffn_bf16=None):
    batch, n, _ = x.shape
    rows = batch * n
    x = x.reshape(rows, D_MODEL)
    reps = 1 if n % ROW_TILE == 0 else batch
    cos, sin = _rope_tables(pos0, n, reps)
    avg = _head_mean_matrix()

    gain = w["gain"]
    later = [(l, i) for l in range(2) for i in range(2)][1:]
    riders = [(stack, m) for stack in w["ffn_f32"] for m in later] if ffn_bf16 is None else []
    seq = lambda a: a.reshape(batch, n, a.shape[-1])
    wide, qk, kv = (D_MODEL, F32), (GLA_QK, F32), (KV_DIM, F32)

    x1, q, k, v, la, *converted = _rowwise_call(
        _ffn_gla_in_kernel, "ffn_gla_in", rows, [x],
        [gain[0][0], *(ffn_bf16 or w["ffn_first"])[0, 0], gain[0][1], w["gla_mix"], w["gla_g2"], w["gla_bg"]],
        [], [wide, qk, qk, (GLA_V, BF16), qk], riders=riders)
    if ffn_bf16 is None:
        ffn_bf16 = dict(w["ffn_first"])
        ffn_bf16.update({m: (converted[j], converted[len(later) + j]) for j, m in enumerate(later)})
    ffn_in = [[ffn_bf16[l, i][0] for i in range(2)] for l in range(2)]
    ffn_out = [[ffn_bf16[l, i][1] for i in range(2)] for l in range(2)]
    o, s_fin = _gla_rec_call(seq(q), seq(k), seq(v), seq(la), s0)
    (x3,) = _rowwise_call(
        _gla_out_ffn_kernel, "gla_out_ffn", rows, [o.reshape(rows, GLA_V), x1],
        [gain[0][1], w["gla_r"], w["gla_norm"], w["gla_out"], gain[0][2], ffn_in[0][1], ffn_out[0][1]], [],
        [wide])

    transposed = n % ROW_TILE == 0
    x4, k_new, v_new, *rest = _rowwise_call(
        _kv_ffn_q_kernel, "kv_ffn_q", rows, [x3],
        [w["kv_gain"], w["kv_w"], w["k_norm"], avg, gain[1][0], ffn_in[1][0], ffn_out[1][0], gain[1][1],
         w["attn_q"], w["q_norm"]], [cos, sin],
        [wide, kv, kv] + [kv] * N_GROUPS, seq_len=n, n_transposed=2 if transposed else 0)
    qs = rest[:N_GROUPS]

    qs = [seq(a) for a in qs]
    if caches is None:
        os_ = _attn_prompt_call(qs, seq(k_new), seq(v_new))
    else:
        os_ = _attn_sample_call(qs, seq(k_new), seq(v_new), *caches)
    (x,) = _rowwise_call(
        _attn_out_ffn_kernel, "attn_out_ffn", rows, [a.reshape(rows, KV_DIM) for a in os_] + [x4],
        [w["attn_out"], gain[1][2], ffn_in[1][1], ffn_out[1][1]], [],
        [wide], row_tile=LAST_ROW_TILE)

    if transposed:
        k_out, v_out = (jnp.transpose(a.reshape(batch, N_KV_HEADS, HEAD_DIM, n), (0, 3, 1, 2))
                        for a in rest[N_GROUPS:])
    else:
        k_out, v_out = (a.reshape(batch, n, N_KV_HEADS, HEAD_DIM) for a in (k_new, v_new))
    return (x.reshape(batch, n, D_MODEL), s_fin[None], k_out, v_out), ffn_bf16


def kernel(x_prompt, x_sample, state_gla, cache_k_win, cache_v_win, norm_gains, ffn_w_in, ffn_w_out,
           gla_w_in, gla_w_gate2, gla_b_gate, gla_out_norm, gla_w_out, kv_norm, kv_w, k_norm,
           attn_w_q, q_norm, attn_w_out):
    assert norm_gains.shape[0] == 2 and gla_w_in.shape[0] == 1 and attn_w_q.shape[0] == 1
    row = lambda a: a.reshape(1, -1)
    whole = lambda a: [(0, a.shape[-1])]
    gate0 = 2 * GLA_QK + GLA_V
    pad = LANES - GLA_RANK
    first_in, first_out, gla_out, kv_w16, attn_q, attn_out = _cast_weights_call([
        (ffn_w_in, (0, 0), whole(ffn_w_in)), (ffn_w_out, (0, 0), whole(ffn_w_out)),
        (gla_w_out, (0,), whole(gla_w_out)), (kv_w, (), whole(kv_w)),
        (attn_w_q, (0,), whole(attn_w_q)), (attn_w_out, (0,), whole(attn_w_out))])
    gla_mix = gla_w_in[0, :, :gate0 + LANES].astype(BF16)
    gla_r = gla_w_in[0, :, gate0 + GLA_RANK:].astype(BF16)
    w = {
        "gain": [[row(norm_gains[l, i]) for i in range(3)] for l in range(2)],
        "ffn_first": {(0, 0): (first_in, first_out)},
        "ffn_f32": (ffn_w_in, ffn_w_out),
        "gla_mix": gla_mix,
        "gla_r": gla_r,
        "gla_g2": jnp.pad(gla_w_gate2[0].astype(BF16), ((0, pad), (0, 0))),
        "gla_bg": row(gla_b_gate[0]),
        "gla_norm": row(gla_out_norm[0]),
        "gla_out": gla_out,
        "kv_gain": row(kv_norm),
        "kv_w": kv_w16,
        "k_norm": row(jnp.tile(k_norm, N_KV_HEADS)),
        "attn_q": attn_q,
        "q_norm": row(jnp.tile(q_norm[0], N_KV_HEADS)),
        "attn_out": attn_out,
    }
    caches = tuple(jnp.transpose(c, (0, 2, 3, 1)).reshape(c.shape[0], KV_DIM, c.shape[1])
                   for c in (cache_k_win, cache_v_win))

    (y_p, s_p, k_p, v_p), ffn_bf16 = _run_group(x_prompt, 0, None, None, w)
    (y_s, s_s, k_s, v_s), _ = _run_group(x_sample, PAST_LEN, state_gla[0], caches, w, ffn_bf16)
    keep = min(MAX_WINDOW, x_prompt.shape[1])
    return (y_p, y_s, s_p, s_s, k_p[:, -keep:], v_p[:, -keep:], k_s, v_s)
```

```python
import functools

import numpy as np
import jax
import jax.numpy as jnp
from jax import lax
from jax.experimental import pallas as pl
from jax.experimental.pallas import tpu as pltpu

F32 = jnp.float32
BF16 = jnp.bfloat16

D_MODEL = 1024
FFN_DIM = 2688
EPS = 1e-6
PAST_LEN = 8192
GLA_HEADS = 4
GLA_DK = 128
GLA_DV = 256
GLA_RANK = 16
GLA_TAU = 16.0
GLA_QK = GLA_HEADS * GLA_DK
GLA_V = GLA_HEADS * GLA_DV
HEAD_DIM = 64
N_KV_HEADS = 4
KV_DIM = N_KV_HEADS * HEAD_DIM
DIL_GROUPS = ((128, 1), (512, 4), (2048, 16))
N_GROUPS = len(DIL_GROUPS)
MAX_WINDOW = max(w for w, _ in DIL_GROUPS)
ROPE_THETA = 10000.0

LANES = 128
SUBLANES = 8
BF16_ROWS = 16
GLA_CHUNK = 128
GLA_BATCH_BLOCK = 8
LOG2_E = 1.4426950408889634
ATTN_QBLOCK = 128
ROW_TILE = 512
LAST_ROW_TILE = 1024
MXU_TILE = 256
FFN_MAIN = (FFN_DIM // MXU_TILE) * MXU_TILE
FFN_REM = FFN_DIM - FFN_MAIN
FFN_SPLITS = (1280,)
ATTN_SPLIT = 4
TAIL_SLABS = 2
REWEIGHT_ROWS = 256
CAST_STEPS = 8
VMEM_LIMIT = 52 * 1024 * 1024


def _dot(a, b):
    return jnp.dot(a, b, preferred_element_type=F32)


def _dot_nt(a, b):
    return lax.dot_general(a, b, (((1,), (1,)), ((), ())), preferred_element_type=F32)


def _dot_tn(a, b):
    return lax.dot_general(a, b, (((0,), (0,)), ((), ())), preferred_element_type=F32)


def _rms(x, g):
    ms = jnp.mean(x * x, axis=-1, keepdims=True)
    return x * lax.rsqrt(ms + EPS) * g


def _silu(x):
    return x * jax.nn.sigmoid(x)


def _group_mean_sq(x, avg):
    sq = x * x
    hi = sq.astype(BF16)
    lo = (sq - hi.astype(F32)).astype(BF16)
    return _dot(hi, avg) + _dot(lo, avg)


def _rope(x, cos, sin_signed):
    w = x.shape[-1]
    lane = lax.broadcasted_iota(jnp.int32, (1, w), 1)
    first_half = (lane % HEAD_DIM) < (HEAD_DIM // 2)
    rot = jnp.where(first_half, pltpu.roll(x, w - HEAD_DIM // 2, 1), pltpu.roll(x, HEAD_DIM // 2, 1))
    return x * cos + rot * sin_signed


def _ffn_half(x, g_ref, win_ref, wout_ref):
    h = _rms(x, g_ref[...]).astype(BF16)
    acc = None
    for c0, c1 in zip((0,) + FFN_SPLITS, FFN_SPLITS + (FFN_MAIN,)):
        gate = _dot(h, win_ref[:, c0:c1])
        up = _dot(h, win_ref[:, FFN_DIM + c0:FFN_DIM + c1])
        act = (_silu(gate) * up).astype(BF16)
        part = _dot(act, wout_ref[c0:c1, :])
        acc = part if acc is None else acc + part
    if FFN_REM:
        w_rem = jnp.concatenate([win_ref[:, FFN_MAIN:FFN_DIM], win_ref[:, FFN_DIM + FFN_MAIN:]], axis=1)
        gate_up = _dot(h, w_rem)
        act = (_silu(gate_up[:, :FFN_REM]) * gate_up[:, FFN_REM:]).astype(BF16)
        acc = acc + _dot(act, wout_ref[FFN_MAIN:, :])
    return x + 0.5 * acc


def _ffn_half_slabs(x, g_ref, win_ref, wout_ref, n_slabs):
    h = _rms(x, g_ref[...]).astype(BF16)
    bounds = list(zip((0,) + FFN_SPLITS, FFN_SPLITS + (FFN_MAIN,)))
    acts = []
    for c0, c1 in bounds:
        gate = _dot(h, win_ref[:, c0:c1])
        up = _dot(h, win_ref[:, FFN_DIM + c0:FFN_DIM + c1])
        acts.append((_silu(gate) * up).astype(BF16))
    w_rem = jnp.concatenate([win_ref[:, FFN_MAIN:FFN_DIM], win_ref[:, FFN_DIM + FFN_MAIN:]], axis=1)
    gate_up = _dot(h, w_rem)
    act_rem = (_silu(gate_up[:, :FFN_REM]) * gate_up[:, FFN_REM:]).astype(BF16)
    acc = None
    for (c0, c1), act in zip(bounds[:-1], acts[:-1]):
        part = _dot(act, wout_ref[c0:c1, :])
        acc = part if acc is None else acc + part
    c0, c1 = bounds[-1]
    slab = x.shape[0] // n_slabs
    outs = []
    for s in range(n_slabs):
        rows = slice(s * slab, (s + 1) * slab)
        tail = _dot(acts[-1][rows], wout_ref[c0:c1, :]) + _dot(act_rem[rows], wout_ref[FFN_MAIN:, :])
        outs.append(x[rows] + 0.5 * (acc[rows] + tail))
    return outs


def _ffn_gla_in_kernel(x_ref, g_ffn_ref, win_ref, wout_ref, g_mix_ref, wmix_ref, wg2_ref, bg_ref,
                       x1_ref, q_ref, k_ref, v_ref, la_ref):
    slabs = _ffn_half_slabs(x_ref[...], g_ffn_ref, win_ref, wout_ref, TAIL_SLABS)
    slab = x_ref.shape[0] // TAIL_SLABS
    qk_end = 2 * GLA_QK
    for s, x1 in enumerate(slabs):
        rows = slice(s * slab, (s + 1) * slab)
        x1_ref[rows, :] = x1
        h = _rms(x1, g_mix_ref[...]).astype(BF16)
        g_lr = _dot(h, wmix_ref[:, qk_end + GLA_V:])
        z = _dot(g_lr.astype(BF16), wg2_ref[...]) + bg_ref[...]
        log_sig = jnp.minimum(z, 0.0) - jnp.log1p(jnp.exp(-jnp.abs(z)))
        la_ref[rows, :] = log_sig * (1.0 / GLA_TAU)
        q_ref[rows, :] = _dot(h, wmix_ref[:, :GLA_QK]) * (GLA_DK ** -0.5)
        k_ref[rows, :] = _dot(h, wmix_ref[:, GLA_QK:qk_end])
        v_ref[rows, :] = _dot(h, wmix_ref[:, qk_end:qk_end + GLA_V]).astype(BF16)


def _gla_out_ffn_kernel(o_ref, x1_ref, g_mix_ref, wr_ref, gn_ref, wo_ref, g_ffn_ref, win_ref, wout_ref, x3_ref):
    x1 = x1_ref[...]
    r = _dot(_rms(x1, g_mix_ref[...]).astype(BF16), wr_ref[...])
    o = o_ref[...]
    on = jnp.concatenate([_rms(o[:, h * GLA_DV:(h + 1) * GLA_DV], gn_ref[...]) for h in range(GLA_HEADS)],
                         axis=1)
    x2 = x1 + _dot((on * _silu(r)).astype(BF16), wo_ref[...])
    x3_ref[...] = _ffn_half(x2, g_ffn_ref, win_ref, wout_ref)


def _kv_ffn_q_kernel(x3_ref, g_kv_ref, wkv_ref, kn_ref, avg_ref, g_ffn_ref, win_ref, wout_ref, g_mix_ref,
                     wq_ref, qn_ref, cos_ref, sin_ref, x4_ref, k_ref, v_ref, q0_ref, q1_ref, q2_ref,
                     *kv_t_refs):
    x3 = x3_ref[...]
    cos, sin, avg = cos_ref[...], sin_ref[...], avg_ref[...]
    kv = _dot(_rms(x3, g_kv_ref[...]).astype(BF16), wkv_ref[...])
    k = kv[:, :KV_DIM]
    kn = k * lax.rsqrt(_group_mean_sq(k, avg) + EPS) * kn_ref[...]
    k_rot = _rope(kn, cos, sin)
    k_ref[...] = k_rot
    v_ref[...] = kv[:, KV_DIM:]
    if kv_t_refs:
        kt_ref, vt_ref = kv_t_refs
        kt_ref[0] = k_rot.T
        vt_ref[0] = kv[:, KV_DIM:].T
    x4 = _ffn_half(x3, g_ffn_ref, win_ref, wout_ref)
    x4_ref[...] = x4
    h = _rms(x4, g_mix_ref[...]).astype(BF16)
    for g, q_ref in enumerate((q0_ref, q1_ref, q2_ref)):
        q = _dot(h, wq_ref[:, g * KV_DIM:(g + 1) * KV_DIM])
        qn = q * lax.rsqrt(_group_mean_sq(q, avg) + EPS) * qn_ref[...]
        q_ref[...] = _rope(qn, cos, sin) * (HEAD_DIM ** -0.5)


def _attn_out_ffn_kernel(o0_ref, o1_ref, o2_ref, x4_ref, wo_ref, g_ffn_ref, win_ref, wout_ref, y_ref):
    x5 = x4_ref[...]
    for g, o_ref in enumerate((o0_ref, o1_ref, o2_ref)):
        x5 = x5 + _dot(o_ref[...].astype(BF16), wo_ref[g * KV_DIM:(g + 1) * KV_DIM, :])
    y_ref[...] = _ffn_half(x5, g_ffn_ref, win_ref, wout_ref)


def _gla_rec_kernel(*refs, chunk, has_state):
    if has_state:
        q_ref, k_ref, v_ref, la_ref, m_ref, s0_ref, o_ref, st_ref = refs
    else:
        q_ref, k_ref, v_ref, la_ref, m_ref, o_ref, st_ref = refs
    c = pl.program_id(1)

    @pl.when(c == 0)
    def _():
        if has_state:
            st_ref[...] = s0_ref[...]
        else:
            st_ref[...] = jnp.zeros_like(st_ref)

    heads = [slice(h * GLA_DK, (h + 1) * GLA_DK) for h in range(GLA_HEADS)]
    values = [slice(h * GLA_DV, (h + 1) * GLA_DV) for h in range(GLA_HEADS)]
    row = lax.broadcasted_iota(jnp.int32, (chunk, 1), 0)

    def one_batch_row(b):
        q = q_ref[b]
        k = k_ref[b]
        attn = [None] * GLA_HEADS

        def add_level(qt, kt, level_index):
            for h, hs in enumerate(heads):
                scores = _dot_nt(qt[:, hs], kt[:, hs])
                if level_index is not None:
                    scores = scores * m_ref[level_index]
                attn[h] = scores if attn[h] is None else attn[h] + scores

        add_level(q.astype(BF16), k.astype(BF16), 0)

        lsum = la_ref[b] * LOG2_E
        rsum = jnp.zeros_like(lsum)
        s = 1
        level_index = 0
        while s < chunk:
            level_index += 1
            btot = lsum + rsum
            if s < SUBLANES:
                qt = (q * jnp.exp2(lsum)).astype(BF16)
                kt = (k * jnp.exp2(rsum)).astype(BF16)
                odd = (row & s) != 0
                lsum = lsum + jnp.where(odd, pltpu.roll(btot, s, 0), 0.0)
                rsum = rsum + jnp.where(odd, 0.0, pltpu.roll(btot, chunk - s, 0))
            else:
                zero = jnp.zeros((s, q.shape[1]), F32)
                q_slabs, k_slabs, l_slabs, r_slabs = [], [], [], []
                for lo in range(0, chunk, 2 * s):
                    ev, od = slice(lo, lo + s), slice(lo + s, lo + 2 * s)
                    q_slabs += [zero, q[od] * jnp.exp2(lsum[od])]
                    k_slabs += [k[ev] * jnp.exp2(rsum[ev]), zero]
                    l_slabs += [lsum[ev], lsum[od] + btot[ev]]
                    r_slabs += [rsum[ev] + btot[od], rsum[od]]
                qt = jnp.concatenate(q_slabs, axis=0).astype(BF16)
                kt = jnp.concatenate(k_slabs, axis=0).astype(BF16)
                lsum, rsum = jnp.concatenate(l_slabs, axis=0), jnp.concatenate(r_slabs, axis=0)
            add_level(qt, kt, None if s >= SUBLANES and 2 * s == chunk else level_index)
            s *= 2

        qt = (q * jnp.exp2(lsum)).astype(BF16)
        kd = (k * jnp.exp2(rsum)).astype(BF16)
        total = lsum[0:1, :] + rsum[0:1, :]
        for h, hs in enumerate(heads):
            st = st_ref[b, h]
            v = v_ref[b, :, values[h]]
            o_ref[b, :, values[h]] = _dot(attn[h].astype(BF16), v) + _dot(qt[:, hs], st.astype(BF16))
            tot_col = jnp.broadcast_to(total[:, hs], (GLA_DK, GLA_DK)).T[:, 0:1]
            st_ref[b, h] = jnp.exp2(tot_col) * st + _dot_tn(kd[:, hs], v)

    for b in range(q_ref.shape[0]):
        one_batch_row(b)


def _pair_masks():
    lane = lax.broadcasted_iota(jnp.int32, (1, LANES), 1)
    first = lane < HEAD_DIM
    return first, jnp.logical_not(first)


def _attn_prompt_kernel(q0_ref, q1_ref, q2_ref, k_ref, v_ref, bias_ref, o0_ref, o1_ref, o2_ref,
                        qd_ref, kd_ref, vd_ref, od_ref, lse_ref, *, n):
    q_refs = (q0_ref, q1_ref, q2_ref)
    o_refs = (o0_ref, o1_ref, o2_ref)
    head_masks = _pair_masks()
    sub = n // ATTN_SPLIT
    split_groups = [g for g, (_, dil) in enumerate(DIL_GROUPS) if dil % ATTN_SPLIT == 0]

    for cls in range(ATTN_SPLIT):
        src = pl.ds(cls, sub, stride=ATTN_SPLIT)
        dst = pl.ds(cls * sub, sub)
        kd_ref[dst, :] = k_ref[0, src, :]
        vd_ref[dst, :] = v_ref[0, src, :]
        for slot, g in enumerate(split_groups):
            qd_ref[slot, dst, :] = q_refs[g][0, src, :]

    def attend(qt, kt, vt, bias):
        qblk = qt.shape[0]
        qm = jnp.concatenate([jnp.where(hm, qt, 0.0) for hm in head_masks], axis=0).astype(BF16)
        s = _dot_nt(qm, kt) + bias
        m = jnp.max(s, axis=-1, keepdims=True)
        p = jnp.exp(s - m)
        den = jnp.sum(p, axis=-1, keepdims=True)
        pv = _dot(p.astype(BF16), vt)
        pick = lambda a: jnp.where(head_masks[0], a[:qblk], a[qblk:])
        den = pick(den)
        return pick(pv) / den, pick(m) + jnp.log(den)

    for g, (win, dil) in enumerate(DIL_GROUPS):
        length = n // dil
        qblk = min(ATTN_QBLOCK, length)
        nblk = length // qblk
        nkeys = min(2 * qblk, length)
        for idx in range(dil * nblk):
            blk = idx // dil
            u0 = max(blk - 1, 0) * qblk
            bias = bias_ref[2 * g + min(blk, 1), :, :nkeys]
            if g in split_groups:
                slot = split_groups.index(g)
                step = dil // ATTN_SPLIT
                base = (idx % ATTN_SPLIT) * sub + (idx // ATTN_SPLIT) % step
                q_rows = pl.ds(base + step * blk * qblk, qblk, stride=step)
                k_rows = pl.ds(base + step * u0, nkeys, stride=step)
                o, lse = attend(qd_ref[slot, q_rows, :], kd_ref[k_rows, :].astype(BF16),
                                vd_ref[k_rows, :].astype(BF16), bias)
                od_ref[slot, q_rows, :] = o
            else:
                assert dil == 1
                q_rows = pl.ds(blk * qblk, qblk)
                k_rows = pl.ds(u0, nkeys)
                o, lse = attend(q_refs[g][0, q_rows, :], k_ref[0, k_rows, :].astype(BF16),
                                v_ref[0, k_rows, :].astype(BF16), bias)
                o_refs[g][0, q_rows, :] = o
            lse_ref[g, q_rows, :] = lse

    rows_per_step = REWEIGHT_ROWS
    steps_per_class = sub // rows_per_step

    def reweight(t, carry):
        cls = t // steps_per_class
        start = (t % steps_per_class) * rows_per_step
        nat = pl.ds(cls + ATTN_SPLIT * start, rows_per_step, stride=ATTN_SPLIT)
        grp = pl.ds(pl.multiple_of(cls * sub + start, rows_per_step), rows_per_step)
        lse = [lse_ref[g, grp if g in split_groups else nat, :] for g in range(N_GROUPS)]
        top = jnp.maximum(jnp.maximum(lse[0], lse[1]), lse[2])
        e = [jnp.exp(l - top) for l in lse]
        tot = e[0] + e[1] + e[2]
        for g in range(N_GROUPS):
            if g in split_groups:
                o = od_ref[split_groups.index(g), grp, :]
            else:
                o = o_refs[g][0, nat, :]
            o_refs[g][0, nat, :] = o * (e[g] / tot)
        return carry

    lax.fori_loop(0, ATTN_SPLIT * steps_per_class, reweight, 0)


def _attn_sample_kernel(q0_ref, q1_ref, q2_ref, kn_ref, vn_ref, kc_ref, vc_ref, o0_ref, o1_ref, o2_ref,
                        *, n, cache_len):
    q_refs = (q0_ref, q1_ref, q2_ref)
    o_refs = (o0_ref, o1_ref, o2_ref)
    per_head = N_GROUPS * n
    rows = N_KV_HEADS * per_head
    r = lax.broadcasted_iota(jnp.int32, (rows, 1), 0)
    grp = (r % per_head) // n
    tq = r % n
    dil = jnp.zeros_like(r)
    win = jnp.zeros_like(r)
    for g, (w_g, d_g) in enumerate(DIL_GROUPS):
        dil = jnp.where(grp == g, d_g, dil)
        win = jnp.where(grp == g, w_g, win)
    d_cache = (cache_len + tq) - lax.broadcasted_iota(jnp.int32, (1, cache_len), 1)
    valid_c = ((d_cache & (dil - 1)) == 0) & (d_cache <= win)
    d_new = tq - lax.broadcasted_iota(jnp.int32, (1, n), 1)
    valid_n = (d_new >= 0) & ((d_new & (dil - 1)) == 0) & (d_new <= win)
    own_lanes = (lax.broadcasted_iota(jnp.int32, (1, KV_DIM), 1) // HEAD_DIM) == (r // per_head)

    q = jnp.concatenate([q_ref[0] for q_ref in q_refs] * N_KV_HEADS, axis=0)
    q = jnp.where(own_lanes, q, 0.0).astype(BF16)
    sc = jnp.where(valid_c, _dot(q, kc_ref[0].astype(BF16)), -jnp.inf)
    sn = jnp.where(valid_n, _dot_nt(q, kn_ref[0].astype(BF16)), -jnp.inf)
    m = jnp.maximum(jnp.max(sc, axis=-1, keepdims=True), jnp.max(sn, axis=-1, keepdims=True))
    pc = jnp.exp(sc - m)
    pn = jnp.exp(sn - m)
    den = jnp.sum(pc, axis=-1, keepdims=True) + jnp.sum(pn, axis=-1, keepdims=True)
    out = (_dot_nt(pc.astype(BF16), vc_ref[0].astype(BF16)) + _dot(pn.astype(BF16), vn_ref[0].astype(BF16))) / den
    lse = m + jnp.log(den)
    acc = [None] * N_GROUPS
    for h in range(N_KV_HEADS):
        blocks = [slice(h * per_head + g * n, h * per_head + (g + 1) * n) for g in range(N_GROUPS)]
        lses = [lse[b] for b in blocks]
        top = jnp.maximum(jnp.maximum(lses[0], lses[1]), lses[2])
        e = [jnp.exp(l - top) for l in lses]
        tot = e[0] + e[1] + e[2]
        for g, b in enumerate(blocks):
            term = jnp.where(own_lanes[b], out[b] * (e[g] / tot), 0.0)
            acc[g] = term if acc[g] is None else acc[g] + term
    for g in range(N_GROUPS):
        o_refs[g][0] = acc[g]


def _params(semantics):
    return pltpu.CompilerParams(dimension_semantics=semantics, vmem_limit_bytes=VMEM_LIMIT)


def _row_spec(tm, width):
    return pl.BlockSpec((tm, width), lambda i: (i, 0))


def _const_spec(shape):
    return pl.BlockSpec(shape, lambda i: (0,) * len(shape), pipeline_mode=pl.Buffered(1))


def _with_riders(kernel_fn, n_in, n_out, n_riders):
    def kernel(*refs):
        ins, srcs = refs[:n_in], refs[n_in:n_in + n_riders]
        outs, dsts = refs[n_in + n_riders:n_in + n_riders + n_out], refs[n_in + n_riders + n_out:]
        kernel_fn(*ins, *outs)
        for src_ref, dst_ref in zip(srcs, dsts):
            dst_ref[...] = src_ref[...].astype(BF16)
    return kernel


def _rowwise_call(kernel_fn, name, rows, row_inputs, consts, tables, outs, seq_len=None, n_transposed=0,
                  row_tile=ROW_TILE, riders=()):
    tm = row_tile if rows % row_tile == 0 else rows
    if tables:
        tm = min(tm, tables[0].shape[0])
    assert rows % tm == 0 and all(t.shape[0] % tm == 0 for t in tables)
    in_specs = [_row_spec(tm, a.shape[1]) for a in row_inputs]
    in_specs += [_const_spec(c.shape) for c in consts]
    for t in tables:
        period = t.shape[0] // tm
        in_specs.append(pl.BlockSpec((tm, t.shape[1]), lambda i, period=period: (i % period, 0)))
    out_specs = [_row_spec(tm, width) for width, _ in outs]
    out_shape = [jax.ShapeDtypeStruct((rows, width), dtype) for width, dtype in outs]
    if n_transposed:
        assert seq_len % tm == 0
        per_seq = seq_len // tm
        out_specs += [pl.BlockSpec((1, KV_DIM, tm), lambda i: (i // per_seq, 0, i % per_seq))] * n_transposed
        out_shape += [jax.ShapeDtypeStruct((rows // seq_len, KV_DIM, seq_len), F32)] * n_transposed
    operands = [*row_inputs, *consts, *tables]
    if riders:
        kernel_fn = _with_riders(kernel_fn, len(operands), len(out_specs), len(riders))
    n_steps = rows // tm
    for stack, index in riders:
        r_rows, width = stack.shape[-2:]
        per_step = next(r for r in range(BF16_ROWS, r_rows + 1, BF16_ROWS)
                        if r_rows % r == 0 and r_rows // r <= n_steps)
        last = r_rows // per_step - 1
        in_specs.append(pl.BlockSpec((None,) * len(index) + (per_step, width),
                                     lambda i, index=index, last=last: (*index, jnp.minimum(i, last), 0)))
        out_specs.append(pl.BlockSpec((per_step, width), lambda i, last=last: (jnp.minimum(i, last), 0)))
        out_shape.append(jax.ShapeDtypeStruct((r_rows, width), BF16))
        operands.append(stack)
    return pl.pallas_call(
        kernel_fn,
        grid=(n_steps,),
        in_specs=in_specs,
        out_specs=out_specs,
        out_shape=out_shape,
        compiler_params=_params(("arbitrary",) if riders else ("parallel",)),
        name=name,
    )(*operands)


def _cast_kernel(*refs, windows):
    srcs, dsts = refs[:len(windows)], iter(refs[len(windows):])
    for src_ref, cols in zip(srcs, windows):
        for c0, c1 in cols:
            next(dsts)[...] = src_ref[:, c0:c1].astype(BF16)


def _cast_weights_call(items):
    in_specs, out_specs, out_shape, operands = [], [], [], []
    for arr, index, cols in items:
        rows, width = arr.shape[-2:]
        assert rows % (CAST_STEPS * BF16_ROWS) == 0
        blk = rows // CAST_STEPS
        in_specs.append(pl.BlockSpec((None,) * len(index) + (blk, width),
                                     lambda i, index=index: (*index, i, 0)))
        operands.append(arr)
        for c0, c1 in cols:
            out_specs.append(pl.BlockSpec((blk, c1 - c0), lambda i: (i, 0)))
            out_shape.append(jax.ShapeDtypeStruct((rows, c1 - c0), BF16))
    return pl.pallas_call(
        functools.partial(_cast_kernel, windows=[cols for _, _, cols in items]),
        grid=(CAST_STEPS,),
        in_specs=in_specs,
        out_specs=out_specs,
        out_shape=out_shape,
        compiler_params=_params(("parallel",)),
        name="cast_weights",
    )(*operands)


def _gla_level_masks(chunk):
    i = np.arange(chunk)[:, None]
    j = np.arange(chunk)[None, :]
    levels = [i == j]
    s = 1
    while s < chunk:
        levels.append(((i ^ j) < 2 * s) & ((i & s) != 0) & ((j & s) == 0))
        s *= 2
    return jnp.asarray(np.stack(levels).astype(np.float32))


def _gla_rec_call(q, k, v, la, s0):
    batch, n, _ = q.shape
    chunk = min(GLA_CHUNK, n)
    assert n % chunk == 0 and chunk & (chunk - 1) == 0
    has_state = s0 is not None
    bb = GLA_BATCH_BLOCK if batch % GLA_BATCH_BLOCK == 0 else 1
    seq_spec = lambda width: pl.BlockSpec((bb, chunk, width), lambda b, c: (b, c, 0))
    state_spec = pl.BlockSpec((bb, GLA_HEADS, GLA_DK, GLA_DV), lambda b, c: (b, 0, 0, 0))
    masks = _gla_level_masks(chunk)
    in_specs = [seq_spec(GLA_QK), seq_spec(GLA_QK), seq_spec(GLA_V), seq_spec(GLA_QK),
                pl.BlockSpec(masks.shape, lambda b, c: (0, 0, 0), pipeline_mode=pl.Buffered(1))]
    args = [q, k, v, la, masks]
    if has_state:
        in_specs.append(state_spec)
        args.append(s0)
    return pl.pallas_call(
        functools.partial(_gla_rec_kernel, chunk=chunk, has_state=has_state),
        grid=(batch // bb, n // chunk),
        in_specs=in_specs,
        out_specs=[seq_spec(GLA_V), state_spec],
        out_shape=[jax.ShapeDtypeStruct((batch, n, GLA_V), F32),
                   jax.ShapeDtypeStruct((batch, GLA_HEADS, GLA_DK, GLA_DV), F32)],
        compiler_params=_params(("parallel", "arbitrary")),
        name="gla_rec",
    )(*args)


def _attn_window_bias(n):
    out = []
    for win, dil in DIL_GROUPS:
        qblk = min(ATTN_QBLOCK, n // dil)
        r = np.arange(2 * qblk)[:, None] % qblk
        c = np.arange(2 * ATTN_QBLOCK)[None, :]
        for shift in (0, qblk):
            delta = shift + r - c
            out.append(np.where((delta >= 0) & (delta <= win // dil), 0.0, -np.inf))
    return jnp.asarray(np.stack(out).astype(np.float32))


def _attn_prompt_call(qs, k, v):
    batch, n, _ = k.shape
    assert n % (ATTN_QBLOCK * max(d for _, d in DIL_GROUPS)) == 0
    n_split = sum(1 for _, d in DIL_GROUPS if d % ATTN_SPLIT == 0)
    spec = pl.BlockSpec((1, n, LANES), lambda b, p: (b, 0, p))
    bias = _attn_window_bias(n)
    bias_spec = pl.BlockSpec(bias.shape, lambda b, p: (0, 0, 0), pipeline_mode=pl.Buffered(1))
    return pl.pallas_call(
        functools.partial(_attn_prompt_kernel, n=n),
        grid=(batch, KV_DIM // LANES),
        in_specs=[spec] * (N_GROUPS + 2) + [bias_spec],
        out_specs=[spec] * N_GROUPS,
        out_shape=[jax.ShapeDtypeStruct((batch, n, KV_DIM), F32)] * N_GROUPS,
        scratch_shapes=[pltpu.VMEM((n_split, n, LANES), F32), pltpu.VMEM((n, LANES), F32),
                        pltpu.VMEM((n, LANES), F32), pltpu.VMEM((n_split, n, LANES), F32),
                        pltpu.VMEM((N_GROUPS, n, LANES), F32)],
        compiler_params=_params(("parallel", "parallel")),
        name="attn_prompt",
    )(*qs, k, v, bias)


def _attn_sample_call(qs, k_new, v_new, k_cache, v_cache):
    batch, n, _ = k_new.shape
    cache_len = k_cache.shape[2]
    new_spec = pl.BlockSpec((1, n, KV_DIM), lambda b: (b, 0, 0))
    cache_spec = pl.BlockSpec((1, KV_DIM, cache_len), lambda b: (b, 0, 0))
    return pl.pallas_call(
        functools.partial(_attn_sample_kernel, n=n, cache_len=cache_len),
        grid=(batch,),
        in_specs=[new_spec] * (N_GROUPS + 2) + [cache_spec] * 2,
        out_specs=[new_spec] * N_GROUPS,
        out_shape=[jax.ShapeDtypeStruct((batch, n, KV_DIM), F32)] * N_GROUPS,
        compiler_params=_params(("parallel",)),
        name="attn_sample",
    )(*qs, k_new, v_new, k_cache, v_cache)


def _rope_tables(pos0, n, reps):
    half = HEAD_DIM // 2
    inv = ROPE_THETA ** (-np.arange(half, dtype=np.float64) / half)
    ang = (pos0 + np.arange(n, dtype=np.float64))[:, None] * inv[None, :]
    cos = np.concatenate([np.cos(ang), np.cos(ang)], axis=-1)
    sin = np.concatenate([-np.sin(ang), np.sin(ang)], axis=-1)
    cos = np.tile(cos, (reps, N_KV_HEADS)).astype(np.float32)
    sin = np.tile(sin, (reps, N_KV_HEADS)).astype(np.float32)
    return jnp.asarray(cos), jnp.asarray(sin)


def _head_mean_matrix():
    idx = np.arange(KV_DIM) // HEAD_DIM
    return jnp.asarray((idx[:, None] == idx[None, :]).astype(np.float32) / HEAD_DIM, dtype=BF16)


def _run_group(x, pos0, s0, caches, w, ffn_bf16=None):
    batch, n, _ = x.shape
    rows = batch * n
    x = x.reshape(rows, D_MODEL)
    reps = 1 if n % ROW_TILE == 0 else batch
    cos, sin = _rope_tables(pos0, n, reps)
    avg = _head_mean_matrix()

    gain = w["gain"]
    later = [(l, i) for l in range(2) for i in range(2)][1:]
    riders = [(stack, m) for stack in w["ffn_f32"] for m in later] if ffn_bf16 is None else []
    seq = lambda a: a.reshape(batch, n, a.shape[-1])
    wide, qk, kv = (D_MODEL, F32), (GLA_QK, F32), (KV_DIM, F32)

    x1, q, k, v, la, *converted = _rowwise_call(
        _ffn_gla_in_kernel, "ffn_gla_in", rows, [x],
        [gain[0][0], *(ffn_bf16 or w["ffn_first"])[0, 0], gain[0][1], w["gla_mix"], w["gla_g2"], w["gla_bg"]],
        [], [wide, qk, qk, (GLA_V, BF16), qk], riders=riders)
    if ffn_bf16 is None:
        ffn_bf16 = dict(w["ffn_first"])
        ffn_bf16.update({m: (converted[j], converted[len(later) + j]) for j, m in enumerate(later)})
    ffn_in = [[ffn_bf16[l, i][0] for i in range(2)] for l in range(2)]
    ffn_out = [[ffn_bf16[l, i][1] for i in range(2)] for l in range(2)]
    o, s_fin = _gla_rec_call(seq(q), seq(k), seq(v), seq(la), s0)
    (x3,) = _rowwise_call(
        _gla_out_ffn_kernel, "gla_out_ffn", rows, [o.reshape(rows, GLA_V), x1],
        [gain[0][1], w["gla_r"], w["gla_norm"], w["gla_out"], gain[0][2], ffn_in[0][1], ffn_out[0][1]], [],
        [wide])

    transposed = n % ROW_TILE == 0
    x4, k_new, v_new, *rest = _rowwise_call(
        _kv_ffn_q_kernel, "kv_ffn_q", rows, [x3],
        [w["kv_gain"], w["kv_w"], w["k_norm"], avg, gain[1][0], ffn_in[1][0], ffn_out[1][0], gain[1][1],
         w["attn_q"], w["q_norm"]], [cos, sin],
        [wide, kv, kv] + [kv] * N_GROUPS, seq_len=n, n_transposed=2 if transposed else 0)
    qs = rest[:N_GROUPS]

    qs = [seq(a) for a in qs]
    if caches is None:
        os_ = _attn_prompt_call(qs, seq(k_new), seq(v_new))
    else:
        os_ = _attn_sample_call(qs, seq(k_new), seq(v_new), *caches)
    (x,) = _rowwise_call(
        _attn_out_ffn_kernel, "attn_out_ffn", rows, [a.reshape(rows, KV_DIM) for a in os_] + [x4],
        [w["attn_out"], gain[1][2], ffn_in[1][1], ffn_out[1][1]], [],
        [wide], row_tile=LAST_ROW_TILE)

    if transposed:
        k_out, v_out = (jnp.transpose(a.reshape(batch, N_KV_HEADS, HEAD_DIM, n), (0, 3, 1, 2))
                        for a in rest[N_GROUPS:])
    else:
        k_out, v_out = (a.reshape(batch, n, N_KV_HEADS, HEAD_DIM) for a in (k_new, v_new))
    return (x.reshape(batch, n, D_MODEL), s_fin[None], k_out, v_out), ffn_bf16


def kernel(x_prompt, x_sample, state_gla, cache_k_win, cache_v_win, norm_gains, ffn_w_in, ffn_w_out,
           gla_w_in, gla_w_gate2, gla_b_gate, gla_out_norm, gla_w_out, kv_norm, kv_w, k_norm,
           attn_w_q, q_norm, attn_w_out):
    assert norm_gains.shape[0] == 2 and gla_w_in.shape[0] == 1 and attn_w_q.shape[0] == 1
    row = lambda a: a.reshape(1, -1)
    whole = lambda a: [(0, a.shape[-1])]
    gate0 = 2 * GLA_QK + GLA_V
    pad = LANES - GLA_RANK
    first_in, first_out, gla_out, kv_w16, attn_q, attn_out = _cast_weights_call([
        (ffn_w_in, (0, 0), whole(ffn_w_in)), (ffn_w_out, (0, 0), whole(ffn_w_out)),
        (gla_w_out, (0,), whole(gla_w_out)), (kv_w, (), whole(kv_w)),
        (attn_w_q, (0,), whole(attn_w_q)), (attn_w_out, (0,), whole(attn_w_out))])
    gla_mix = gla_w_in[0, :, :gate0 + LANES].astype(BF16)
    gla_r = gla_w_in[0, :, gate0 + GLA_RANK:].astype(BF16)
    w = {
        "gain": [[row(norm_gains[l, i]) for i in range(3)] for l in range(2)],
        "ffn_first": {(0, 0): (first_in, first_out)},
        "ffn_f32": (ffn_w_in, ffn_w_out),
        "gla_mix": gla_mix,
        "gla_r": gla_r,
        "gla_g2": jnp.pad(gla_w_gate2[0].astype(BF16), ((0, pad), (0, 0))),
        "gla_bg": row(gla_b_gate[0]),
        "gla_norm": row(gla_out_norm[0]),
        "gla_out": gla_out,
        "kv_gain": row(kv_norm),
        "kv_w": kv_w16,
        "k_norm": row(jnp.tile(k_norm, N_KV_HEADS)),
        "attn_q": attn_q,
        "q_norm": row(jnp.tile(q_norm[0], N_KV_HEADS)),
        "attn_out": attn_out,
    }
    caches = tuple(jnp.transpose(c, (0, 2, 3, 1)).reshape(c.shape[0], KV_DIM, c.shape[1])
                   for c in (cache_k_win, cache_v_win))

    (y_p, s_p, k_p, v_p), ffn_bf16 = _run_group(x_prompt, 0, None, None, w)
    (y_s, s_s, k_s, v_s), _ = _run_group(x_sample, PAST_LEN, state_gla[0], caches, w, ffn_bf16)
    keep = min(MAX_WINDOW, x_prompt.shape[1])
    return (y_p, y_s, s_p, s_s, k_p[:, -keep:], v_p[:, -keep:], k_s, v_s)
```

```python
import functools

import numpy as np
import jax
import jax.numpy as jnp
from jax import lax
from jax.experimental import pallas as pl
from jax.experimental.pallas import tpu as pltpu

F32 = jnp.float32
BF16 = jnp.bfloat16

D_MODEL = 1024
FFN_DIM = 2688
EPS = 1e-6
PAST_LEN = 8192
GLA_HEADS = 4
GLA_DK = 128
GLA_DV = 256
GLA_RANK = 16
GLA_TAU = 16.0
GLA_QK = GLA_HEADS * GLA_DK
GLA_V = GLA_HEADS * GLA_DV
HEAD_DIM = 64
N_KV_HEADS = 4
KV_DIM = N_KV_HEADS * HEAD_DIM
DIL_GROUPS = ((128, 1), (512, 4), (2048, 16))
N_GROUPS = len(DIL_GROUPS)
MAX_WINDOW = max(w for w, _ in DIL_GROUPS)
ROPE_THETA = 10000.0

LANES = 128
SUBLANES = 8
BF16_ROWS = 16
GLA_CHUNK = 128
GLA_BATCH_BLOCK = 8
LOG2_E = 1.4426950408889634
ATTN_QBLOCK = 128
ROW_TILE = 512
LAST_ROW_TILE = 1024
MXU_TILE = 256
FFN_MAIN = (FFN_DIM // MXU_TILE) * MXU_TILE
FFN_REM = FFN_DIM - FFN_MAIN
FFN_SPLITS = (1280,)
ATTN_SPLIT = 4
TAIL_SLABS = 2
REWEIGHT_ROWS = 256
CAST_STEPS = 8
VMEM_LIMIT = 52 * 1024 * 1024


def _dot(a, b):
    return jnp.dot(a, b, preferred_element_type=F32)


def _dot_nt(a, b):
    return lax.dot_general(a, b, (((1,), (1,)), ((), ())), preferred_element_type=F32)


def _dot_tn(a, b):
    return lax.dot_general(a, b, (((0,), (0,)), ((), ())), preferred_element_type=F32)


def _rms(x, g):
    ms = jnp.mean(x * x, axis=-1, keepdims=True)
    return x * lax.rsqrt(ms + EPS) * g


def _silu(x):
    return x * jax.nn.sigmoid(x)


def _group_mean_sq(x, avg):
    sq = x * x
    hi = sq.astype(BF16)
    lo = (sq - hi.astype(F32)).astype(BF16)
    return _dot(hi, avg) + _dot(lo, avg)


def _rope(x, cos, sin_signed):
    w = x.shape[-1]
    lane = lax.broadcasted_iota(jnp.int32, (1, w), 1)
    first_half = (lane % HEAD_DIM) < (HEAD_DIM // 2)
    rot = jnp.where(first_half, pltpu.roll(x, w - HEAD_DIM // 2, 1), pltpu.roll(x, HEAD_DIM // 2, 1))
    return x * cos + rot * sin_signed


def _ffn_half(x, g_ref, win_ref, wout_ref):
    h = _rms(x, g_ref[...]).astype(BF16)
    acc = None
    for c0, c1 in zip((0,) + FFN_SPLITS, FFN_SPLITS + (FFN_MAIN,)):
        gate = _dot(h, win_ref[:, c0:c1])
        up = _dot(h, win_ref[:, FFN_DIM + c0:FFN_DIM + c1])
        act = (_silu(gate) * up).astype(BF16)
        part = _dot(act, wout_ref[c0:c1, :])
        acc = part if acc is None else acc + part
    if FFN_REM:
        w_rem = jnp.concatenate([win_ref[:, FFN_MAIN:FFN_DIM], win_ref[:, FFN_DIM + FFN_MAIN:]], axis=1)
        gate_up = _dot(h, w_rem)
        act = (_silu(gate_up[:, :FFN_REM]) * gate_up[:, FFN_REM:]).astype(BF16)
        acc = acc + _dot(act, wout_ref[FFN_MAIN:, :])
    return x + 0.5 * acc


def _ffn_half_slabs(x, g_ref, win_ref, wout_ref, n_slabs):
    h = _rms(x, g_ref[...]).astype(BF16)
    bounds = list(zip((0,) + FFN_SPLITS, FFN_SPLITS + (FFN_MAIN,)))
    acts = []
    for c0, c1 in bounds:
        gate = _dot(h, win_ref[:, c0:c1])
        up = _dot(h, win_ref[:, FFN_DIM + c0:FFN_DIM + c1])
        acts.append((_silu(gate) * up).astype(BF16))
    w_rem = jnp.concatenate([win_ref[:, FFN_MAIN:FFN_DIM], win_ref[:, FFN_DIM + FFN_MAIN:]], axis=1)
    gate_up = _dot(h, w_rem)
    act_rem = (_silu(gate_up[:, :FFN_REM]) * gate_up[:, FFN_REM:]).astype(BF16)
    acc = None
    for (c0, c1), act in zip(bounds[:-1], acts[:-1]):
        part = _dot(act, wout_ref[c0:c1, :])
        acc = part if acc is None else acc + part
    c0, c1 = bounds[-1]
    slab = x.shape[0] // n_slabs
    outs = []
    for s in range(n_slabs):
        rows = slice(s * slab, (s + 1) * slab)
        tail = _dot(acts[-1][rows], wout_ref[c0:c1, :]) + _dot(act_rem[rows], wout_ref[FFN_MAIN:, :])
        outs.append(x[rows] + 0.5 * (acc[rows] + tail))
    return outs


def _ffn_gla_in_kernel(x_ref, g_ffn_ref, win_ref, wout_ref, g_mix_ref, wmix_ref, wg2_ref, bg_ref,
                       x1_ref, q_ref, k_ref, v_ref, la_ref):
    slabs = _ffn_half_slabs(x_ref[...], g_ffn_ref, win_ref, wout_ref, TAIL_SLABS)
    slab = x_ref.shape[0] // TAIL_SLABS
    qk_end = 2 * GLA_QK
    for s, x1 in enumerate(slabs):
        rows = slice(s * slab, (s + 1) * slab)
        x1_ref[rows, :] = x1
        h = _rms(x1, g_mix_ref[...]).astype(BF16)
        g_lr = _dot(h, wmix_ref[:, qk_end + GLA_V:])
        z = _dot(g_lr.astype(BF16), wg2_ref[...]) + bg_ref[...]
        log_sig = jnp.minimum(z, 0.0) - jnp.log1p(jnp.exp(-jnp.abs(z)))
        la_ref[rows, :] = log_sig * (1.0 / GLA_TAU)
        q_ref[rows, :] = _dot(h, wmix_ref[:, :GLA_QK]) * (GLA_DK ** -0.5)
        k_ref[rows, :] = _dot(h, wmix_ref[:, GLA_QK:qk_end])
        v_ref[rows, :] = _dot(h, wmix_ref[:, qk_end:qk_end + GLA_V]).astype(BF16)


def _gla_out_ffn_kv_kernel(o_ref, x1_ref, g_mix_ref, wr_ref, gn_ref, wo_ref, g_ffn_ref, win_ref, wout_ref,
                           g_kv_ref, wkv_ref, kn_ref, avg_ref, cos_ref, sin_ref, x3_ref, k_ref, v_ref,
                           *kv_t_refs):
    x1 = x1_ref[...]
    r = _dot(_rms(x1, g_mix_ref[...]).astype(BF16), wr_ref[...])
    o = o_ref[...]
    on = jnp.concatenate([_rms(o[:, h * GLA_DV:(h + 1) * GLA_DV], gn_ref[...]) for h in range(GLA_HEADS)],
                         axis=1)
    x2 = x1 + _dot((on * _silu(r)).astype(BF16), wo_ref[...])
    slabs = _ffn_half_slabs(x2, g_ffn_ref, win_ref, wout_ref, TAIL_SLABS)
    slab = x1_ref.shape[0] // TAIL_SLABS
    avg = avg_ref[...]
    for s, x3 in enumerate(slabs):
        rows = slice(s * slab, (s + 1) * slab)
        x3_ref[rows, :] = x3
        kv = _dot(_rms(x3, g_kv_ref[...]).astype(BF16), wkv_ref[...])
        k = kv[:, :KV_DIM]
        kn = k * lax.rsqrt(_group_mean_sq(k, avg) + EPS) * kn_ref[...]
        k_rot = _rope(kn, cos_ref[rows, :], sin_ref[rows, :])
        k_ref[rows, :] = k_rot
        v_ref[rows, :] = kv[:, KV_DIM:]
        if kv_t_refs:
            kt_ref, vt_ref = kv_t_refs
            kt_ref[0, :, rows] = k_rot.T
            vt_ref[0, :, rows] = kv[:, KV_DIM:].T


def _ffn_q_kernel(x3_ref, avg_ref, g_ffn_ref, win_ref, wout_ref, g_mix_ref, wq_ref, qn_ref, cos_ref, sin_ref,
                  x4_ref, q0_ref, q1_ref, q2_ref):
    x3 = x3_ref[...]
    cos, sin, avg = cos_ref[...], sin_ref[...], avg_ref[...]
    x4 = _ffn_half(x3, g_ffn_ref, win_ref, wout_ref)
    x4_ref[...] = x4
    h = _rms(x4, g_mix_ref[...]).astype(BF16)
    for g, q_ref in enumerate((q0_ref, q1_ref, q2_ref)):
        q = _dot(h, wq_ref[:, g * KV_DIM:(g + 1) * KV_DIM])
        qn = q * lax.rsqrt(_group_mean_sq(q, avg) + EPS) * qn_ref[...]
        q_ref[...] = _rope(qn, cos, sin) * (HEAD_DIM ** -0.5)


def _attn_out_ffn_kernel(o0_ref, o1_ref, o2_ref, x4_ref, wo_ref, g_ffn_ref, win_ref, wout_ref, y_ref):
    x5 = x4_ref[...]
    for g, o_ref in enumerate((o0_ref, o1_ref, o2_ref)):
        x5 = x5 + _dot(o_ref[...].astype(BF16), wo_ref[g * KV_DIM:(g + 1) * KV_DIM, :])
    y_ref[...] = _ffn_half(x5, g_ffn_ref, win_ref, wout_ref)


def _gla_rec_kernel(*refs, chunk, has_state):
    if has_state:
        q_ref, k_ref, v_ref, la_ref, m_ref, s0_ref, o_ref, st_ref = refs
    else:
        q_ref, k_ref, v_ref, la_ref, m_ref, o_ref, st_ref = refs
    c = pl.program_id(1)

    @pl.when(c == 0)
    def _():
        if has_state:
            st_ref[...] = s0_ref[...]
        else:
            st_ref[...] = jnp.zeros_like(st_ref)

    heads = [slice(h * GLA_DK, (h + 1) * GLA_DK) for h in range(GLA_HEADS)]
    values = [slice(h * GLA_DV, (h + 1) * GLA_DV) for h in range(GLA_HEADS)]
    row = lax.broadcasted_iota(jnp.int32, (chunk, 1), 0)

    def one_batch_row(b):
        q = q_ref[b]
        k = k_ref[b]
        attn = [None] * GLA_HEADS

        def add_level(qt, kt, level_index):
            for h, hs in enumerate(heads):
                scores = _dot_nt(qt[:, hs], kt[:, hs])
                if level_index is not None:
                    scores = scores * m_ref[level_index]
                attn[h] = scores if attn[h] is None else attn[h] + scores

        add_level(q.astype(BF16), k.astype(BF16), 0)

        lsum = la_ref[b] * LOG2_E
        rsum = jnp.zeros_like(lsum)
        s = 1
        level_index = 0
        while s < chunk:
            level_index += 1
            btot = lsum + rsum
            if s < SUBLANES:
                qt = (q * jnp.exp2(lsum)).astype(BF16)
                kt = (k * jnp.exp2(rsum)).astype(BF16)
                odd = (row & s) != 0
                lsum = lsum + jnp.where(odd, pltpu.roll(btot, s, 0), 0.0)
                rsum = rsum + jnp.where(odd, 0.0, pltpu.roll(btot, chunk - s, 0))
            else:
                zero = jnp.zeros((s, q.shape[1]), F32)
                q_slabs, k_slabs, l_slabs, r_slabs = [], [], [], []
                for lo in range(0, chunk, 2 * s):
                    ev, od = slice(lo, lo + s), slice(lo + s, lo + 2 * s)
                    q_slabs += [zero, q[od] * jnp.exp2(lsum[od])]
                    k_slabs += [k[ev] * jnp.exp2(rsum[ev]), zero]
                    l_slabs += [lsum[ev], lsum[od] + btot[ev]]
                    r_slabs += [rsum[ev] + btot[od], rsum[od]]
                qt = jnp.concatenate(q_slabs, axis=0).astype(BF16)
                kt = jnp.concatenate(k_slabs, axis=0).astype(BF16)
                lsum, rsum = jnp.concatenate(l_slabs, axis=0), jnp.concatenate(r_slabs, axis=0)
            add_level(qt, kt, None if s >= SUBLANES and 2 * s == chunk else level_index)
            s *= 2

        qt = (q * jnp.exp2(lsum)).astype(BF16)
        kd = (k * jnp.exp2(rsum)).astype(BF16)
        total = lsum[0:1, :] + rsum[0:1, :]
        for h, hs in enumerate(heads):
            st = st_ref[b, h]
            v = v_ref[b, :, values[h]]
            o_ref[b, :, values[h]] = _dot(attn[h].astype(BF16), v) + _dot(qt[:, hs], st.astype(BF16))
            tot_col = jnp.broadcast_to(total[:, hs], (GLA_DK, GLA_DK)).T[:, 0:1]
            st_ref[b, h] = jnp.exp2(tot_col) * st + _dot_tn(kd[:, hs], v)

    for b in range(q_ref.shape[0]):
        one_batch_row(b)


def _pair_masks():
    lane = lax.broadcasted_iota(jnp.int32, (1, LANES), 1)
    first = lane < HEAD_DIM
    return first, jnp.logical_not(first)


def _attn_prompt_kernel(q0_ref, q1_ref, q2_ref, k_ref, v_ref, bias_ref, o0_ref, o1_ref, o2_ref,
                        qd_ref, kd_ref, vd_ref, od_ref, lse_ref, *, n):
    q_refs = (q0_ref, q1_ref, q2_ref)
    o_refs = (o0_ref, o1_ref, o2_ref)
    head_masks = _pair_masks()
    sub = n // ATTN_SPLIT
    split_groups = [g for g, (_, dil) in enumerate(DIL_GROUPS) if dil % ATTN_SPLIT == 0]

    for cls in range(ATTN_SPLIT):
        src = pl.ds(cls, sub, stride=ATTN_SPLIT)
        dst = pl.ds(cls * sub, sub)
        kd_ref[dst, :] = k_ref[0, src, :]
        vd_ref[dst, :] = v_ref[0, src, :]
        for slot, g in enumerate(split_groups):
            qd_ref[slot, dst, :] = q_refs[g][0, src, :]

    def attend(qt, kt, vt, bias):
        qblk = qt.shape[0]
        qm = jnp.concatenate([jnp.where(hm, qt, 0.0) for hm in head_masks], axis=0).astype(BF16)
        s = _dot_nt(qm, kt) + bias
        m = jnp.max(s, axis=-1, keepdims=True)
        p = jnp.exp(s - m)
        den = jnp.sum(p, axis=-1, keepdims=True)
        pv = _dot(p.astype(BF16), vt)
        pick = lambda a: jnp.where(head_masks[0], a[:qblk], a[qblk:])
        den = pick(den)
        return pick(pv) / den, pick(m) + jnp.log(den)

    for g, (win, dil) in enumerate(DIL_GROUPS):
        length = n // dil
        qblk = min(ATTN_QBLOCK, length)
        nblk = length // qblk
        nkeys = min(2 * qblk, length)
        for idx in range(dil * nblk):
            blk = idx // dil
            u0 = max(blk - 1, 0) * qblk
            bias = bias_ref[2 * g + min(blk, 1), :, :nkeys]
            if g in split_groups:
                slot = split_groups.index(g)
                step = dil // ATTN_SPLIT
                base = (idx % ATTN_SPLIT) * sub + (idx // ATTN_SPLIT) % step
                q_rows = pl.ds(base + step * blk * qblk, qblk, stride=step)
                k_rows = pl.ds(base + step * u0, nkeys, stride=step)
                o, lse = attend(qd_ref[slot, q_rows, :], kd_ref[k_rows, :].astype(BF16),
                                vd_ref[k_rows, :].astype(BF16), bias)
                od_ref[slot, q_rows, :] = o
            else:
                assert dil == 1
                q_rows = pl.ds(blk * qblk, qblk)
                k_rows = pl.ds(u0, nkeys)
                o, lse = attend(q_refs[g][0, q_rows, :], k_ref[0, k_rows, :].astype(BF16),
                                v_ref[0, k_rows, :].astype(BF16), bias)
                o_refs[g][0, q_rows, :] = o
            lse_ref[g, q_rows, :] = lse

    rows_per_step = REWEIGHT_ROWS
    steps_per_class = sub // rows_per_step

    def reweight(t, carry):
        cls = t // steps_per_class
        start = (t % steps_per_class) * rows_per_step
        nat = pl.ds(cls + ATTN_SPLIT * start, rows_per_step, stride=ATTN_SPLIT)
        grp = pl.ds(pl.multiple_of(cls * sub + start, rows_per_step), rows_per_step)
        lse = [lse_ref[g, grp if g in split_groups else nat, :] for g in range(N_GROUPS)]
        top = jnp.maximum(jnp.maximum(lse[0], lse[1]), lse[2])
        e = [jnp.exp(l - top) for l in lse]
        tot = e[0] + e[1] + e[2]
        for g in range(N_GROUPS):
            if g in split_groups:
                o = od_ref[split_groups.index(g), grp, :]
            else:
                o = o_refs[g][0, nat, :]
            o_refs[g][0, nat, :] = o * (e[g] / tot)
        return carry

    lax.fori_loop(0, ATTN_SPLIT * steps_per_class, reweight, 0)


def _attn_sample_kernel(q0_ref, q1_ref, q2_ref, kn_ref, vn_ref, kc_ref, vc_ref, o0_ref, o1_ref, o2_ref,
                        *, n, cache_len):
    q_refs = (q0_ref, q1_ref, q2_ref)
    o_refs = (o0_ref, o1_ref, o2_ref)
    per_head = N_GROUPS * n
    rows = N_KV_HEADS * per_head
    r = lax.broadcasted_iota(jnp.int32, (rows, 1), 0)
    grp = (r % per_head) // n
    tq = r % n
    dil = jnp.zeros_like(r)
    win = jnp.zeros_like(r)
    for g, (w_g, d_g) in enumerate(DIL_GROUPS):
        dil = jnp.where(grp == g, d_g, dil)
        win = jnp.where(grp == g, w_g, win)
    d_cache = (cache_len + tq) - lax.broadcasted_iota(jnp.int32, (1, cache_len), 1)
    valid_c = ((d_cache & (dil - 1)) == 0) & (d_cache <= win)
    d_new = tq - lax.broadcasted_iota(jnp.int32, (1, n), 1)
    valid_n = (d_new >= 0) & ((d_new & (dil - 1)) == 0) & (d_new <= win)
    own_lanes = (lax.broadcasted_iota(jnp.int32, (1, KV_DIM), 1) // HEAD_DIM) == (r // per_head)

    q = jnp.concatenate([q_ref[0] for q_ref in q_refs] * N_KV_HEADS, axis=0)
    q = jnp.where(own_lanes, q, 0.0).astype(BF16)
    sc = jnp.where(valid_c, _dot(q, kc_ref[0].astype(BF16)), -jnp.inf)
    sn = jnp.where(valid_n, _dot_nt(q, kn_ref[0].astype(BF16)), -jnp.inf)
    m = jnp.maximum(jnp.max(sc, axis=-1, keepdims=True), jnp.max(sn, axis=-1, keepdims=True))
    pc = jnp.exp(sc - m)
    pn = jnp.exp(sn - m)
    den = jnp.sum(pc, axis=-1, keepdims=True) + jnp.sum(pn, axis=-1, keepdims=True)
    out = (_dot_nt(pc.astype(BF16), vc_ref[0].astype(BF16)) + _dot(pn.astype(BF16), vn_ref[0].astype(BF16))) / den
    lse = m + jnp.log(den)
    acc = [None] * N_GROUPS
    for h in range(N_KV_HEADS):
        blocks = [slice(h * per_head + g * n, h * per_head + (g + 1) * n) for g in range(N_GROUPS)]
        lses = [lse[b] for b in blocks]
        top = jnp.maximum(jnp.maximum(lses[0], lses[1]), lses[2])
        e = [jnp.exp(l - top) for l in lses]
        tot = e[0] + e[1] + e[2]
        for g, b in enumerate(blocks):
            term = jnp.where(own_lanes[b], out[b] * (e[g] / tot), 0.0)
            acc[g] = term if acc[g] is None else acc[g] + term
    for g in range(N_GROUPS):
        o_refs[g][0] = acc[g]


def _params(semantics):
    return pltpu.CompilerParams(dimension_semantics=semantics, vmem_limit_bytes=VMEM_LIMIT)


def _row_spec(tm, width):
    return pl.BlockSpec((tm, width), lambda i: (i, 0))


def _const_spec(shape):
    return pl.BlockSpec(shape, lambda i: (0,) * len(shape), pipeline_mode=pl.Buffered(1))


def _with_riders(kernel_fn, n_in, n_out, n_riders):
    def kernel(*refs):
        ins, srcs = refs[:n_in], refs[n_in:n_in + n_riders]
        outs, dsts = refs[n_in + n_riders:n_in + n_riders + n_out], refs[n_in + n_riders + n_out:]
        kernel_fn(*ins, *outs)
        for src_ref, dst_ref in zip(srcs, dsts):
            dst_ref[...] = src_ref[...].astype(BF16)
    return kernel


def _rowwise_call(kernel_fn, name, rows, row_inputs, consts, tables, outs, seq_len=None, n_transposed=0,
                  row_tile=ROW_TILE, riders=()):
    tm = row_tile if rows % row_tile == 0 else rows
    if tables:
        tm = min(tm, tables[0].shape[0])
    assert rows % tm == 0 and all(t.shape[0] % tm == 0 for t in tables)
    in_specs = [_row_spec(tm, a.shape[1]) for a in row_inputs]
    in_specs += [_const_spec(c.shape) for c in consts]
    for t in tables:
        period = t.shape[0] // tm
        in_specs.append(pl.BlockSpec((tm, t.shape[1]), lambda i, period=period: (i % period, 0)))
    out_specs = [_row_spec(tm, width) for width, _ in outs]
    out_shape = [jax.ShapeDtypeStruct((rows, width), dtype) for width, dtype in outs]
    if n_transposed:
        assert seq_len % tm == 0
        per_seq = seq_len // tm
        out_specs += [pl.BlockSpec((1, KV_DIM, tm), lambda i: (i // per_seq, 0, i % per_seq))] * n_transposed
        out_shape += [jax.ShapeDtypeStruct((rows // seq_len, KV_DIM, seq_len), F32)] * n_transposed
    operands = [*row_inputs, *consts, *tables]
    if riders:
        kernel_fn = _with_riders(kernel_fn, len(operands), len(out_specs), len(riders))
    n_steps = rows // tm
    for stack, index in riders:
        r_rows, width = stack.shape[-2:]
        per_step = next(r for r in range(BF16_ROWS, r_rows + 1, BF16_ROWS)
                        if r_rows % r == 0 and r_rows // r <= n_steps)
        last = r_rows // per_step - 1
        in_specs.append(pl.BlockSpec((None,) * len(index) + (per_step, width),
                                     lambda i, index=index, last=last: (*index, jnp.minimum(i, last), 0)))
        out_specs.append(pl.BlockSpec((per_step, width), lambda i, last=last: (jnp.minimum(i, last), 0)))
        out_shape.append(jax.ShapeDtypeStruct((r_rows, width), BF16))
        operands.append(stack)
    return pl.pallas_call(
        kernel_fn,
        grid=(n_steps,),
        in_specs=in_specs,
        out_specs=out_specs,
        out_shape=out_shape,
        compiler_params=_params(("arbitrary",) if riders else ("parallel",)),
        name=name,
    )(*operands)


def _cast_kernel(*refs, windows):
    srcs, dsts = refs[:len(windows)], iter(refs[len(windows):])
    for src_ref, cols in zip(srcs, windows):
        for c0, c1 in cols:
            next(dsts)[...] = src_ref[:, c0:c1].astype(BF16)


def _cast_weights_call(items):
    in_specs, out_specs, out_shape, operands = [], [], [], []
    for arr, index, cols in items:
        rows, width = arr.shape[-2:]
        assert rows % (CAST_STEPS * BF16_ROWS) == 0
        blk = rows // CAST_STEPS
        in_specs.append(pl.BlockSpec((None,) * len(index) + (blk, width),
                                     lambda i, index=index: (*index, i, 0)))
        operands.append(arr)
        for c0, c1 in cols:
            out_specs.append(pl.BlockSpec((blk, c1 - c0), lambda i: (i, 0)))
            out_shape.append(jax.ShapeDtypeStruct((rows, c1 - c0), BF16))
    return pl.pallas_call(
        functools.partial(_cast_kernel, windows=[cols for _, _, cols in items]),
        grid=(CAST_STEPS,),
        in_specs=in_specs,
        out_specs=out_specs,
        out_shape=out_shape,
        compiler_params=_params(("parallel",)),
        name="cast_weights",
    )(*operands)


def _gla_level_masks(chunk):
    i = np.arange(chunk)[:, None]
    j = np.arange(chunk)[None, :]
    levels = [i == j]
    s = 1
    while s < chunk:
        levels.append(((i ^ j) < 2 * s) & ((i & s) != 0) & ((j & s) == 0))
        s *= 2
    return jnp.asarray(np.stack(levels).astype(np.float32))


def _gla_rec_call(q, k, v, la, s0):
    batch, n, _ = q.shape
    chunk = min(GLA_CHUNK, n)
    assert n % chunk == 0 and chunk & (chunk - 1) == 0
    has_state = s0 is not None
    bb = GLA_BATCH_BLOCK if batch % GLA_BATCH_BLOCK == 0 else 1
    seq_spec = lambda width: pl.BlockSpec((bb, chunk, width), lambda b, c: (b, c, 0))
    state_spec = pl.BlockSpec((bb, GLA_HEADS, GLA_DK, GLA_DV), lambda b, c: (b, 0, 0, 0))
    masks = _gla_level_masks(chunk)
    in_specs = [seq_spec(GLA_QK), seq_spec(GLA_QK), seq_spec(GLA_V), seq_spec(GLA_QK),
                pl.BlockSpec(masks.shape, lambda b, c: (0, 0, 0), pipeline_mode=pl.Buffered(1))]
    args = [q, k, v, la, masks]
    if has_state:
        in_specs.append(state_spec)
        args.append(s0)
    return pl.pallas_call(
        functools.partial(_gla_rec_kernel, chunk=chunk, has_state=has_state),
        grid=(batch // bb, n // chunk),
        in_specs=in_specs,
        out_specs=[seq_spec(GLA_V), state_spec],
        out_shape=[jax.ShapeDtypeStruct((batch, n, GLA_V), F32),
                   jax.ShapeDtypeStruct((batch, GLA_HEADS, GLA_DK, GLA_DV), F32)],
        compiler_params=_params(("parallel", "arbitrary")),
        name="gla_rec",
    )(*args)


def _attn_window_bias(n):
    out = []
    for win, dil in DIL_GROUPS:
        qblk = min(ATTN_QBLOCK, n // dil)
        r = np.arange(2 * qblk)[:, None] % qblk
        c = np.arange(2 * ATTN_QBLOCK)[None, :]
        for shift in (0, qblk):
            delta = shift + r - c
            out.append(np.where((delta >= 0) & (delta <= win // dil), 0.0, -np.inf))
    return jnp.asarray(np.stack(out).astype(np.float32))


def _attn_prompt_call(qs, k, v):
    batch, n, _ = k.shape
    assert n % (ATTN_QBLOCK * max(d for _, d in DIL_GROUPS)) == 0
    n_split = sum(1 for _, d in DIL_GROUPS if d % ATTN_SPLIT == 0)
    spec = pl.BlockSpec((1, n, LANES), lambda b, p: (b, 0, p))
    bias = _attn_window_bias(n)
    bias_spec = pl.BlockSpec(bias.shape, lambda b, p: (0, 0, 0), pipeline_mode=pl.Buffered(1))
    return pl.pallas_call(
        functools.partial(_attn_prompt_kernel, n=n),
        grid=(batch, KV_DIM // LANES),
        in_specs=[spec] * (N_GROUPS + 2) + [bias_spec],
        out_specs=[spec] * N_GROUPS,
        out_shape=[jax.ShapeDtypeStruct((batch, n, KV_DIM), F32)] * N_GROUPS,
        scratch_shapes=[pltpu.VMEM((n_split, n, LANES), F32), pltpu.VMEM((n, LANES), F32),
                        pltpu.VMEM((n, LANES), F32), pltpu.VMEM((n_split, n, LANES), F32),
                        pltpu.VMEM((N_GROUPS, n, LANES), F32)],
        compiler_params=_params(("parallel", "parallel")),
        name="attn_prompt",
    )(*qs, k, v, bias)


def _attn_sample_call(qs, k_new, v_new, k_cache, v_cache):
    batch, n, _ = k_new.shape
    cache_len = k_cache.shape[2]
    new_spec = pl.BlockSpec((1, n, KV_DIM), lambda b: (b, 0, 0))
    cache_spec = pl.BlockSpec((1, KV_DIM, cache_len), lambda b: (b, 0, 0))
    return pl.pallas_call(
        functools.partial(_attn_sample_kernel, n=n, cache_len=cache_len),
        grid=(batch,),
        in_specs=[new_spec] * (N_GROUPS + 2) + [cache_spec] * 2,
        out_specs=[new_spec] * N_GROUPS,
        out_shape=[jax.ShapeDtypeStruct((batch, n, KV_DIM), F32)] * N_GROUPS,
        compiler_params=_params(("parallel",)),
        name="attn_sample",
    )(*qs, k_new, v_new, k_cache, v_cache)


def _rope_tables(pos0, n, reps):
    half = HEAD_DIM // 2
    inv = ROPE_THETA ** (-np.arange(half, dtype=np.float64) / half)
    ang = (pos0 + np.arange(n, dtype=np.float64))[:, None] * inv[None, :]
    cos = np.concatenate([np.cos(ang), np.cos(ang)], axis=-1)
    sin = np.concatenate([-np.sin(ang), np.sin(ang)], axis=-1)
    cos = np.tile(cos, (reps, N_KV_HEADS)).astype(np.float32)
    sin = np.tile(sin, (reps, N_KV_HEADS)).astype(np.float32)
    return jnp.asarray(cos), jnp.asarray(sin)


def _head_mean_matrix():
    idx = np.arange(KV_DIM) // HEAD_DIM
    return jnp.asarray((idx[:, None] == idx[None, :]).astype(np.float32) / HEAD_DIM, dtype=BF16)


def _run_group(x, pos0, s0, caches, w, ffn_bf16=None):
    batch, n, _ = x.shape
    rows = batch * n
    x = x.reshape(rows, D_MODEL)
    reps = 1 if n % ROW_TILE == 0 else batch
    cos, sin = _rope_tables(pos0, n, reps)
    avg = _head_mean_matrix()

    gain = w["gain"]
    later = [(l, i) for l in range(2) for i in range(2)][1:]
    riders = [(stack, m) for stack in w["ffn_f32"] for m in later] if ffn_bf16 is None else []
    seq = lambda a: a.reshape(batch, n, a.shape[-1])
    wide, qk, kv = (D_MODEL, F32), (GLA_QK, F32), (KV_DIM, F32)

    x1, q, k, v, la, *converted = _rowwise_call(
        _ffn_gla_in_kernel, "ffn_gla_in", rows, [x],
        [gain[0][0], *(ffn_bf16 or w["ffn_first"])[0, 0], gain[0][1], w["gla_mix"], w["gla_g2"], w["gla_bg"]],
        [], [wide, qk, qk, (GLA_V, BF16), qk], riders=riders)
    if ffn_bf16 is None:
        ffn_bf16 = dict(w["ffn_first"])
        ffn_bf16.update({m: (converted[j], converted[len(later) + j]) for j, m in enumerate(later)})
    ffn_in = [[ffn_bf16[l, i][0] for i in range(2)] for l in range(2)]
    ffn_out = [[ffn_bf16[l, i][1] for i in range(2)] for l in range(2)]
    o, s_fin = _gla_rec_call(seq(q), seq(k), seq(v), seq(la), s0)
    transposed = n % ROW_TILE == 0
    x3, k_new, v_new, *kv_t = _rowwise_call(
        _gla_out_ffn_kv_kernel, "gla_out_ffn_kv", rows, [o.reshape(rows, GLA_V), x1],
        [gain[0][1], w["gla_r"], w["gla_norm"], w["gla_out"], gain[0][2], ffn_in[0][1], ffn_out[0][1],
         w["kv_gain"], w["kv_w"], w["k_norm"], avg], [cos, sin],
        [wide, kv, kv], seq_len=n, n_transposed=2 if transposed else 0)

    x4, *qs = _rowwise_call(
        _ffn_q_kernel, "ffn_q", rows, [x3],
        [avg, gain[1][0], ffn_in[1][0], ffn_out[1][0], gain[1][1], w["attn_q"], w["q_norm"]], [cos, sin],
        [wide] + [kv] * N_GROUPS)

    qs = [seq(a) for a in qs]
    if caches is None:
        os_ = _attn_prompt_call(qs, seq(k_new), seq(v_new))
    else:
        os_ = _attn_sample_call(qs, seq(k_new), seq(v_new), *caches)
    (x,) = _rowwise_call(
        _attn_out_ffn_kernel, "attn_out_ffn", rows, [a.reshape(rows, KV_DIM) for a in os_] + [x4],
        [w["attn_out"], gain[1][2], ffn_in[1][1], ffn_out[1][1]], [],
        [wide], row_tile=LAST_ROW_TILE)

    if transposed:
        k_out, v_out = (jnp.transpose(a.reshape(batch, N_KV_HEADS, HEAD_DIM, n), (0, 3, 1, 2))
                        for a in kv_t)
    else:
        k_out, v_out = (a.reshape(batch, n, N_KV_HEADS, HEAD_DIM) for a in (k_new, v_new))
    return (x.reshape(batch, n, D_MODEL), s_fin[None], k_out, v_out), ffn_bf16


def kernel(x_prompt, x_sample, state_gla, cache_k_win, cache_v_win, norm_gains, ffn_w_in, ffn_w_out,
           gla_w_in, gla_w_gate2, gla_b_gate, gla_out_norm, gla_w_out, kv_norm, kv_w, k_norm,
           attn_w_q, q_norm, attn_w_out):
    assert norm_gains.shape[0] == 2 and gla_w_in.shape[0] == 1 and attn_w_q.shape[0] == 1
    row = lambda a: a.reshape(1, -1)
    whole = lambda a: [(0, a.shape[-1])]
    gate0 = 2 * GLA_QK + GLA_V
    pad = LANES - GLA_RANK
    first_in, first_out, gla_out, kv_w16, attn_q, attn_out = _cast_weights_call([
        (ffn_w_in, (0, 0), whole(ffn_w_in)), (ffn_w_out, (0, 0), whole(ffn_w_out)),
        (gla_w_out, (0,), whole(gla_w_out)), (kv_w, (), whole(kv_w)),
        (attn_w_q, (0,), whole(attn_w_q)), (attn_w_out, (0,), whole(attn_w_out))])
    gla_mix = gla_w_in[0, :, :gate0 + LANES].astype(BF16)
    gla_r = gla_w_in[0, :, gate0 + GLA_RANK:].astype(BF16)
    w = {
        "gain": [[row(norm_gains[l, i]) for i in range(3)] for l in range(2)],
        "ffn_first": {(0, 0): (first_in, first_out)},
        "ffn_f32": (ffn_w_in, ffn_w_out),
        "gla_mix": gla_mix,
        "gla_r": gla_r,
        "gla_g2": jnp.pad(gla_w_gate2[0].astype(BF16), ((0, pad), (0, 0))),
        "gla_bg": row(gla_b_gate[0]),
        "gla_norm": row(gla_out_norm[0]),
        "gla_out": gla_out,
        "kv_gain": row(kv_norm),
        "kv_w": kv_w16,
        "k_norm": row(jnp.tile(k_norm, N_KV_HEADS)),
        "attn_q": attn_q,
        "q_norm": row(jnp.tile(q_norm[0], N_KV_HEADS)),
        "attn_out": attn_out,
    }
    caches = tuple(jnp.transpose(c, (0, 2, 3, 1)).reshape(c.shape[0], KV_DIM, c.shape[1])
                   for c in (cache_k_win, cache_v_win))

    (y_p, s_p, k_p, v_p), ffn_bf16 = _run_group(x_prompt, 0, None, None, w)
    (y_s, s_s, k_s, v_s), _ = _run_group(x_sample, PAST_LEN, state_gla[0], caches, w, ffn_bf16)
    keep = min(MAX_WINDOW, x_prompt.shape[1])
    return (y_p, y_s, s_p, s_s, k_p[:, -keep:], v_p[:, -keep:], k_s, v_s)
```
